```python
import math
import jax, jax.numpy as jnp
from jax import lax
import numpy as np

D_MODEL = 1024
BATCH = 8
SEQ = 2048
DEPTH = 2
DEC_BATCH = 128
DEC_SEQ = 1
PAST_LEN = 16384
PAGE_SIZE = 128

EPS = 1e-6
D_MIX = D_MODEL
SSD_HEADS = 6
SSD_HEAD_DIM = 64
SSD_WIDTH = SSD_HEADS * SSD_HEAD_DIM
SSD_GROUPS = 2
SSD_HPG = SSD_HEADS // SSD_GROUPS
SSD_STATE = 128
SSD_CONV = 4
SSD_CHUNK = 128
SSD_CONV_CH = SSD_WIDTH + 2 * SSD_GROUPS * SSD_STATE
MLA_HEADS = 6
MLA_NOPE = 64
MLA_ROPE = 32
MLA_QK = MLA_NOPE + MLA_ROPE
MLA_V = 64
MLA_WIDTH = MLA_HEADS * MLA_V
Q_LORA = 256
KV_LORA = 128
ROPE_THETA = 10000.0
ATTN_SCALE = MLA_QK ** -0.5
Q_BLOCK = 128
POOL_WINDOWS = (2, 4, 8, 16)
POOL_GROUPS = 4
POOL_GROUP_DIM = 64
POOL_WIDTH = POOL_GROUPS * POOL_GROUP_DIM
POOL_BUF = 15
D_FF = -(-8 * D_MODEL // (3 * 256)) * 256
IN_SIZES = (SSD_WIDTH, SSD_CONV_CH, SSD_HEADS, Q_LORA, KV_LORA, MLA_ROPE, POOL_WIDTH)
IN_COLS = SSD_WIDTH + SSD_CONV_CH + SSD_HEADS + Q_LORA + KV_LORA + MLA_ROPE + POOL_WIDTH

kernel_name = 'hymba_ssd_mla_pool_decoder_step'


def rmsnorm(x, g):
    xf = x.astype(jnp.float32)
    y = xf * lax.rsqrt(jnp.mean(xf * xf, axis=-1, keepdims=True) + EPS)
    return (y * g.astype(jnp.float32)).astype(x.dtype)


def rope(x, pos):
    half = MLA_ROPE // 2
    inv = 1.0 / (ROPE_THETA ** (jnp.arange(half, dtype=jnp.float32) / half))
    ang = pos.astype(jnp.float32)[:, None] * inv[None, :]
    cos = jnp.cos(ang)[:, None, :]
    sin = jnp.sin(ang)[:, None, :]
    xf = x.astype(jnp.float32)
    x1, x2 = xf[..., :half], xf[..., half:]
    return jnp.concatenate([x1 * cos - x2 * sin, x2 * cos + x1 * sin], -1).astype(x.dtype)


def segsum(a):
    T = a.shape[-1]
    x = jnp.broadcast_to(a[..., :, None], a.shape + (T,))
    x = jnp.where(jnp.tril(jnp.ones((T, T), bool), -1), x, 0.0)
    s = jnp.cumsum(x, axis=-2)
    return jnp.where(jnp.tril(jnp.ones((T, T), bool)), s, -jnp.inf)


def ssd_scan(x, a, B, C, h0):
    b_, L, H, P = x.shape
    N = B.shape[-1]
    T = SSD_CHUNK if L % SSD_CHUNK == 0 else L
    nc = L // T
    xc = x.reshape(b_, nc, T, H, P)
    Bc = B.reshape(b_, nc, T, H, N)
    Cc = C.reshape(b_, nc, T, H, N)
    ac = a.reshape(b_, nc, T, H).transpose(0, 3, 1, 2)
    acum = jnp.cumsum(ac, axis=-1)
    G = jnp.einsum('bclhn,bcshn->bhcls', Cc, Bc) * jnp.exp(segsum(ac))
    y_diag = jnp.einsum('bhcls,bcshp->bclhp', G, xc)
    decay_states = jnp.exp(acum[..., -1:] - acum)
    states = jnp.einsum('bclhn,bhcl,bclhp->bchpn', Bc, decay_states, xc)
    states = jnp.concatenate([h0[:, None], states], axis=1)
    chunk_tot = jnp.pad(acum[..., -1], ((0, 0), (0, 0), (1, 0)))
    states = jnp.einsum('bhzc,bchpn->bzhpn', jnp.exp(segsum(chunk_tot)), states)
    prev, final = states[:, :-1], states[:, -1]
    y_off = jnp.einsum('bclhn,bchpn,bhcl->bclhp', Cc, prev, jnp.exp(acum))
    return (y_diag + y_off).reshape(b_, L, H, P), final


def causal_conv(xbc, buf, w, b):
    L = xbc.shape[1]
    xp = jnp.concatenate([buf, xbc], axis=1)
    y = b
    for k in range(SSD_CONV):
        y = y + xp[:, k:k + L] * w[k]
    return jax.nn.silu(y), xp[:, -(SSD_CONV - 1):]


def ssd_mixer(z, xbc, dt_raw, conv_buf, h0, conv_w, conv_b, dt_bias, a_log, d_skip, g_norm):
    b_, L, _ = z.shape
    f32 = jnp.float32
    xbc, conv_new = causal_conv(xbc, conv_buf, conv_w, conv_b)
    gn = SSD_GROUPS * SSD_STATE
    xs = xbc[..., :SSD_WIDTH].reshape(b_, L, SSD_HEADS, SSD_HEAD_DIM).astype(f32)
    Bm = xbc[..., SSD_WIDTH:SSD_WIDTH + gn].reshape(b_, L, SSD_GROUPS, SSD_STATE)
    Cm = xbc[..., SSD_WIDTH + gn:].reshape(b_, L, SSD_GROUPS, SSD_STATE)
    Bh = jnp.repeat(Bm, SSD_HPG, axis=2).astype(f32)
    Ch = jnp.repeat(Cm, SSD_HPG, axis=2).astype(f32)
    dt = jax.nn.softplus((dt_raw + dt_bias).astype(f32))
    A = -jnp.exp(a_log.astype(f32))
    y, hT = ssd_scan(xs * dt[..., None], dt * A, Bh, Ch, h0.astype(f32))
    y = y + d_skip.astype(f32)[:, None] * xs
    y = y.reshape(b_, L, SSD_GROUPS, SSD_WIDTH // SSD_GROUPS)
    zg = jax.nn.silu(z.astype(f32)).reshape(b_, L, SSD_GROUPS, SSD_WIDTH // SSD_GROUPS)
    y = rmsnorm(y * zg, g_norm.reshape(SSD_GROUPS, SSD_WIDTH // SSD_GROUPS))
    return y.reshape(b_, L, SSD_WIDTH).astype(z.dtype), hT.astype(h0.dtype), conv_new


def pool_mixer(u, buf, pos, w_pool, scale):
    b_, L, C = u.shape
    f32 = jnp.float32
    up = jnp.concatenate([buf, u], axis=1).astype(f32)
    S = jnp.concatenate([jnp.zeros((b_, 1, C), f32), jnp.cumsum(up, axis=1)], axis=1)
    end = S[:, POOL_BUF + 1:]
    means = []
    for g, w in enumerate(POOL_WINDOWS):
        sl = slice(g * POOL_GROUP_DIM, (g + 1) * POOL_GROUP_DIM)
        start = S[:, POOL_BUF + 1 - w:POOL_BUF + 1 - w + L, sl]
        cnt = jnp.minimum(pos + 1, w).astype(f32)[None, :, None]
        means.append((end[..., sl] - start) / cnt)
    pooled = jnp.concatenate(means, axis=-1) - u.astype(f32)
    out = jnp.einsum('blgc,gcd->blgd', pooled.reshape(b_, L, POOL_GROUPS, POOL_GROUP_DIM), w_pool.astype(f32))
    out = out.reshape(b_, L, C) * scale.astype(f32)
    return out.astype(u.dtype), up[:, -POOL_BUF:].astype(u.dtype)


def mla_keys(lat, kpe_r, w_k_up, g_k):
    k_nope = jnp.einsum('...r,rhd->...hd', lat, w_k_up)
    k_pe = jnp.broadcast_to(kpe_r[..., None, :], k_nope.shape[:-1] + (MLA_ROPE,))
    return rmsnorm(jnp.concatenate([k_nope, k_pe], axis=-1), g_k)


def mla_attend_prompt(q, lat, kpe_r, w_k_up, w_v_up, g_k):
    b_, L = q.shape[0], q.shape[1]
    k = mla_keys(lat, kpe_r, w_k_up, g_k)
    v = jnp.einsum('blr,rhd->blhd', lat, w_v_up)
    nb = L // Q_BLOCK
    qb = q.reshape(b_, nb, Q_BLOCK, MLA_HEADS, MLA_QK).transpose(1, 0, 2, 3, 4)
    kpos = jnp.arange(L)

    def block(args):
        qi, i = args
        s = jnp.einsum('bqhd,bkhd->bhqk', qi, k, preferred_element_type=jnp.float32) * ATTN_SCALE
        qpos = i * Q_BLOCK + jnp.arange(Q_BLOCK)
        s = jnp.where(kpos[None, :] <= qpos[:, None], s, -jnp.inf)
        p = jax.nn.softmax(s, axis=-1).astype(v.dtype)
        return jnp.einsum('bhqk,bkhd->bqhd', p, v)

    o = lax.map(block, (qb, jnp.arange(nb)))
    return o.transpose(1, 0, 2, 3, 4).reshape(b_, L, MLA_WIDTH)


def mla_attend_sample(q, lat_new, kpe_new, cache_lat, cache_kpe, page_table, li, w_k_up, w_v_up, g_k):
    b_, T = q.shape[0], q.shape[1]
    past = page_table.shape[1] * PAGE_SIZE
    kpos = jnp.arange(past + T)
    qpos = past + jnp.arange(T)
    mask = kpos[None, :] <= qpos[:, None]

    def one(args):
        pt, qs, ln, kn = args
        lat = jnp.concatenate([cache_lat[li, pt].reshape(past, KV_LORA), ln], axis=0)
        kp = jnp.concatenate([cache_kpe[li, pt].reshape(past, MLA_ROPE), kn], axis=0)
        k = mla_keys(lat, kp, w_k_up, g_k)
        s = jnp.einsum('thd,shd->hts', qs, k, preferred_element_type=jnp.float32) * ATTN_SCALE
        s = jnp.where(mask[None], s, -jnp.inf)
        p = jax.nn.softmax(s, axis=-1).astype(lat.dtype)
        o_lat = jnp.einsum('hts,sr->thr', p, lat)
        return jnp.einsum('thr,rhd->thd', o_lat, w_v_up)

    o = lax.map(one, (page_table, q, lat_new, kpe_new))
    return o.reshape(b_, T, MLA_WIDTH)


def trunk_layer(x, c, pos0, lp, h0, conv_buf, pool_buf, attend):
    b_, L, _ = x.shape
    pos = pos0 + jnp.arange(L, dtype=jnp.int32)
    mod = (jax.nn.silu(c) @ lp['w_ada'] + lp['b_ada']).reshape(b_, 6, D_MODEL)[:, :, None, :]
    shift1, scale1, gate1, shift2, scale2, gate2 = (mod[:, i] for i in range(6))
    h = rmsnorm(x, lp['g_norm1']) * (1.0 + scale1) + shift1
    proj = h @ lp['w_in']
    offs = np.cumsum(IN_SIZES)[:-1].tolist()
    z, xbc, dt_raw, cq, ckv, kpe, u = jnp.split(proj, offs, axis=-1)
    ssd_out, hT, conv_new = ssd_mixer(z, xbc, dt_raw, conv_buf, h0, lp['conv_w'], lp['conv_b'],
                                      lp['dt_bias'], lp['a_log'], lp['d_skip'], lp['g_ssd_norm'])
    q = jnp.einsum('blr,rhd->blhd', rmsnorm(cq, lp['g_q_lora']), lp['w_q_up'])
    q = jnp.concatenate([q[..., :MLA_NOPE], rope(q[..., MLA_NOPE:], pos)], axis=-1)
    q = rmsnorm(q, lp['g_qk_q'])
    lat = rmsnorm(ckv, lp['g_kv_lora'])
    kpe_r = rope(kpe[:, :, None, :], pos)[:, :, 0, :]
    mla_out = rmsnorm(attend(q, lat, kpe_r), lp['g_mla_out'])
    pool_out, pool_new = pool_mixer(u, pool_buf, pos, lp['w_pool'], lp['pool_scale'])
    mix = jnp.concatenate([ssd_out, mla_out, pool_out], axis=-1) @ lp['w_out']
    x = x + gate1 * mix
    h2 = rmsnorm(x, lp['g_norm2']) * (1.0 + scale2) + shift2
    ffn = (jax.nn.silu(h2 @ lp['w_gate']) * (h2 @ lp['w_up'])) @ lp['w_down']
    x = x + gate2 * ffn
    return x, lat, kpe_r, hT, conv_new, pool_new


def setup_inputs(seed: int = 0) -> dict:
    key = jax.random.key(seed)
    keys = iter(jax.random.split(key, 48))
    f32 = jnp.float32

    def normal(shape, scale=1.0):
        return jax.random.normal(next(keys), shape, f32) * scale

    def gain(shape):
        return 1.0 + 0.02 * normal(shape)

    n_pages = PAST_LEN // PAGE_SIZE
    n_pool = (DEC_BATCH * n_pages * 5) // 4
    perm = jax.random.permutation(next(keys), n_pool)
    page_table = perm[:DEC_BATCH * n_pages].reshape(DEC_BATCH, n_pages).astype(jnp.int32)
    dt0 = jnp.exp(jax.random.uniform(next(keys), (DEPTH, SSD_HEADS), f32, math.log(1e-3), math.log(1e-1)))
    dt_bias = dt0 + jnp.log(-jnp.expm1(-dt0))
    a_log = jnp.log(jax.random.uniform(next(keys), (DEPTH, SSD_HEADS), f32, 1.0, 16.0))
    return {
        'x_prompt': normal((BATCH, SEQ, D_MODEL)),
        'x_sample': normal((DEC_BATCH, DEC_SEQ, D_MODEL)),
        'cache_kv_latent': normal((DEPTH, n_pool, PAGE_SIZE, KV_LORA)),
        'cache_k_rope': normal((DEPTH, n_pool, PAGE_SIZE, MLA_ROPE)),
        'state_ssm': normal((DEPTH, DEC_BATCH, SSD_HEADS, SSD_HEAD_DIM, SSD_STATE), 0.5),
        'state_conv': normal((DEPTH, DEC_BATCH, SSD_CONV - 1, SSD_CONV_CH)),
        'state_pool': normal((DEPTH, DEC_BATCH, POOL_BUF, POOL_WIDTH)),
        'page_table': page_table,
        'c_prompt': normal((BATCH, D_MODEL)),
        'c_sample': normal((DEC_BATCH, D_MODEL)),
        'w_ada': normal((DEPTH, D_MODEL, 6 * D_MODEL), 0.5 * D_MODEL ** -0.5),
        'b_ada': normal((DEPTH, 6 * D_MODEL), 0.01),
        'g_norm1': gain((DEPTH, D_MODEL)),
        'w_in': normal((DEPTH, D_MODEL, IN_COLS), D_MODEL ** -0.5),
        'conv_w': normal((DEPTH, SSD_CONV, SSD_CONV_CH), SSD_CONV ** -0.5),
        'conv_b': normal((DEPTH, SSD_CONV_CH), 0.01),
        'dt_bias': dt_bias,
        'a_log': a_log,
        'd_skip': gain((DEPTH, SSD_HEADS)),
        'g_ssd_norm': gain((DEPTH, SSD_WIDTH)),
        'g_q_lora': gain((DEPTH, Q_LORA)),
        'w_q_up': normal((DEPTH, Q_LORA, MLA_HEADS, MLA_QK), Q_LORA ** -0.5),
        'g_kv_lora': gain((DEPTH, KV_LORA)),
        'w_k_up': normal((DEPTH, KV_LORA, MLA_HEADS, MLA_NOPE), KV_LORA ** -0.5),
        'w_v_up': normal((DEPTH, KV_LORA, MLA_HEADS, MLA_V), KV_LORA ** -0.5),
        'g_qk_q': gain((DEPTH, MLA_QK)),
        'g_qk_k': gain((DEPTH, MLA_QK)),
        'g_mla_out': gain((DEPTH, MLA_WIDTH)),
        'w_pool': normal((DEPTH, POOL_GROUPS, POOL_GROUP_DIM, POOL_GROUP_DIM), POOL_GROUP_DIM ** -0.5),
        'pool_scale': gain((DEPTH, POOL_WIDTH)),
        'w_out': normal((DEPTH, D_MIX, D_MODEL), D_MIX ** -0.5),
        'g_norm2': gain((DEPTH, D_MODEL)),
        'w_gate': normal((DEPTH, D_MODEL, D_FF), D_MODEL ** -0.5),
        'w_up': normal((DEPTH, D_MODEL, D_FF), D_MODEL ** -0.5),
        'w_down': normal((DEPTH, D_FF, D_MODEL), D_FF ** -0.5),
    }


def reference(x_prompt, x_sample, cache_kv_latent, cache_k_rope, state_ssm, state_conv, state_pool, page_table,
              c_prompt, c_sample, w_ada, b_ada, g_norm1, w_in, conv_w, conv_b, dt_bias, a_log, d_skip, g_ssd_norm,
              g_q_lora, w_q_up, g_kv_lora, w_k_up, w_v_up, g_qk_q, g_qk_k, g_mla_out, w_pool, pool_scale,
              w_out, g_norm2, w_gate, w_up, w_down):
    yp, ys = x_prompt, x_sample
    bp = x_prompt.shape[0]
    dtype = x_prompt.dtype
    p_new = [[] for _ in range(5)]
    s_new = [[] for _ in range(5)]
    for li in range(DEPTH):
        lp = dict(w_ada=w_ada[li], b_ada=b_ada[li], g_norm1=g_norm1[li], w_in=w_in[li], conv_w=conv_w[li],
                  conv_b=conv_b[li], dt_bias=dt_bias[li], a_log=a_log[li], d_skip=d_skip[li],
                  g_ssd_norm=g_ssd_norm[li], g_q_lora=g_q_lora[li], w_q_up=w_q_up[li], g_kv_lora=g_kv_lora[li],
                  g_qk_q=g_qk_q[li], g_mla_out=g_mla_out[li], w_pool=w_pool[li], pool_scale=pool_scale[li],
                  w_out=w_out[li], g_norm2=g_norm2[li], w_gate=w_gate[li], w_up=w_up[li], w_down=w_down[li])
        wk, wv, gk = w_k_up[li], w_v_up[li], g_qk_k[li]

        def attend_prompt(q, lat, kp, wk=wk, wv=wv, gk=gk):
            return mla_attend_prompt(q, lat, kp, wk, wv, gk)

        def attend_sample(q, lat, kp, wk=wk, wv=wv, gk=gk, li=li):
            return mla_attend_sample(q, lat, kp, cache_kv_latent, cache_k_rope, page_table, li, wk, wv, gk)

        h0 = jnp.zeros((bp, SSD_HEADS, SSD_HEAD_DIM, SSD_STATE), dtype)
        cb0 = jnp.zeros((bp, SSD_CONV - 1, SSD_CONV_CH), dtype)
        pb0 = jnp.zeros((bp, POOL_BUF, POOL_WIDTH), dtype)
        yp, *pst = trunk_layer(yp, c_prompt, 0, lp, h0, cb0, pb0, attend_prompt)
        ys, *sst = trunk_layer(ys, c_sample, PAST_LEN, lp, state_ssm[li], state_conv[li], state_pool[li],
                               attend_sample)
        for lst, v in zip(p_new, pst):
            lst.append(v)
        for lst, v in zip(s_new, sst):
            lst.append(v)
    p_lat, p_kpe, p_ssm, p_conv, p_pool = (jnp.stack(v, axis=0) for v in p_new)
    s_lat, s_kpe, s_ssm, s_conv, s_pool = (jnp.stack(v, axis=0) for v in s_new)
    return (yp, ys, p_lat, p_kpe, p_ssm, p_conv, p_pool, s_lat, s_kpe, s_ssm, s_conv, s_pool)
```

```python
import functools
import math

import jax
import jax.numpy as jnp
import numpy as np
from jax import lax
from jax.experimental import pallas as pl
from jax.experimental.pallas import tpu as pltpu

f32 = jnp.float32
bf16 = jnp.bfloat16
HIGHEST = lax.Precision.HIGHEST

EPS = 1e-6
PAGE = 128
NH = 6
HD = 64
NOPE = 64
ROPE = 32
QK = NOPE + ROPE
SSD_W = NH * HD
SSD_G = 2
SSD_N = 128
CONV_K = 4
CONV_CH = SSD_W + 2 * SSD_G * SSD_N
Q_LORA = 256
KV_LORA = 128
POOL_W = 256
POOL_BUF = 15
ROPE_THETA = 10000.0
ATTN_SCALE = QK ** -0.5
LANE = 128
HPAD = 16
VMEM_LIMIT = 56 * 1024 * 1024

_C_Z, _C_XBC, _C_CQ, _C_CKV, _C_U, _C_KA, _C_KB, _C_DT, _C_END = 0, 384, 1280, 1536, 1664, 1920, 2048, 2176, 2304


def _cparams(sem):
    return pltpu.CompilerParams(dimension_semantics=sem, vmem_limit_bytes=VMEM_LIMIT)


def _silu(x):
    return x * jax.nn.sigmoid(x)


def _softplus(x):
    return jnp.maximum(x, 0.0) + jnp.log1p(jnp.exp(-jnp.abs(x)))


def _rms(x, g, n):
    ms = jnp.sum(x * x, axis=-1, keepdims=True) * (1.0 / n)
    return x * lax.rsqrt(ms + EPS) * g


def _dot(a, b):
    return jnp.dot(a, b, preferred_element_type=f32)


def _dot_nt(a, b):
    return lax.dot_general(a, b, (((1,), (1,)), ((), ())), preferred_element_type=f32)


def _dot_tn(a, b):
    return lax.dot_general(a, b, (((0,), (0,)), ((), ())), preferred_element_type=f32)


def _dot_exact(a, b):
    return jnp.dot(a, b, precision=HIGHEST, preferred_element_type=f32)


def _head_expand():
    r = lax.broadcasted_iota(jnp.int32, (LANE, SSD_W), 0)
    c = lax.broadcasted_iota(jnp.int32, (LANE, SSD_W), 1)
    return jnp.where(c // HD == r, 1.0, 0.0).astype(f32)


def _ada_kernel(c_ref, w_ref, b_ref, o_ref):
    s = _silu(c_ref[...]).astype(bf16)
    o_ref[0] = _dot(s, w_ref[0].astype(bf16)) + b_ref[0]


def _ada_mod(c_all, w_ada, b_ada):
    depth, d, n6 = w_ada.shape
    rows = c_all.shape[0]
    tn = 1536
    return pl.pallas_call(
        _ada_kernel,
        grid=(depth, n6 // tn),
        in_specs=[pl.BlockSpec((rows, d), lambda l, j: (0, 0)),
                  pl.BlockSpec((1, d, tn), lambda l, j: (l, 0, j)),
                  pl.BlockSpec((1, 1, tn), lambda l, j: (l, 0, j))],
        out_specs=pl.BlockSpec((1, rows, tn), lambda l, j: (l, 0, j)),
        out_shape=jax.ShapeDtypeStruct((depth, rows, n6), f32),
        compiler_params=_cparams(("arbitrary", "arbitrary")),
        name="ada_mod",
    )(c_all, w_ada, b_ada.reshape(depth, 1, n6))


def _inproj_kernel(*refs, with_kv):
    (x_ref, sh_ref, sc_ref, g1_ref, w_ref, gq_ref, wq_ref, cosq_ref, sinq_ref, gqk_ref, gkv_ref,
     cosk_ref, sink_ref) = refs[:13]
    if with_kv:
        wk_ref, gk_ref, wv_ref = refs[13:16]
        z_ref, xbc_ref, dt_ref, u_ref, lat_ref, kpe_ref, q_ref, k_ref, v_ref = refs[16:]
    else:
        z_ref, xbc_ref, dt_ref, u_ref, lat_ref, kpe_ref, q_ref, kpe128_ref = refs[13:]
    d = x_ref.shape[-1]
    h = _rms(x_ref[0], g1_ref[...], d) * (1.0 + sc_ref[0]) + sh_ref[0]
    proj = _dot(h.astype(bf16), w_ref[...])
    z_ref[0] = proj[:, _C_Z:_C_XBC]
    xbc_ref[0] = proj[:, _C_XBC:_C_CQ]
    u_ref[0] = proj[:, _C_U:_C_KA]
    dt_ref[0] = proj[:, _C_DT:_C_END]
    cqn = _rms(proj[:, _C_CQ:_C_CKV], gq_ref[...], Q_LORA).astype(bf16)
    qq = _dot(cqn, wq_ref[...])
    cosq = cosq_ref[...]
    sinq = sinq_ref[...]
    gqk = gqk_ref[...]
    for hh in range(NH):
        qh = qq[:, LANE * hh:LANE * (hh + 1)] * cosq + qq[:, NH * LANE + LANE * hh:NH * LANE + LANE * (hh + 1)] * sinq
        q_ref[0, :, LANE * hh:LANE * (hh + 1)] = _rms(qh, gqk, QK).astype(q_ref.dtype)
    lat = _rms(proj[:, _C_CKV:_C_U], gkv_ref[...], KV_LORA)
    lat_ref[0] = lat
    kper = proj[:, _C_KA:_C_KB] * cosk_ref[...] + proj[:, _C_KB:_C_DT] * sink_ref[...]
    kpe_ref[0] = kper[:, :ROPE]
    if with_kv:
        latb = lat.astype(bf16)
        kn = _dot(latb, wk_ref[...])
        kp = pltpu.roll(kper, NOPE, 1)
        gk = gk_ref[...]
        for hh in range(NH):
            kh = kn[:, LANE * hh:LANE * (hh + 1)] + kp
            k_ref[0, :, LANE * hh:LANE * (hh + 1)] = _rms(kh, gk, QK).astype(bf16)
        v_ref[0] = _dot(latb, wv_ref[...]).astype(bf16)
    else:
        kpe128_ref[0] = kper


def _inproj(x, shift, scale, wts, tabs, *, tm, with_kv):
    g_, t_, d = x.shape
    tmod = shift.shape[1]
    nt = t_ // tm
    per_row = tmod != 1
    mod_spec = pl.BlockSpec((1, tm if per_row else 1, d), (lambda g, t: (g, t, 0)) if per_row else (lambda g, t: (g, 0, 0)))
    cosq, sinq, cosk, sink = tabs
    tab_rows = cosq.shape[0] != 1
    tab_spec = pl.BlockSpec((tm if tab_rows else 1, LANE), (lambda g, t: (t, 0)) if tab_rows else (lambda g, t: (0, 0)))

    def full(a):
        return pl.BlockSpec(a.shape, lambda g, t: (0,) * a.ndim)

    def row(c):
        return pl.BlockSpec((1, tm, c), lambda g, t: (g, t, 0))

    ins = [x, shift, scale, wts["g1"], wts["w_in"], wts["g_q"], wts["w_q"], cosq, sinq, wts["g_qk_q"], wts["g_kv"], cosk, sink]
    specs = [row(d), mod_spec, mod_spec, full(wts["g1"]), full(wts["w_in"]), full(wts["g_q"]), full(wts["w_q"]), tab_spec,
             tab_spec, full(wts["g_qk_q"]), full(wts["g_kv"]), tab_spec, tab_spec]
    widths = [SSD_W, CONV_CH, LANE, POOL_W, KV_LORA, ROPE, NH * LANE]
    dtypes = [f32, f32, f32, f32, f32, f32, bf16 if with_kv else f32]
    if with_kv:
        ins += [wts["w_k"], wts["g_qk_k"], wts["w_v"]]
        specs += [full(wts["w_k"]), full(wts["g_qk_k"]), full(wts["w_v"])]
        widths += [NH * LANE, SSD_W]
        dtypes += [bf16, bf16]
    else:
        widths += [LANE]
        dtypes += [f32]
    return pl.pallas_call(
        functools.partial(_inproj_kernel, with_kv=with_kv),
        grid=(g_, nt),
        in_specs=specs,
        out_specs=[row(c) for c in widths],
        out_shape=[jax.ShapeDtypeStruct((g_, t_, c), dt) for c, dt in zip(widths, dtypes)],
        compiler_params=_cparams(("arbitrary", "arbitrary")),
        name="inproj_kv" if with_kv else "inproj",
    )(*ins)


def _ssd_prompt_kernel(xbc_ref, z_ref, dt_ref, cw_ref, cb_ref, dtb_ref, a_ref, dsk_ref, gn_ref,
                       y_ref, st_ref, ext_ref):
    c = pl.program_id(1)
    t_ = xbc_ref.shape[1]

    @pl.when(c == 0)
    def _():
        ext_ref[0:8, :] = jnp.zeros((8, CONV_CH), f32)
        st_ref[...] = jnp.zeros(st_ref.shape, f32)

    x = xbc_ref[0]
    ext_ref[8:8 + t_, :] = x
    cw = cw_ref[...]
    conv = cb_ref[...] + cw[3:4] * x
    for k in range(CONV_K - 1):
        conv = conv + cw[k:k + 1] * ext_ref[5 + k:5 + k + t_, :]
    ext_ref[0:8, :] = x[t_ - 8:, :]
    act = _silu(conv)
    xs = act[:, :SSD_W]
    bm = [act[:, SSD_W + SSD_N * g:SSD_W + SSD_N * (g + 1)].astype(bf16) for g in range(SSD_G)]
    cm = [act[:, SSD_W + SSD_N * (SSD_G + g):SSD_W + SSD_N * (SSD_G + g + 1)].astype(bf16) for g in range(SSD_G)]

    dt = _softplus(dt_ref[0] + dtb_ref[...])
    a = dt * a_ref[...]
    ri = lax.broadcasted_iota(jnp.int32, (t_, t_), 0)
    ci = lax.broadcasted_iota(jnp.int32, (t_, t_), 1)
    tril = ri >= ci
    acum = _dot_exact(jnp.where(tril, 1.0, 0.0).astype(f32), a)
    acum_t = acum.T
    expand = _head_expand()
    dt_l = _dot_exact(dt, expand)
    acum_l = _dot_exact(acum, expand)
    last_l = acum_l[t_ - 1:t_, :]
    e_l = jnp.exp(acum_l)
    decay_l = jnp.exp(last_l - acum_l)
    xdt = xs * dt_l
    xdt_w = xdt * decay_l

    cb = [_dot_nt(cm[g], bm[g]) for g in range(SSD_G)]
    lane = lax.broadcasted_iota(jnp.int32, (t_, LANE), 1)
    rowi = lax.broadcasted_iota(jnp.int32, (LANE, SSD_N), 0)
    first = lane < HD
    rfirst = rowi < HD
    y_cols = []
    for j in range(NH // 2):
        sl = slice(LANE * j, LANE * (j + 1))
        h0, h1 = 2 * j, 2 * j + 1
        g0, g1 = h0 // (NH // SSD_G), h1 // (NH // SSD_G)
        xj = xdt[:, sl].astype(bf16)
        yd = []
        for hh, gg in ((h0, g0), (h1, g1)):
            seg = jnp.where(tril, jnp.exp(jnp.minimum(acum[:, hh:hh + 1] - acum_t[hh:hh + 1, :], 0.0)), 0.0)
            yd.append(_dot((cb[gg] * seg).astype(bf16), xj))
        y_diag = jnp.where(first, yd[0], yd[1])
        hp = st_ref[0, sl, :]
        hpb = hp.astype(bf16)
        xw = xdt_w[:, sl].astype(bf16)
        if g0 == g1:
            y_off = _dot_nt(cm[g0], hpb)
            s_new = _dot_tn(xw, bm[g0])
        else:
            y_off = jnp.where(first, _dot_nt(cm[g0], hpb), _dot_nt(cm[g1], hpb))
            s_new = jnp.where(rfirst, _dot_tn(xw, bm[g0]), _dot_tn(xw, bm[g1]))
        tot = jnp.where(rfirst, jnp.exp(acum[t_ - 1:t_, h0:h0 + 1]), jnp.exp(acum[t_ - 1:t_, h1:h1 + 1]))
        st_ref[0, sl, :] = tot * hp + s_new
        y_cols.append(y_diag + y_off * e_l[:, sl])
    y = jnp.concatenate(y_cols, axis=-1) + dsk_ref[...] * xs
    v = y * _silu(z_ref[0])
    gl = lax.broadcasted_iota(jnp.int32, (t_, SSD_W), 1) < SSD_W // SSD_G
    v2 = v * v
    ss0 = jnp.sum(jnp.where(gl, v2, 0.0), axis=-1, keepdims=True)
    ss1 = jnp.sum(jnp.where(gl, 0.0, v2), axis=-1, keepdims=True)
    gw = SSD_W // SSD_G
    rinv = jnp.where(gl, lax.rsqrt(ss0 * (1.0 / gw) + EPS), lax.rsqrt(ss1 * (1.0 / gw) + EPS))
    y_ref[0] = v * rinv * gn_ref[...]


def _ssd_prompt(xbc, z, dt, wts):
    b_, l_, _ = xbc.shape
    t_ = 128
    nc = l_ // t_

    def full(a):
        return pl.BlockSpec(a.shape, lambda b, c: (0,) * a.ndim)

    def row(w):
        return pl.BlockSpec((1, t_, w), lambda b, c: (b, c, 0))

    small = [wts["conv_w"], wts["conv_b"], wts["dt_bias"], wts["a_neg"], wts["d_skip_l"], wts["g_ssd"]]
    return pl.pallas_call(
        _ssd_prompt_kernel,
        grid=(b_, nc),
        in_specs=[row(CONV_CH), row(SSD_W), row(LANE)] + [full(a) for a in small],
        out_specs=[row(SSD_W), pl.BlockSpec((1, SSD_W, SSD_N), lambda b, c: (b, 0, 0))],
        out_shape=[jax.ShapeDtypeStruct((b_, l_, SSD_W), f32), jax.ShapeDtypeStruct((b_, SSD_W, SSD_N), f32)],
        scratch_shapes=[pltpu.VMEM((8 + t_, CONV_CH), f32)],
        compiler_params=_cparams(("arbitrary", "arbitrary")),
        name="ssd_prompt",
    )(xbc, z, dt, *small)


def _attn_prompt_kernel(q_ref, k_ref, v_ref, g_ref, o_ref, m_ref, l_ref, acc_ref, *, tq):
    qi = pl.program_id(1)
    m_ref[...] = jnp.full(m_ref.shape, -1e30, f32)
    l_ref[...] = jnp.zeros(l_ref.shape, f32)
    acc_ref[...] = jnp.zeros(acc_ref.shape, f32)
    row = lax.broadcasted_iota(jnp.int32, (tq, tq), 0) + qi * tq
    col = lax.broadcasted_iota(jnp.int32, (tq, tq), 1)
    first = lax.broadcasted_iota(jnp.int32, (tq, LANE), 1) < HD

    def body(j, carry):
        off = pl.multiple_of(j * tq, tq)
        kblk = k_ref[0, pl.ds(off, tq), :]
        vblk = v_ref[0, pl.ds(off, tq), :]
        masked = (col + j * tq) > row
        for p in range(NH // 2):
            vp = vblk[:, LANE * p:LANE * (p + 1)]
            parts = []
            for hh in (2 * p, 2 * p + 1):
                s = _dot_nt(q_ref[0, :, LANE * hh:LANE * (hh + 1)], kblk[:, LANE * hh:LANE * (hh + 1)]) * ATTN_SCALE
                s = jnp.where(masked, -1e30, s)
                m_old = m_ref[hh]
                m_new = jnp.maximum(m_old, jnp.max(s, axis=-1, keepdims=True))
                alpha = jnp.exp(m_old - m_new)
                pe = jnp.exp(s - m_new)
                l_ref[hh] = alpha * l_ref[hh] + jnp.sum(pe, axis=-1, keepdims=True)
                m_ref[hh] = m_new
                parts.append((alpha, _dot(pe.astype(bf16), vp)))
            al = jnp.where(first, parts[0][0], parts[1][0])
            up = jnp.where(first, parts[0][1], parts[1][1])
            acc_ref[:, LANE * p:LANE * (p + 1)] = acc_ref[:, LANE * p:LANE * (p + 1)] * al + up
        return carry

    lax.fori_loop(0, qi + 1, body, 0)
    cols = []
    for p in range(NH // 2):
        inv = jnp.where(first, 1.0 / l_ref[2 * p], 1.0 / l_ref[2 * p + 1])
        cols.append(acc_ref[:, LANE * p:LANE * (p + 1)] * inv)
    o_ref[0] = _rms(jnp.concatenate(cols, axis=-1), g_ref[...], SSD_W)


def _attn_prompt(q, k, v, g_out, *, tq):
    b_, l_, _ = q.shape
    return pl.pallas_call(
        functools.partial(_attn_prompt_kernel, tq=tq),
        grid=(b_, l_ // tq),
        in_specs=[pl.BlockSpec((1, tq, NH * LANE), lambda b, i: (b, i, 0)),
                  pl.BlockSpec((1, l_, NH * LANE), lambda b, i: (b, 0, 0)),
                  pl.BlockSpec((1, l_, SSD_W), lambda b, i: (b, 0, 0)),
                  pl.BlockSpec(g_out.shape, lambda b, i: (0, 0))],
        out_specs=pl.BlockSpec((1, tq, SSD_W), lambda b, i: (b, i, 0)),
        out_shape=jax.ShapeDtypeStruct((b_, l_, SSD_W), f32),
        scratch_shapes=[pltpu.VMEM((NH, tq, 1), f32), pltpu.VMEM((NH, tq, 1), f32), pltpu.VMEM((tq, SSD_W), f32)],
        compiler_params=_cparams(("arbitrary", "arbitrary")),
        name="attn_prompt",
    )(q, k, v, g_out)


def _pool_select(lane, a, b, c, d):
    return jnp.where(lane < 64, a, jnp.where(lane < 128, b, jnp.where(lane < 192, c, d)))


def _pool_prompt_kernel(u_ref, w_ref, sc_ref, o_ref, e1, e2, e4, e8, *, rt):
    t_ = u_ref.shape[1]
    hist = 16
    for e in (e1, e2, e4, e8):
        e[0:hist, :] = jnp.zeros((hist, POOL_W), f32)
    e1[hist:hist + t_, :] = u_ref[0]
    lane = lax.broadcasted_iota(jnp.int32, (rt, POOL_W), 1)
    win = _pool_select(lane, 2, 4, 8, 16)
    for i in range(t_ // rt):
        r0 = hist + i * rt
        a = e1[r0:r0 + rt, :]
        s2 = a + e1[r0 - 1:r0 - 1 + rt, :]
        e2[r0:r0 + rt, :] = s2
        s4 = s2 + e2[r0 - 2:r0 - 2 + rt, :]
        e4[r0:r0 + rt, :] = s4
        s8 = s4 + e4[r0 - 4:r0 - 4 + rt, :]
        e8[r0:r0 + rt, :] = s8
        s16 = s8 + e8[r0 - 8:r0 - 8 + rt, :]
        pos = lax.broadcasted_iota(jnp.int32, (rt, POOL_W), 0) + i * rt
        cnt = jnp.minimum(pos + 1, win).astype(f32)
        pooled = _pool_select(lane, s2, s4, s8, s16) / cnt - a
        o_ref[0, i * rt:(i + 1) * rt, :] = _dot(pooled.astype(bf16), w_ref[...]) * sc_ref[...]


def _pool_prompt(u, w_bd, scale):
    b_, l_, _ = u.shape
    return pl.pallas_call(
        functools.partial(_pool_prompt_kernel, rt=256),
        grid=(b_,),
        in_specs=[pl.BlockSpec((1, l_, POOL_W), lambda b: (b, 0, 0)),
                  pl.BlockSpec(w_bd.shape, lambda b: (0, 0)),
                  pl.BlockSpec(scale.shape, lambda b: (0, 0))],
        out_specs=pl.BlockSpec((1, l_, POOL_W), lambda b: (b, 0, 0)),
        out_shape=jax.ShapeDtypeStruct((b_, l_, POOL_W), f32),
        scratch_shapes=[pltpu.VMEM((16 + l_, POOL_W), f32)] * 4,
        compiler_params=_cparams(("arbitrary",)),
        name="pool_prompt",
    )(u, w_bd, scale)


def _pool_sample_kernel(u_ref, buf_ref, w_ref, sc_ref, o_ref):
    u = u_ref[...]
    run = u
    sums = {}
    for i in range(1, 16):
        run = run + buf_ref[POOL_BUF - i]
        if i + 1 in (2, 4, 8, 16):
            sums[i + 1] = run * (1.0 / (i + 1))
    lane = lax.broadcasted_iota(jnp.int32, u.shape, 1)
    pooled = _pool_select(lane, sums[2], sums[4], sums[8], sums[16]) - u
    o_ref[...] = _dot(pooled.astype(bf16), w_ref[...]) * sc_ref[...]


def _pool_sample(u, buf_t, w_bd, scale):
    return pl.pallas_call(
        _pool_sample_kernel,
        out_shape=jax.ShapeDtypeStruct(u.shape, f32),
        compiler_params=pltpu.CompilerParams(vmem_limit_bytes=VMEM_LIMIT),
        name="pool_sample",
    )(u, buf_t, w_bd, scale)


def _ffn_kernel(x_ref, ssd_ref, mla_ref, pool_ref, g1_ref, sh_ref, sc_ref, g2_ref, gn_ref,
                wo_ref, wg_ref, wu_ref, wd_ref, o_ref, *, nch):
    d = x_ref.shape[-1]
    mix = _dot(ssd_ref[0].astype(bf16), wo_ref[0:SSD_W, :])
    mix = mix + _dot(mla_ref[0].astype(bf16), wo_ref[SSD_W:2 * SSD_W, :])
    mix = mix + _dot(pool_ref[0].astype(bf16), wo_ref[2 * SSD_W:, :])
    x1 = x_ref[0] + g1_ref[0] * mix
    h2 = (_rms(x1, gn_ref[...], d) * (1.0 + sc_ref[0]) + sh_ref[0]).astype(bf16)
    hc = wg_ref.shape[1] // nch
    acc = jnp.zeros(x1.shape, f32)
    for c in range(nch):
        gate = _dot(h2, wg_ref[:, hc * c:hc * (c + 1)])
        up = _dot(h2, wu_ref[:, hc * c:hc * (c + 1)])
        acc = acc + _dot((_silu(gate) * up).astype(bf16), wd_ref[hc * c:hc * (c + 1), :])
    o_ref[0] = x1 + g2_ref[0] * acc


def _ffn(x, ssd, mla, pool, mods, wts, *, tm):
    g_, t_, d = x.shape
    gate1, shift2, scale2, gate2 = mods
    per_row = gate1.shape[1] != 1
    mod_spec = pl.BlockSpec((1, tm if per_row else 1, d), (lambda g, t: (g, t, 0)) if per_row else (lambda g, t: (g, 0, 0)))

    def full(a):
        return pl.BlockSpec(a.shape, lambda g, t: (0,) * a.ndim, pipeline_mode=pl.Buffered(1))

    def row(c):
        return pl.BlockSpec((1, tm, c), lambda g, t: (g, t, 0))

    ws = [wts["g2"], wts["w_out"], wts["w_gate"], wts["w_up"], wts["w_down"]]
    return pl.pallas_call(
        functools.partial(_ffn_kernel, nch=2),
        grid=(g_, t_ // tm),
        in_specs=[row(d), row(SSD_W), row(SSD_W), row(POOL_W), mod_spec, mod_spec, mod_spec, mod_spec] + [full(a) for a in ws],
        out_specs=row(d),
        out_shape=jax.ShapeDtypeStruct((g_, t_, d), f32),
        compiler_params=_cparams(("arbitrary", "arbitrary")),
        name="outproj_ffn",
    )(x, ssd, mla, pool, gate1, shift2, scale2, gate2, *ws)


def _ssd_sample_pre_kernel(xbc_ref, buf_ref, dt_ref, cw_ref, cb_ref, dtb_ref, a_ref,
                           xs_ref, b_ref, c_ref, xdt_t_ref, da_ref):
    cw = cw_ref[...]
    conv = cb_ref[...] + cw[3:4] * xbc_ref[...]
    for k in range(CONV_K - 1):
        conv = conv + cw[k:k + 1] * buf_ref[k]
    act = _silu(conv)
    xs = act[:, :SSD_W]
    xs_ref[...] = xs
    b_ref[...] = act[:, SSD_W:SSD_W + SSD_G * SSD_N]
    c_ref[...] = act[:, SSD_W + SSD_G * SSD_N:]
    dt = _softplus(dt_ref[...] + dtb_ref[...])
    da_ref[...] = jnp.exp(dt * a_ref[...])
    xdt = xs * _dot_exact(dt, _head_expand())
    for j in range(SSD_W // LANE):
        xdt_t_ref[LANE * j:LANE * (j + 1), :] = xdt[:, LANE * j:LANE * (j + 1)].T


def _ssd_sample_pre(xbc, buf_t, dt, wts):
    b_ = xbc.shape[0]
    shapes = [(b_, SSD_W), (b_, SSD_G * SSD_N), (b_, SSD_G * SSD_N), (SSD_W, b_), (b_, LANE)]
    return pl.pallas_call(
        _ssd_sample_pre_kernel,
        out_shape=[jax.ShapeDtypeStruct(s, f32) for s in shapes],
        compiler_params=pltpu.CompilerParams(vmem_limit_bytes=VMEM_LIMIT),
        name="ssd_sample_pre",
    )(xbc, buf_t, dt, wts["conv_w"], wts["conv_b"], wts["dt_bias"], wts["a_neg"])


def _ssd_sample_state_kernel(da_ref, st_ref, xdt_t_ref, b_ref, c_ref, new_ref, y_t_ref, *, tb):
    i = pl.program_id(0)

    @pl.when(i == 0)
    def _():
        y_t_ref[...] = jnp.zeros(y_t_ref.shape, f32)

    nb = xdt_t_ref.shape[1]
    lane = lax.broadcasted_iota(jnp.int32, (HD, nb), 1)
    for bl in range(tb):
        b = i * tb + bl
        sel = lane == b
        for hh in range(NH):
            g = hh // (NH // SSD_G)
            rs = slice(HD * hh, HD * (hh + 1))
            col = jnp.sum(jnp.where(sel, xdt_t_ref[rs, :], 0.0), axis=1, keepdims=True)
            brow = b_ref[bl:bl + 1, SSD_N * g:SSD_N * (g + 1)]
            crow = c_ref[bl:bl + 1, SSD_N * g:SSD_N * (g + 1)]
            new = da_ref[b, hh] * st_ref[bl, rs, :] + col * brow
            new_ref[bl, rs, :] = new
            ycol = jnp.sum(new * crow, axis=1, keepdims=True)
            y_t_ref[rs, :] = jnp.where(sel, ycol, y_t_ref[rs, :])


def _ssd_sample_state(da, state, xdt_t, bm, cm, *, tb=8):
    b_ = state.shape[0]
    return pl.pallas_call(
        functools.partial(_ssd_sample_state_kernel, tb=tb),
        grid=(b_ // tb,),
        in_specs=[pl.BlockSpec(memory_space=pltpu.SMEM),
                  pl.BlockSpec((tb, SSD_W, SSD_N), lambda i: (i, 0, 0)),
                  pl.BlockSpec(xdt_t.shape, lambda i: (0, 0)),
                  pl.BlockSpec((tb, SSD_G * SSD_N), lambda i: (i, 0)),
                  pl.BlockSpec((tb, SSD_G * SSD_N), lambda i: (i, 0))],
        out_specs=[pl.BlockSpec((tb, SSD_W, SSD_N), lambda i: (i, 0, 0)),
                   pl.BlockSpec((SSD_W, b_), lambda i: (0, 0))],
        out_shape=[jax.ShapeDtypeStruct(state.shape, f32), jax.ShapeDtypeStruct((SSD_W, b_), f32)],
        compiler_params=_cparams(("arbitrary",)),
        name="ssd_sample_state",
    )(da, state, xdt_t, bm, cm)


def _ssd_sample_post_kernel(y_ref, xs_ref, z_ref, dsk_ref, gn_ref, o_ref):
    y = y_ref[...] + dsk_ref[...] * xs_ref[...]
    v = y * _silu(z_ref[...])
    gl = lax.broadcasted_iota(jnp.int32, v.shape, 1) < SSD_W // SSD_G
    v2 = v * v
    gw = SSD_W // SSD_G
    ss0 = jnp.sum(jnp.where(gl, v2, 0.0), axis=-1, keepdims=True)
    ss1 = jnp.sum(jnp.where(gl, 0.0, v2), axis=-1, keepdims=True)
    rinv = jnp.where(gl, lax.rsqrt(ss0 * (1.0 / gw) + EPS), lax.rsqrt(ss1 * (1.0 / gw) + EPS))
    o_ref[...] = v * rinv * gn_ref[...]


def _ssd_sample_post(y, xs, z, wts):
    return pl.pallas_call(
        _ssd_sample_post_kernel,
        out_shape=jax.ShapeDtypeStruct(y.shape, f32),
        name="ssd_sample_post",
    )(y, xs, z, wts["d_skip_l"], wts["g_ssd"])


def _mla_sample_pre_kernel(q_ref, lat_ref, kpe_ref, wk_ref, wkt_ref, gk_ref, qabs_ref, qpe_ref, snew_ref):
    kn = _dot(lat_ref[...].astype(bf16), wk_ref[...])
    kp = pltpu.roll(kpe_ref[...], NOPE, 1)
    gk = gk_ref[...]
    lane = lax.broadcasted_iota(jnp.int32, snew_ref.shape, 1)
    snew = jnp.zeros(snew_ref.shape, f32)
    for hh in range(NH):
        sl = slice(LANE * hh, LANE * (hh + 1))
        qh = q_ref[:, sl]
        knew = _rms(kn[:, sl] + kp, gk, QK)
        snew = jnp.where(lane == hh, jnp.sum(qh * knew, axis=-1, keepdims=True) * ATTN_SCALE, snew)
        qg = qh * gk
        qabs_ref[:, sl] = _dot_exact(qg, wkt_ref[hh])
        qpe_ref[:, sl] = pltpu.roll(qg, NOPE, 1)
    snew_ref[...] = snew


def _mla_sample_pre(q, lat, kpe128, wts):
    b_ = q.shape[0]
    shapes = [(b_, NH * LANE), (b_, NH * LANE), (b_, LANE)]
    return pl.pallas_call(
        _mla_sample_pre_kernel,
        out_shape=[jax.ShapeDtypeStruct(s, f32) for s in shapes],
        name="mla_sample_pre",
    )(q, lat, kpe128, wts["w_k"], wts["w_kt"], wts["g_qk_k"])


def _attn_sample_kernel(pt_ref, lat_hbm, kpe_hbm, wkt_ref, qabs_ref, qpe_ref, snew_ref, latnew_ref, o_ref,
                        lat_buf, kpe_buf, sem, wext, *, li, pages, nchunk):
    b = pl.program_id(0)
    nb = pl.num_programs(0)
    rc = pages * PAGE

    def copies(bb, cc, slot):
        out = []
        for i in range(pages):
            page = pt_ref[bb, cc * pages + i]
            out.append(pltpu.make_async_copy(lat_hbm.at[li, page], lat_buf.at[slot, pl.ds(i * PAGE, PAGE)], sem.at[0, slot]))
            out.append(pltpu.make_async_copy(kpe_hbm.at[li, page], kpe_buf.at[slot, pl.ds(i * PAGE, PAGE)], sem.at[1, slot]))
        return out

    def start(bb, cc, slot):
        for cp in copies(bb, cc, slot):
            cp.start()

    def wait(bb, cc, slot):
        for cp in copies(bb, cc, slot):
            cp.wait()

    @pl.when(b == 0)
    def _():
        start(0, 0, 0)
        wext[0:NH * HD, :] = wkt_ref[...]

    wext[NH * HD:NH * HD + HPAD, :] = qabs_ref[0]
    qpe = qpe_ref[0]
    ones = jnp.ones((HPAD, ROPE), bf16)
    rowi = lax.broadcasted_iota(jnp.int32, (HPAD, rc), 0)

    def chunk(cc, slot, carry):
        m, l, acc = carry
        wait(b, cc, slot)

        @pl.when(cc + 1 < nchunk)
        def _():
            start(b, cc + 1, 1 - slot)

        @pl.when(jnp.logical_and(cc + 1 == nchunk, b + 1 < nb))
        def _():
            start(b + 1, 0, 1 - slot)

        latb = lat_buf[slot].astype(bf16)
        kpe = kpe_buf[slot]
        a = _dot_nt(wext[...], latb)
        nsq = jnp.zeros((HPAD, rc), f32)
        for hh in range(NH):
            kh = a[HD * hh:HD * (hh + 1), :]
            nsq = jnp.where(rowi == hh, jnp.sum(kh * kh, axis=0, keepdims=True), nsq)
        s_pe = _dot_nt(qpe, kpe.astype(bf16))
        ksq = _dot_nt(ones, (kpe * kpe).astype(bf16))
        s = (a[NH * HD:, :] + s_pe) * lax.rsqrt((nsq + ksq) * (1.0 / QK) + EPS) * ATTN_SCALE
        m_new = jnp.maximum(m, jnp.max(s, axis=-1, keepdims=True))
        alpha = jnp.exp(m - m_new)
        pe = jnp.exp(s - m_new)
        l = alpha * l + jnp.sum(pe, axis=-1, keepdims=True)
        acc = alpha * acc + _dot(pe.astype(bf16), latb)
        return m_new, l, acc

    def body(c2, carry):
        carry = chunk(2 * c2, 0, carry)
        return chunk(2 * c2 + 1, 1, carry)

    init = (snew_ref[0][:, 0:1], jnp.ones((HPAD, 1), f32), jnp.broadcast_to(latnew_ref[0], (HPAD, KV_LORA)))
    m, l, acc = lax.fori_loop(0, nchunk // 2, body, init)
    o_ref[0] = acc / l


def _attn_sample(page_table, cache_lat, cache_kpe, wkt, qabs, qpe, snew, lat_new, *, li, pages=8):
    b_, npages = page_table.shape
    nchunk = npages // pages
    assert npages % pages == 0 and nchunk % 2 == 0
    rc = pages * PAGE
    grid_spec = pltpu.PrefetchScalarGridSpec(
        num_scalar_prefetch=1,
        grid=(b_,),
        in_specs=[pl.BlockSpec(memory_space=pl.ANY),
                  pl.BlockSpec(memory_space=pl.ANY),
                  pl.BlockSpec(wkt.shape, lambda b, pt: (0, 0)),
                  pl.BlockSpec((1, HPAD, KV_LORA), lambda b, pt: (b, 0, 0)),
                  pl.BlockSpec((1, HPAD, ROPE), lambda b, pt: (b, 0, 0)),
                  pl.BlockSpec((1, HPAD, LANE), lambda b, pt: (b, 0, 0)),
                  pl.BlockSpec((1, 1, KV_LORA), lambda b, pt: (b, 0, 0))],
        out_specs=pl.BlockSpec((1, HPAD, KV_LORA), lambda b, pt: (b, 0, 0)),
        scratch_shapes=[pltpu.VMEM((2, rc, KV_LORA), f32),
                        pltpu.VMEM((2, rc, ROPE), f32),
                        pltpu.SemaphoreType.DMA((2, 2)),
                        pltpu.VMEM((NH * HD + HPAD, KV_LORA), bf16)],
    )
    return pl.pallas_call(
        functools.partial(_attn_sample_kernel, li=li, pages=pages, nchunk=nchunk),
        grid_spec=grid_spec,
        out_shape=jax.ShapeDtypeStruct((b_, HPAD, KV_LORA), f32),
        compiler_params=_cparams(("arbitrary",)),
        name="attn_sample",
    )(page_table, cache_lat, cache_kpe, wkt, qabs, qpe, snew, lat_new)


def _mla_sample_post_kernel(o_ref, wv_ref, g_ref, out_ref):
    out_ref[...] = _rms(_dot(o_ref[...].astype(bf16), wv_ref[...]), g_ref[...], SSD_W)


def _mla_sample_post(o_lat, w_v_bd, g_out):
    return pl.pallas_call(
        _mla_sample_post_kernel,
        out_shape=jax.ShapeDtypeStruct((o_lat.shape[0], SSD_W), f32),
        name="mla_sample_post",
    )(o_lat, w_v_bd, g_out)


def _pad_heads(w, lo, hi):
    pad = [(0, 0)] * (w.ndim - 1) + [(lo, LANE - hi)]
    w = jnp.pad(w, pad)
    return w.reshape(w.shape[:-2] + (NH * LANE,))


def _rot_cols(w):
    half = ROPE // 2
    return jnp.concatenate([-w[..., half:], w[..., :half]], axis=-1)


def _pad_lanes(a, lo=0):
    return jnp.pad(a, [(0, 0)] * (a.ndim - 1) + [(lo, LANE - lo - a.shape[-1])])


def _layer_weights(p, li):
    d = p["w_in"].shape[1]
    w_in = p["w_in"][li]
    offs = np.cumsum([SSD_W, CONV_CH, NH, Q_LORA, KV_LORA, ROPE, POOL_W])
    wz, wxbc, wdt, wcq, wckv, wkpe, wu = jnp.split(w_in, offs[:-1].tolist(), axis=1)
    w_big = jnp.concatenate([wz, wxbc, wcq, wckv, wu, _pad_lanes(wkpe), _pad_lanes(_rot_cols(wkpe)), _pad_lanes(wdt)],
                            axis=1).astype(bf16)
    wq = p["w_q_up"][li]
    wq_pe = wq[..., NOPE:]
    wq_plain = _pad_heads(wq, 0, QK)
    wq_rot = _pad_heads(_rot_cols(wq_pe), NOPE, QK)
    wk = p["w_k_up"][li]
    wk_pad = _pad_heads(wk, 0, NOPE)
    wk_t = jnp.transpose(wk, (1, 2, 0))
    wv = p["w_v_up"][li]
    w_v_bd = jnp.zeros((NH, KV_LORA, NH, HD), f32)
    w_v_bd = w_v_bd.at[jnp.arange(NH), :, jnp.arange(NH), :].set(jnp.transpose(wv, (1, 0, 2)))
    wp = p["w_pool"][li]
    ng = wp.shape[0]
    w_pool_bd = jnp.zeros((ng, HD, ng, HD), f32).at[jnp.arange(ng), :, jnp.arange(ng), :].set(wp)
    return dict(
        g1=p["g_norm1"][li].reshape(1, d), w_in=w_big,
        g_q=p["g_q_lora"][li].reshape(1, Q_LORA), w_q=jnp.concatenate([wq_plain, wq_rot], axis=1).astype(bf16),
        g_qk_q=_pad_lanes(p["g_qk_q"][li].reshape(1, QK)), g_qk_k=_pad_lanes(p["g_qk_k"][li].reshape(1, QK)),
        g_kv=p["g_kv_lora"][li].reshape(1, KV_LORA),
        w_k=wk_pad.astype(bf16), w_v=wv.reshape(KV_LORA, NH * HD).astype(bf16),
        w_kt=jnp.pad(wk_t, ((0, 0), (0, LANE - NOPE), (0, 0))),
        w_kt_flat=wk_t.reshape(NH * NOPE, KV_LORA).astype(bf16),
        w_v_bd=w_v_bd.reshape(NH * KV_LORA, NH * HD).astype(bf16),
        g_mla=p["g_mla_out"][li].reshape(1, SSD_W),
        conv_w=p["conv_w"][li], conv_b=p["conv_b"][li].reshape(1, CONV_CH),
        dt_bias=_pad_lanes(p["dt_bias"][li].reshape(1, NH)),
        a_neg=_pad_lanes(-jnp.exp(p["a_log"][li].astype(f32)).reshape(1, NH)),
        d_skip_l=jnp.repeat(p["d_skip"][li], HD).reshape(1, SSD_W), g_ssd=p["g_ssd_norm"][li].reshape(1, SSD_W),
        w_pool=w_pool_bd.reshape(POOL_W, POOL_W).astype(bf16), pool_scale=p["pool_scale"][li].reshape(1, POOL_W),
        g2=p["g_norm2"][li].reshape(1, d), w_out=p["w_out"][li].astype(bf16),
        w_gate=p["w_gate"][li].astype(bf16), w_up=p["w_up"][li].astype(bf16), w_down=p["w_down"][li].astype(bf16),
    )


def _rope_tables(pos):
    half = ROPE // 2
    inv = 1.0 / (ROPE_THETA ** (jnp.arange(half, dtype=f32) / half))
    ang = pos.astype(f32)[:, None] * inv[None, :]
    cos2 = jnp.concatenate([jnp.cos(ang)] * 2, axis=-1)
    sin2 = jnp.concatenate([jnp.sin(ang)] * 2, axis=-1)
    n = pos.shape[0]
    cosq = jnp.concatenate([jnp.ones((n, NOPE), f32), cos2, jnp.zeros((n, LANE - QK), f32)], axis=-1)
    sinq = _pad_lanes(sin2, NOPE)
    return cosq, sinq, _pad_lanes(cos2), _pad_lanes(sin2)


def kernel(x_prompt, x_sample, cache_kv_latent, cache_k_rope, state_ssm, state_conv, state_pool, page_table, c_prompt, c_sample, w_ada, b_ada, g_norm1, w_in, conv_w, conv_b, dt_bias, a_log, d_skip, g_ssd_norm, g_q_lora, w_q_up, g_kv_lora, w_k_up, w_v_up, g_qk_q, g_qk_k, g_mla_out, w_pool, pool_scale, w_out, g_norm2, w_gate, w_up, w_down):
    params = dict(g_norm1=g_norm1, w_in=w_in, conv_w=conv_w, conv_b=conv_b, dt_bias=dt_bias, a_log=a_log, d_skip=d_skip,
                  g_ssd_norm=g_ssd_norm, g_q_lora=g_q_lora, w_q_up=w_q_up, g_kv_lora=g_kv_lora, w_k_up=w_k_up,
                  w_v_up=w_v_up, g_qk_q=g_qk_q, g_qk_k=g_qk_k, g_mla_out=g_mla_out, w_pool=w_pool, pool_scale=pool_scale,
                  w_out=w_out, g_norm2=g_norm2, w_gate=w_gate, w_up=w_up, w_down=w_down)
    depth = w_ada.shape[0]
    bp, seq, d = x_prompt.shape
    bs = x_sample.shape[0]
    past = page_table.shape[1] * PAGE

    mod = _ada_mod(jnp.concatenate([c_prompt, c_sample], axis=0), w_ada, b_ada).reshape(depth, bp + bs, 6, d)
    tabs_p = _rope_tables(jnp.arange(seq, dtype=jnp.int32))
    tabs_s = _rope_tables(jnp.full((1,), past, jnp.int32))

    yp = x_prompt
    ys = x_sample.reshape(1, bs, d)
    p_new = [[] for _ in range(5)]
    s_new = [[] for _ in range(5)]
    for li in range(depth):
        wts = _layer_weights(params, li)
        mp = [mod[li, :bp, i].reshape(bp, 1, d) for i in range(6)]
        ms = [mod[li, bp:, i].reshape(1, bs, d) for i in range(6)]

        z, xbc, dt, u, lat, kpe, q, k, v = _inproj(yp, mp[0], mp[1], wts, tabs_p, tm=512, with_kv=True)
        ssd_out, h_t = _ssd_prompt(xbc, z, dt, wts)
        mla_out = _attn_prompt(q, k, v, wts["g_mla"], tq=512)
        pool_out = _pool_prompt(u, wts["w_pool"], wts["pool_scale"])
        yp = _ffn(yp, ssd_out, mla_out, pool_out, (mp[2], mp[3], mp[4], mp[5]), wts, tm=256)
        for lst, val in zip(p_new, (lat, kpe, h_t.reshape(bp, NH, HD, SSD_N), xbc[:, seq - (CONV_K - 1):],
                                    u[:, seq - POOL_BUF:])):
            lst.append(val)

        z, xbc, dt, u, lat, kpe, q, kpe128 = _inproj(ys, ms[0], ms[1], wts, tabs_s, tm=bs, with_kv=False)
        z, xbc, dt, u, lat, kpe, q, kpe128 = (a[0] for a in (z, xbc, dt, u, lat, kpe, q, kpe128))
        conv_buf = state_conv[li]
        xs, bm, cm, xdt_t, da = _ssd_sample_pre(xbc, jnp.transpose(conv_buf, (1, 0, 2)), dt, wts)
        h_new, y_t = _ssd_sample_state(da[:, :8], state_ssm[li].reshape(bs, SSD_W, SSD_N), xdt_t, bm, cm)
        ssd_out = _ssd_sample_post(y_t.T, xs, z, wts)
        qabs, qpe, snew = _mla_sample_pre(q, lat, kpe128, wts)
        qabs = jnp.pad(qabs.reshape(bs, NH, LANE), ((0, 0), (0, HPAD - NH), (0, 0))).astype(bf16)
        qpe = jnp.pad(qpe.reshape(bs, NH, LANE)[:, :, :ROPE], ((0, 0), (0, HPAD - NH), (0, 0))).astype(bf16)
        snew_b = jnp.broadcast_to(jnp.pad(snew[:, :NH], ((0, 0), (0, HPAD - NH)))[:, :, None], (bs, HPAD, LANE))
        o_lat = _attn_sample(page_table, cache_kv_latent, cache_k_rope, wts["w_kt_flat"], qabs, qpe, snew_b,
                             lat.reshape(bs, 1, KV_LORA), li=li)
        mla_out = _mla_sample_post(o_lat[:, :NH].reshape(bs, NH * KV_LORA), wts["w_v_bd"], wts["g_mla"])
        pool_buf = state_pool[li]
        pool_out = _pool_sample(u, jnp.transpose(pool_buf, (1, 0, 2)), wts["w_pool"], wts["pool_scale"])
        ys = _ffn(ys, ssd_out[None], mla_out[None], pool_out[None], (ms[2], ms[3], ms[4], ms[5]), wts, tm=bs)
        conv_new = jnp.concatenate([conv_buf[:, 1:], xbc[:, None, :]], axis=1)
        pool_new = jnp.concatenate([pool_buf[:, 1:], u[:, None, :]], axis=1)
        for lst, val in zip(s_new, (lat[:, None, :], kpe[:, None, :], h_new.reshape(bs, NH, HD, SSD_N), conv_new, pool_new)):
            lst.append(val)

    outs_p = [jnp.stack(vv, axis=0) for vv in p_new]
    outs_s = [jnp.stack(vv, axis=0) for vv in s_new]
    return (yp, ys.reshape(bs, 1, d), *outs_p, *outs_s)
```

```python
import functools
import math

import jax
import jax.numpy as jnp
import numpy as np
from jax import lax
from jax.experimental import pallas as pl
from jax.experimental.pallas import tpu as pltpu

f32 = jnp.float32
bf16 = jnp.bfloat16
HIGHEST = lax.Precision.HIGHEST

EPS = 1e-6
PAGE = 128
NH = 6
HD = 64
NOPE = 64
ROPE = 32
QK = NOPE + ROPE
SSD_W = NH * HD
SSD_G = 2
SSD_N = 128
CONV_K = 4
CONV_CH = SSD_W + 2 * SSD_G * SSD_N
Q_LORA = 256
KV_LORA = 128
POOL_W = 256
POOL_BUF = 15
ROPE_THETA = 10000.0
ATTN_SCALE = QK ** -0.5
LANE = 128
HPAD = 16
VMEM_LIMIT = 56 * 1024 * 1024

_C_Z, _C_XBC, _C_CQ, _C_CKV, _C_U, _C_KA, _C_KB, _C_DT, _C_END = 0, 384, 1280, 1536, 1664, 1920, 2048, 2176, 2304


def _cparams(sem):
    return pltpu.CompilerParams(dimension_semantics=sem, vmem_limit_bytes=VMEM_LIMIT)


def _silu(x):
    return x * jax.nn.sigmoid(x)


def _softplus(x):
    return jnp.maximum(x, 0.0) + jnp.log1p(jnp.exp(-jnp.abs(x)))


def _rms(x, g, n):
    ms = jnp.sum(x * x, axis=-1, keepdims=True) * (1.0 / n)
    return x * lax.rsqrt(ms + EPS) * g


def _dot(a, b):
    return jnp.dot(a, b, preferred_element_type=f32)


def _dot_nt(a, b):
    return lax.dot_general(a, b, (((1,), (1,)), ((), ())), preferred_element_type=f32)


def _dot_tn(a, b):
    return lax.dot_general(a, b, (((0,), (0,)), ((), ())), preferred_element_type=f32)


def _dot_exact(a, b):
    return jnp.dot(a, b, precision=HIGHEST, preferred_element_type=f32)


def _head_expand():
    r = lax.broadcasted_iota(jnp.int32, (LANE, SSD_W), 0)
    c = lax.broadcasted_iota(jnp.int32, (LANE, SSD_W), 1)
    return jnp.where(c // HD == r, 1.0, 0.0).astype(f32)


def _ada_kernel(c_ref, w_ref, b_ref, o_ref):
    s = _silu(c_ref[...]).astype(bf16)
    o_ref[0] = _dot(s, w_ref[0].astype(bf16)) + b_ref[0]


def _ada_mod(c_all, w_ada, b_ada):
    depth, d, n6 = w_ada.shape
    rows = c_all.shape[0]
    tn = 1536
    return pl.pallas_call(
        _ada_kernel,
        grid=(depth, n6 // tn),
        in_specs=[pl.BlockSpec((rows, d), lambda l, j: (0, 0)),
                  pl.BlockSpec((1, d, tn), lambda l, j: (l, 0, j)),
                  pl.BlockSpec((1, 1, tn), lambda l, j: (l, 0, j))],
        out_specs=pl.BlockSpec((1, rows, tn), lambda l, j: (l, 0, j)),
        out_shape=jax.ShapeDtypeStruct((depth, rows, n6), f32),
        compiler_params=_cparams(("arbitrary", "arbitrary")),
        name="ada_mod",
    )(c_all, w_ada, b_ada.reshape(depth, 1, n6))


def _inproj_kernel(*refs, with_kv):
    (x_ref, sh_ref, sc_ref, g1_ref, w_ref, gq_ref, wq_ref, cosq_ref, sinq_ref, gqk_ref, gkv_ref,
     cosk_ref, sink_ref) = refs[:13]
    if with_kv:
        wk_ref, gk_ref, wv_ref = refs[13:16]
        z_ref, xbc_ref, dt_ref, u_ref, lat_ref, kpe_ref, q_ref, k_ref, v_ref = refs[16:]
    else:
        z_ref, xbc_ref, dt_ref, u_ref, lat_ref, kpe_ref, q_ref, kpe128_ref = refs[13:]
    d = x_ref.shape[-1]
    h = _rms(x_ref[0], g1_ref[...], d) * (1.0 + sc_ref[0]) + sh_ref[0]
    proj = _dot(h.astype(bf16), w_ref[...])
    z_ref[0] = proj[:, _C_Z:_C_XBC]
    xbc_ref[0] = proj[:, _C_XBC:_C_CQ]
    u_ref[0] = proj[:, _C_U:_C_KA]
    dt_ref[0] = proj[:, _C_DT:_C_END]
    cqn = _rms(proj[:, _C_CQ:_C_CKV], gq_ref[...], Q_LORA).astype(bf16)
    qq = _dot(cqn, wq_ref[...])
    cosq = cosq_ref[...]
    sinq = sinq_ref[...]
    gqk = gqk_ref[...]
    for hh in range(NH):
        qh = qq[:, LANE * hh:LANE * (hh + 1)] * cosq + qq[:, NH * LANE + LANE * hh:NH * LANE + LANE * (hh + 1)] * sinq
        q_ref[0, :, LANE * hh:LANE * (hh + 1)] = _rms(qh, gqk, QK).astype(q_ref.dtype)
    lat = _rms(proj[:, _C_CKV:_C_U], gkv_ref[...], KV_LORA)
    lat_ref[0] = lat
    kper = proj[:, _C_KA:_C_KB] * cosk_ref[...] + proj[:, _C_KB:_C_DT] * sink_ref[...]
    kpe_ref[0] = kper[:, :ROPE]
    if with_kv:
        latb = lat.astype(bf16)
        kn = _dot(latb, wk_ref[...])
        kp = pltpu.roll(kper, NOPE, 1)
        gk = gk_ref[...]
        for hh in range(NH):
            kh = kn[:, LANE * hh:LANE * (hh + 1)] + kp
            k_ref[0, :, LANE * hh:LANE * (hh + 1)] = _rms(kh, gk, QK).astype(bf16)
        v_ref[0] = _dot(latb, wv_ref[...]).astype(bf16)
    else:
        kpe128_ref[0] = kper


def _inproj(x, shift, scale, wts, tabs, *, tm, with_kv):
    g_, t_, d = x.shape
    tmod = shift.shape[1]
    nt = t_ // tm
    per_row = tmod != 1
    mod_spec = pl.BlockSpec((1, tm if per_row else 1, d), (lambda g, t: (g, t, 0)) if per_row else (lambda g, t: (g, 0, 0)))
    cosq, sinq, cosk, sink = tabs
    tab_rows = cosq.shape[0] != 1
    tab_spec = pl.BlockSpec((tm if tab_rows else 1, LANE), (lambda g, t: (t, 0)) if tab_rows else (lambda g, t: (0, 0)))

    def full(a):
        return pl.BlockSpec(a.shape, lambda g, t: (0,) * a.ndim)

    def row(c):
        return pl.BlockSpec((1, tm, c), lambda g, t: (g, t, 0))

    ins = [x, shift, scale, wts["g1"], wts["w_in"], wts["g_q"], wts["w_q"], cosq, sinq, wts["g_qk_q"], wts["g_kv"], cosk, sink]
    specs = [row(d), mod_spec, mod_spec, full(wts["g1"]), full(wts["w_in"]), full(wts["g_q"]), full(wts["w_q"]), tab_spec,
             tab_spec, full(wts["g_qk_q"]), full(wts["g_kv"]), tab_spec, tab_spec]
    widths = [SSD_W, CONV_CH, LANE, POOL_W, KV_LORA, ROPE, NH * LANE]
    dtypes = [f32, f32, f32, f32, f32, f32, bf16 if with_kv else f32]
    if with_kv:
        ins += [wts["w_k"], wts["g_qk_k"], wts["w_v"]]
        specs += [full(wts["w_k"]), full(wts["g_qk_k"]), full(wts["w_v"])]
        widths += [NH * LANE, SSD_W]
        dtypes += [bf16, bf16]
    else:
        widths += [LANE]
        dtypes += [f32]
    return pl.pallas_call(
        functools.partial(_inproj_kernel, with_kv=with_kv),
        grid=(g_, nt),
        in_specs=specs,
        out_specs=[row(c) for c in widths],
        out_shape=[jax.ShapeDtypeStruct((g_, t_, c), dt) for c, dt in zip(widths, dtypes)],
        compiler_params=_cparams(("arbitrary", "arbitrary")),
        name="inproj_kv" if with_kv else "inproj",
    )(*ins)


def _ssd_prompt_kernel(xbc_ref, z_ref, dt_ref, cw_ref, cb_ref, dtb_ref, a_ref, dsk_ref, gn_ref,
                       y_ref, st_ref, ext_ref):
    c = pl.program_id(1)
    t_ = xbc_ref.shape[1]

    @pl.when(c == 0)
    def _():
        ext_ref[0:8, :] = jnp.zeros((8, CONV_CH), f32)
        st_ref[...] = jnp.zeros(st_ref.shape, f32)

    x = xbc_ref[0]
    ext_ref[8:8 + t_, :] = x
    cw = cw_ref[...]
    conv = cb_ref[...] + cw[3:4] * x
    for k in range(CONV_K - 1):
        conv = conv + cw[k:k + 1] * ext_ref[5 + k:5 + k + t_, :]
    ext_ref[0:8, :] = x[t_ - 8:, :]
    act = _silu(conv)
    xs = act[:, :SSD_W]
    bm = [act[:, SSD_W + SSD_N * g:SSD_W + SSD_N * (g + 1)].astype(bf16) for g in range(SSD_G)]
    cm = [act[:, SSD_W + SSD_N * (SSD_G + g):SSD_W + SSD_N * (SSD_G + g + 1)].astype(bf16) for g in range(SSD_G)]

    dt = _softplus(dt_ref[0] + dtb_ref[...])
    a = dt * a_ref[...]
    ri = lax.broadcasted_iota(jnp.int32, (t_, t_), 0)
    ci = lax.broadcasted_iota(jnp.int32, (t_, t_), 1)
    tril = ri >= ci
    acum = _dot_exact(jnp.where(tril, 1.0, 0.0).astype(f32), a)
    acum_t = acum.T
    expand = _head_expand()
    dt_l = _dot_exact(dt, expand)
    acum_l = _dot_exact(acum, expand)
    last_l = acum_l[t_ - 1:t_, :]
    e_l = jnp.exp(acum_l)
    decay_l = jnp.exp(last_l - acum_l)
    xdt = xs * dt_l
    xdt_w = xdt * decay_l

    cb = [_dot_nt(cm[g], bm[g]) for g in range(SSD_G)]
    lane = lax.broadcasted_iota(jnp.int32, (t_, LANE), 1)
    rowi = lax.broadcasted_iota(jnp.int32, (LANE, SSD_N), 0)
    first = lane < HD
    rfirst = rowi < HD
    y_cols = []
    for j in range(NH // 2):
        sl = slice(LANE * j, LANE * (j + 1))
        h0, h1 = 2 * j, 2 * j + 1
        g0, g1 = h0 // (NH // SSD_G), h1 // (NH // SSD_G)
        xj = xdt[:, sl].astype(bf16)
        yd = []
        for hh, gg in ((h0, g0), (h1, g1)):
            seg = jnp.where(tril, jnp.exp(jnp.minimum(acum[:, hh:hh + 1] - acum_t[hh:hh + 1, :], 0.0)), 0.0)
            yd.append(_dot((cb[gg] * seg).astype(bf16), xj))
        y_diag = jnp.where(first, yd[0], yd[1])
        hp = st_ref[0, sl, :]
        hpb = hp.astype(bf16)
        xw = xdt_w[:, sl].astype(bf16)
        if g0 == g1:
            y_off = _dot_nt(cm[g0], hpb)
            s_new = _dot_tn(xw, bm[g0])
        else:
            y_off = jnp.where(first, _dot_nt(cm[g0], hpb), _dot_nt(cm[g1], hpb))
            s_new = jnp.where(rfirst, _dot_tn(xw, bm[g0]), _dot_tn(xw, bm[g1]))
        tot = jnp.where(rfirst, jnp.exp(acum[t_ - 1:t_, h0:h0 + 1]), jnp.exp(acum[t_ - 1:t_, h1:h1 + 1]))
        st_ref[0, sl, :] = tot * hp + s_new
        y_cols.append(y_diag + y_off * e_l[:, sl])
    y = jnp.concatenate(y_cols, axis=-1) + dsk_ref[...] * xs
    v = y * _silu(z_ref[0])
    gl = lax.broadcasted_iota(jnp.int32, (t_, SSD_W), 1) < SSD_W // SSD_G
    v2 = v * v
    ss0 = jnp.sum(jnp.where(gl, v2, 0.0), axis=-1, keepdims=True)
    ss1 = jnp.sum(jnp.where(gl, 0.0, v2), axis=-1, keepdims=True)
    gw = SSD_W // SSD_G
    rinv = jnp.where(gl, lax.rsqrt(ss0 * (1.0 / gw) + EPS), lax.rsqrt(ss1 * (1.0 / gw) + EPS))
    y_ref[0] = v * rinv * gn_ref[...]


def _ssd_prompt(xbc, z, dt, wts):
    b_, l_, _ = xbc.shape
    t_ = 128
    nc = l_ // t_

    def full(a):
        return pl.BlockSpec(a.shape, lambda b, c: (0,) * a.ndim)

    def row(w):
        return pl.BlockSpec((1, t_, w), lambda b, c: (b, c, 0))

    small = [wts["conv_w"], wts["conv_b"], wts["dt_bias"], wts["a_neg"], wts["d_skip_l"], wts["g_ssd"]]
    return pl.pallas_call(
        _ssd_prompt_kernel,
        grid=(b_, nc),
        in_specs=[row(CONV_CH), row(SSD_W), row(LANE)] + [full(a) for a in small],
        out_specs=[row(SSD_W), pl.BlockSpec((1, SSD_W, SSD_N), lambda b, c: (b, 0, 0))],
        out_shape=[jax.ShapeDtypeStruct((b_, l_, SSD_W), f32), jax.ShapeDtypeStruct((b_, SSD_W, SSD_N), f32)],
        scratch_shapes=[pltpu.VMEM((8 + t_, CONV_CH), f32)],
        compiler_params=_cparams(("arbitrary", "arbitrary")),
        name="ssd_prompt",
    )(xbc, z, dt, *small)


def _attn_prompt_kernel(q_ref, k_ref, v_ref, g_ref, o_ref, m_ref, l_ref, acc_ref, *, tq):
    qi = pl.program_id(1)
    m_ref[...] = jnp.full(m_ref.shape, -1e30, f32)
    l_ref[...] = jnp.zeros(l_ref.shape, f32)
    acc_ref[...] = jnp.zeros(acc_ref.shape, f32)
    first = lax.broadcasted_iota(jnp.int32, (tq, LANE), 1) < HD
    c_exp = ATTN_SCALE * math.log2(math.e)

    def tile(j, diagonal):
        off = pl.multiple_of(j * tq, tq)
        kblk = k_ref[0, pl.ds(off, tq), :]
        vblk = v_ref[0, pl.ds(off, tq), :]
        if diagonal:
            masked = lax.broadcasted_iota(jnp.int32, (tq, tq), 1) > lax.broadcasted_iota(jnp.int32, (tq, tq), 0)
        for p in range(NH // 2):
            vp = vblk[:, LANE * p:LANE * (p + 1)]
            parts = []
            for hh in (2 * p, 2 * p + 1):
                s = _dot_nt(q_ref[0, :, LANE * hh:LANE * (hh + 1)], kblk[:, LANE * hh:LANE * (hh + 1)])
                if diagonal:
                    s = jnp.where(masked, -1e30, s)
                m_old = m_ref[hh]
                m_new = jnp.maximum(m_old, jnp.max(s, axis=-1, keepdims=True))
                alpha = jnp.exp2((m_old - m_new) * c_exp)
                pe = jnp.exp2((s - m_new) * c_exp)
                part = pe[:, 0:LANE]
                for t in range(1, tq // LANE):
                    part = part + pe[:, LANE * t:LANE * (t + 1)]
                l_ref[hh] = alpha * l_ref[hh] + part
                m_ref[hh] = m_new
                parts.append((alpha, _dot(pe.astype(bf16), vp)))
            al = jnp.where(first, parts[0][0], parts[1][0])
            up = jnp.where(first, parts[0][1], parts[1][1])
            acc_ref[:, LANE * p:LANE * (p + 1)] = acc_ref[:, LANE * p:LANE * (p + 1)] * al + up

    def body(j, carry):
        tile(j, False)
        return carry

    lax.fori_loop(0, qi, body, 0)
    tile(qi, True)
    cols = []
    for p in range(NH // 2):
        l0 = jnp.sum(l_ref[2 * p], axis=-1, keepdims=True)
        l1 = jnp.sum(l_ref[2 * p + 1], axis=-1, keepdims=True)
        cols.append(acc_ref[:, LANE * p:LANE * (p + 1)] * jnp.where(first, 1.0 / l0, 1.0 / l1))
    o_ref[0] = _rms(jnp.concatenate(cols, axis=-1), g_ref[...], SSD_W)


def _attn_prompt(q, k, v, g_out, *, tq):
    b_, l_, _ = q.shape
    return pl.pallas_call(
        functools.partial(_attn_prompt_kernel, tq=tq),
        grid=(b_, l_ // tq),
        in_specs=[pl.BlockSpec((1, tq, NH * LANE), lambda b, i: (b, i, 0)),
                  pl.BlockSpec((1, l_, NH * LANE), lambda b, i: (b, 0, 0)),
                  pl.BlockSpec((1, l_, SSD_W), lambda b, i: (b, 0, 0)),
                  pl.BlockSpec(g_out.shape, lambda b, i: (0, 0))],
        out_specs=pl.BlockSpec((1, tq, SSD_W), lambda b, i: (b, i, 0)),
        out_shape=jax.ShapeDtypeStruct((b_, l_, SSD_W), f32),
        scratch_shapes=[pltpu.VMEM((NH, tq, 1), f32), pltpu.VMEM((NH, tq, LANE), f32), pltpu.VMEM((tq, SSD_W), f32)],
        compiler_params=_cparams(("arbitrary", "arbitrary")),
        name="attn_prompt",
    )(q, k, v, g_out)


def _pool_select(lane, a, b, c, d):
    return jnp.where(lane < 64, a, jnp.where(lane < 128, b, jnp.where(lane < 192, c, d)))


def _pool_prompt_kernel(u_ref, w_ref, sc_ref, o_ref, e1, e2, e4, e8, *, rt):
    t_ = u_ref.shape[1]
    hist = 16
    for e in (e1, e2, e4, e8):
        e[0:hist, :] = jnp.zeros((hist, POOL_W), f32)
    e1[hist:hist + t_, :] = u_ref[0]
    lane = lax.broadcasted_iota(jnp.int32, (rt, POOL_W), 1)
    win = _pool_select(lane, 2, 4, 8, 16)
    for i in range(t_ // rt):
        r0 = hist + i * rt
        a = e1[r0:r0 + rt, :]
        s2 = a + e1[r0 - 1:r0 - 1 + rt, :]
        e2[r0:r0 + rt, :] = s2
        s4 = s2 + e2[r0 - 2:r0 - 2 + rt, :]
        e4[r0:r0 + rt, :] = s4
        s8 = s4 + e4[r0 - 4:r0 - 4 + rt, :]
        e8[r0:r0 + rt, :] = s8
        s16 = s8 + e8[r0 - 8:r0 - 8 + rt, :]
        pos = lax.broadcasted_iota(jnp.int32, (rt, POOL_W), 0) + i * rt
        cnt = jnp.minimum(pos + 1, win).astype(f32)
        pooled = _pool_select(lane, s2, s4, s8, s16) / cnt - a
        o_ref[0, i * rt:(i + 1) * rt, :] = _dot(pooled.astype(bf16), w_ref[...]) * sc_ref[...]


def _pool_prompt(u, w_bd, scale):
    b_, l_, _ = u.shape
    return pl.pallas_call(
        functools.partial(_pool_prompt_kernel, rt=256),
        grid=(b_,),
        in_specs=[pl.BlockSpec((1, l_, POOL_W), lambda b: (b, 0, 0)),
                  pl.BlockSpec(w_bd.shape, lambda b: (0, 0)),
                  pl.BlockSpec(scale.shape, lambda b: (0, 0))],
        out_specs=pl.BlockSpec((1, l_, POOL_W), lambda b: (b, 0, 0)),
        out_shape=jax.ShapeDtypeStruct((b_, l_, POOL_W), f32),
        scratch_shapes=[pltpu.VMEM((16 + l_, POOL_W), f32)] * 4,
        compiler_params=_cparams(("arbitrary",)),
        name="pool_prompt",
    )(u, w_bd, scale)


def _pool_sample_kernel(u_ref, buf_ref, w_ref, sc_ref, o_ref):
    u = u_ref[...]
    run = u
    sums = {}
    for i in range(1, 16):
        run = run + buf_ref[POOL_BUF - i]
        if i + 1 in (2, 4, 8, 16):
            sums[i + 1] = run * (1.0 / (i + 1))
    lane = lax.broadcasted_iota(jnp.int32, u.shape, 1)
    pooled = _pool_select(lane, sums[2], sums[4], sums[8], sums[16]) - u
    o_ref[...] = _dot(pooled.astype(bf16), w_ref[...]) * sc_ref[...]


def _pool_sample(u, buf_t, w_bd, scale):
    return pl.pallas_call(
        _pool_sample_kernel,
        out_shape=jax.ShapeDtypeStruct(u.shape, f32),
        compiler_params=pltpu.CompilerParams(vmem_limit_bytes=VMEM_LIMIT),
        name="pool_sample",
    )(u, buf_t, w_bd, scale)


def _ffn_kernel(x_ref, ssd_ref, mla_ref, pool_ref, g1_ref, sh_ref, sc_ref, g2_ref, gn_ref,
                wo_ref, wg_ref, wu_ref, wd_ref, o_ref, *, nch):
    d = x_ref.shape[-1]
    mix = _dot(ssd_ref[0].astype(bf16), wo_ref[0:SSD_W, :])
    mix = mix + _dot(mla_ref[0].astype(bf16), wo_ref[SSD_W:2 * SSD_W, :])
    mix = mix + _dot(pool_ref[0].astype(bf16), wo_ref[2 * SSD_W:, :])
    x1 = x_ref[0] + g1_ref[0] * mix
    h2 = (_rms(x1, gn_ref[...], d) * (1.0 + sc_ref[0]) + sh_ref[0]).astype(bf16)
    hc = wg_ref.shape[1] // nch
    acc = jnp.zeros(x1.shape, f32)
    for c in range(nch):
        gate = _dot(h2, wg_ref[:, hc * c:hc * (c + 1)])
        up = _dot(h2, wu_ref[:, hc * c:hc * (c + 1)])
        acc = acc + _dot((_silu(gate) * up).astype(bf16), wd_ref[hc * c:hc * (c + 1), :])
    o_ref[0] = x1 + g2_ref[0] * acc


def _ffn(x, ssd, mla, pool, mods, wts, *, tm):
    g_, t_, d = x.shape
    gate1, shift2, scale2, gate2 = mods
    per_row = gate1.shape[1] != 1
    mod_spec = pl.BlockSpec((1, tm if per_row else 1, d), (lambda g, t: (g, t, 0)) if per_row else (lambda g, t: (g, 0, 0)))

    def full(a):
        return pl.BlockSpec(a.shape, lambda g, t: (0,) * a.ndim, pipeline_mode=pl.Buffered(1))

    def row(c):
        return pl.BlockSpec((1, tm, c), lambda g, t: (g, t, 0))

    ws = [wts["g2"], wts["w_out"], wts["w_gate"], wts["w_up"], wts["w_down"]]
    return pl.pallas_call(
        functools.partial(_ffn_kernel, nch=2),
        grid=(g_, t_ // tm),
        in_specs=[row(d), row(SSD_W), row(SSD_W), row(POOL_W), mod_spec, mod_spec, mod_spec, mod_spec] + [full(a) for a in ws],
        out_specs=row(d),
        out_shape=jax.ShapeDtypeStruct((g_, t_, d), f32),
        compiler_params=_cparams(("arbitrary", "arbitrary")),
        name="outproj_ffn",
    )(x, ssd, mla, pool, gate1, shift2, scale2, gate2, *ws)


def _ssd_sample_pre_kernel(xbc_ref, buf_ref, dt_ref, cw_ref, cb_ref, dtb_ref, a_ref,
                           xs_ref, b_ref, c_ref, xdt_t_ref, da_ref):
    cw = cw_ref[...]
    conv = cb_ref[...] + cw[3:4] * xbc_ref[...]
    for k in range(CONV_K - 1):
        conv = conv + cw[k:k + 1] * buf_ref[k]
    act = _silu(conv)
    xs = act[:, :SSD_W]
    xs_ref[...] = xs
    b_ref[...] = act[:, SSD_W:SSD_W + SSD_G * SSD_N]
    c_ref[...] = act[:, SSD_W + SSD_G * SSD_N:]
    dt = _softplus(dt_ref[...] + dtb_ref[...])
    da_ref[...] = jnp.exp(dt * a_ref[...])
    xdt = xs * _dot_exact(dt, _head_expand())
    for j in range(SSD_W // LANE):
        xdt_t_ref[LANE * j:LANE * (j + 1), :] = xdt[:, LANE * j:LANE * (j + 1)].T


def _ssd_sample_pre(xbc, buf_t, dt, wts):
    b_ = xbc.shape[0]
    shapes = [(b_, SSD_W), (b_, SSD_G * SSD_N), (b_, SSD_G * SSD_N), (SSD_W, b_), (b_, LANE)]
    return pl.pallas_call(
        _ssd_sample_pre_kernel,
        out_shape=[jax.ShapeDtypeStruct(s, f32) for s in shapes],
        compiler_params=pltpu.CompilerParams(vmem_limit_bytes=VMEM_LIMIT),
        name="ssd_sample_pre",
    )(xbc, buf_t, dt, wts["conv_w"], wts["conv_b"], wts["dt_bias"], wts["a_neg"])


def _ssd_sample_state_kernel(da_ref, st_ref, xdt_t_ref, b_ref, c_ref, new_ref, y_t_ref, *, tb):
    i = pl.program_id(0)

    @pl.when(i == 0)
    def _():
        y_t_ref[...] = jnp.zeros(y_t_ref.shape, f32)

    nb = xdt_t_ref.shape[1]
    lane = lax.broadcasted_iota(jnp.int32, (HD, nb), 1)
    for bl in range(tb):
        b = i * tb + bl
        sel = lane == b
        for hh in range(NH):
            g = hh // (NH // SSD_G)
            rs = slice(HD * hh, HD * (hh + 1))
            col = jnp.sum(jnp.where(sel, xdt_t_ref[rs, :], 0.0), axis=1, keepdims=True)
            brow = b_ref[bl:bl + 1, SSD_N * g:SSD_N * (g + 1)]
            crow = c_ref[bl:bl + 1, SSD_N * g:SSD_N * (g + 1)]
            new = da_ref[b, hh] * st_ref[bl, rs, :] + col * brow
            new_ref[bl, rs, :] = new
            ycol = jnp.sum(new * crow, axis=1, keepdims=True)
            y_t_ref[rs, :] = jnp.where(sel, ycol, y_t_ref[rs, :])


def _ssd_sample_state(da, state, xdt_t, bm, cm, *, tb=8):
    b_ = state.shape[0]
    return pl.pallas_call(
        functools.partial(_ssd_sample_state_kernel, tb=tb),
        grid=(b_ // tb,),
        in_specs=[pl.BlockSpec(memory_space=pltpu.SMEM),
                  pl.BlockSpec((tb, SSD_W, SSD_N), lambda i: (i, 0, 0)),
                  pl.BlockSpec(xdt_t.shape, lambda i: (0, 0)),
                  pl.BlockSpec((tb, SSD_G * SSD_N), lambda i: (i, 0)),
                  pl.BlockSpec((tb, SSD_G * SSD_N), lambda i: (i, 0))],
        out_specs=[pl.BlockSpec((tb, SSD_W, SSD_N), lambda i: (i, 0, 0)),
                   pl.BlockSpec((SSD_W, b_), lambda i: (0, 0))],
        out_shape=[jax.ShapeDtypeStruct(state.shape, f32), jax.ShapeDtypeStruct((SSD_W, b_), f32)],
        compiler_params=_cparams(("arbitrary",)),
        name="ssd_sample_state",
    )(da, state, xdt_t, bm, cm)


def _ssd_sample_post_kernel(y_ref, xs_ref, z_ref, dsk_ref, gn_ref, o_ref):
    y = y_ref[...] + dsk_ref[...] * xs_ref[...]
    v = y * _silu(z_ref[...])
    gl = lax.broadcasted_iota(jnp.int32, v.shape, 1) < SSD_W // SSD_G
    v2 = v * v
    gw = SSD_W // SSD_G
    ss0 = jnp.sum(jnp.where(gl, v2, 0.0), axis=-1, keepdims=True)
    ss1 = jnp.sum(jnp.where(gl, 0.0, v2), axis=-1, keepdims=True)
    rinv = jnp.where(gl, lax.rsqrt(ss0 * (1.0 / gw) + EPS), lax.rsqrt(ss1 * (1.0 / gw) + EPS))
    o_ref[...] = v * rinv * gn_ref[...]


def _ssd_sample_post(y, xs, z, wts):
    return pl.pallas_call(
        _ssd_sample_post_kernel,
        out_shape=jax.ShapeDtypeStruct(y.shape, f32),
        name="ssd_sample_post",
    )(y, xs, z, wts["d_skip_l"], wts["g_ssd"])


def _mla_sample_pre_kernel(q_ref, lat_ref, kpe_ref, wk_ref, wkt_ref, gk_ref, qabs_ref, qpe_ref, snew_ref):
    kn = _dot(lat_ref[...].astype(bf16), wk_ref[...])
    kp = pltpu.roll(kpe_ref[...], NOPE, 1)
    gk = gk_ref[...]
    lane = lax.broadcasted_iota(jnp.int32, snew_ref.shape, 1)
    snew = jnp.zeros(snew_ref.shape, f32)
    for hh in range(NH):
        sl = slice(LANE * hh, LANE * (hh + 1))
        qh = q_ref[:, sl]
        knew = _rms(kn[:, sl] + kp, gk, QK)
        snew = jnp.where(lane == hh, jnp.sum(qh * knew, axis=-1, keepdims=True) * ATTN_SCALE, snew)
        qg = qh * gk
        qabs_ref[:, sl] = _dot_exact(qg, wkt_ref[hh])
        qpe_ref[:, sl] = pltpu.roll(qg, NOPE, 1)
    snew_ref[...] = snew


def _mla_sample_pre(q, lat, kpe128, wts):
    b_ = q.shape[0]
    shapes = [(b_, NH * LANE), (b_, NH * LANE), (b_, LANE)]
    return pl.pallas_call(
        _mla_sample_pre_kernel,
        out_shape=[jax.ShapeDtypeStruct(s, f32) for s in shapes],
        name="mla_sample_pre",
    )(q, lat, kpe128, wts["w_k"], wts["w_kt"], wts["g_qk_k"])


def _attn_sample_kernel(pt_ref, lat_hbm, kpe_hbm, wkt_ref, qabs_ref, qpe_ref, snew_ref, latnew_ref, o_ref,
                        lat_buf, kpe_buf, sem, wext, latb_ref, s_ref, *, li, pages, nchunk):
    b = pl.program_id(0)
    nb = pl.num_programs(0)
    rc = pages * PAGE
    slot = b % 2

    def copies(bb, c, i, sl):
        page_idx = c * pages + i
        page = pt_ref[bb, page_idx]
        off = pl.multiple_of(page_idx * PAGE, PAGE)
        return (pltpu.make_async_copy(lat_hbm.at[li, page], lat_buf.at[sl, pl.ds(off, PAGE), :], sem.at[0, sl, c]),
                pltpu.make_async_copy(kpe_hbm.at[li, page], kpe_buf.at[sl, :, pl.ds(off, PAGE)], sem.at[1, sl, c]))

    def start_chunk(bb, c, sl):
        for i in range(pages):
            for cp in copies(bb, c, i, sl):
                cp.start()

    @pl.when(b == 0)
    def _():
        def issue(c, carry):
            start_chunk(0, c, 0)
            return carry
        lax.fori_loop(0, nchunk, issue, 0)
        wext[0:NH * HD, :] = wkt_ref[...]

    wext[NH * HD:NH * HD + HPAD, :] = qabs_ref[0]
    qpe = qpe_ref[0]
    rowi = lax.broadcasted_iota(jnp.int32, (HPAD, rc), 0)

    def scores(c, carry):
        for i in range(pages):
            for cp in copies(b, c, i, slot):
                cp.wait()

        @pl.when(b + 1 < nb)
        def _():
            start_chunk(b + 1, c, 1 - slot)

        off = pl.multiple_of(c * rc, rc)
        latb = lat_buf[slot, pl.ds(off, rc), :].astype(bf16)
        latb_ref[pl.ds(off, rc), :] = latb
        kpe = kpe_buf[slot, :, pl.ds(off, rc)]
        a = _dot_nt(wext[...], latb)
        nsq = jnp.zeros((HPAD, rc), f32)
        for hh in range(NH):
            kh = a[HD * hh:HD * (hh + 1), :]
            nsq = jnp.where(rowi == hh, jnp.sum(kh * kh, axis=0, keepdims=True), nsq)
        s_pe = _dot(qpe, kpe.astype(bf16))
        ksq = jnp.sum(kpe * kpe, axis=0, keepdims=True)
        s_ref[:, pl.ds(off, rc)] = (a[NH * HD:, :] + s_pe) * lax.rsqrt((nsq + ksq) * (1.0 / QK) + EPS) * ATTN_SCALE
        return carry

    lax.fori_loop(0, nchunk, scores, 0)

    s = s_ref[...]
    s_new = snew_ref[0][:, 0:1]
    m = jnp.maximum(jnp.max(s, axis=-1, keepdims=True), s_new)
    pe = jnp.exp(s - m)
    p_new = jnp.exp(s_new - m)
    l = jnp.sum(pe, axis=-1, keepdims=True) + p_new
    acc = _dot(pe.astype(bf16), latb_ref[...]) + p_new * latnew_ref[0]
    o_ref[0] = acc / l


def _attn_sample(page_table, cache_lat, cache_kpe_t, wkt, qabs, qpe, snew, lat_new, *, li, pages=16):
    b_, npages = page_table.shape
    nchunk = npages // pages
    assert npages % pages == 0
    seq = npages * PAGE
    grid_spec = pltpu.PrefetchScalarGridSpec(
        num_scalar_prefetch=1,
        grid=(b_,),
        in_specs=[pl.BlockSpec(memory_space=pl.ANY),
                  pl.BlockSpec(memory_space=pl.ANY),
                  pl.BlockSpec(wkt.shape, lambda b, pt: (0, 0)),
                  pl.BlockSpec((1, HPAD, KV_LORA), lambda b, pt: (b, 0, 0)),
                  pl.BlockSpec((1, HPAD, ROPE), lambda b, pt: (b, 0, 0)),
                  pl.BlockSpec((1, HPAD, LANE), lambda b, pt: (b, 0, 0)),
                  pl.BlockSpec((1, 1, KV_LORA), lambda b, pt: (b, 0, 0))],
        out_specs=pl.BlockSpec((1, HPAD, KV_LORA), lambda b, pt: (b, 0, 0)),
        scratch_shapes=[pltpu.VMEM((2, seq, KV_LORA), f32),
                        pltpu.VMEM((2, ROPE, seq), f32),
                        pltpu.SemaphoreType.DMA((2, 2, nchunk)),
                        pltpu.VMEM((NH * HD + HPAD, KV_LORA), bf16),
                        pltpu.VMEM((seq, KV_LORA), bf16),
                        pltpu.VMEM((HPAD, seq), f32)],
    )
    return pl.pallas_call(
        functools.partial(_attn_sample_kernel, li=li, pages=pages, nchunk=nchunk),
        grid_spec=grid_spec,
        out_shape=jax.ShapeDtypeStruct((b_, HPAD, KV_LORA), f32),
        compiler_params=_cparams(("arbitrary",)),
        name="attn_sample",
    )(page_table, cache_lat, cache_kpe_t, wkt, qabs, qpe, snew, lat_new)


def _mla_sample_post_kernel(o_ref, wv_ref, g_ref, out_ref):
    out_ref[...] = _rms(_dot(o_ref[...].astype(bf16), wv_ref[...]), g_ref[...], SSD_W)


def _mla_sample_post(o_lat, w_v_bd, g_out):
    return pl.pallas_call(
        _mla_sample_post_kernel,
        out_shape=jax.ShapeDtypeStruct((o_lat.shape[0], SSD_W), f32),
        name="mla_sample_post",
    )(o_lat, w_v_bd, g_out)


def _pad_heads(w, lo, hi):
    pad = [(0, 0)] * (w.ndim - 1) + [(lo, LANE - hi)]
    w = jnp.pad(w, pad)
    return w.reshape(w.shape[:-2] + (NH * LANE,))


def _rot_cols(w):
    half = ROPE // 2
    return jnp.concatenate([-w[..., half:], w[..., :half]], axis=-1)


def _pad_lanes(a, lo=0):
    return jnp.pad(a, [(0, 0)] * (a.ndim - 1) + [(lo, LANE - lo - a.shape[-1])])


def _layer_weights(p, li):
    d = p["w_in"].shape[1]
    w_in = p["w_in"][li]
    offs = np.cumsum([SSD_W, CONV_CH, NH, Q_LORA, KV_LORA, ROPE, POOL_W])
    wz, wxbc, wdt, wcq, wckv, wkpe, wu = jnp.split(w_in, offs[:-1].tolist(), axis=1)
    w_big = jnp.concatenate([wz, wxbc, wcq, wckv, wu, _pad_lanes(wkpe), _pad_lanes(_rot_cols(wkpe)), _pad_lanes(wdt)],
                            axis=1).astype(bf16)
    wq = p["w_q_up"][li]
    wq_pe = wq[..., NOPE:]
    wq_plain = _pad_heads(wq, 0, QK)
    wq_rot = _pad_heads(_rot_cols(wq_pe), NOPE, QK)
    wk = p["w_k_up"][li]
    wk_pad = _pad_heads(wk, 0, NOPE)
    wk_t = jnp.transpose(wk, (1, 2, 0))
    wv = p["w_v_up"][li]
    w_v_bd = jnp.zeros((NH, KV_LORA, NH, HD), f32)
    w_v_bd = w_v_bd.at[jnp.arange(NH), :, jnp.arange(NH), :].set(jnp.transpose(wv, (1, 0, 2)))
    wp = p["w_pool"][li]
    ng = wp.shape[0]
    w_pool_bd = jnp.zeros((ng, HD, ng, HD), f32).at[jnp.arange(ng), :, jnp.arange(ng), :].set(wp)
    return dict(
        g1=p["g_norm1"][li].reshape(1, d), w_in=w_big,
        g_q=p["g_q_lora"][li].reshape(1, Q_LORA), w_q=jnp.concatenate([wq_plain, wq_rot], axis=1).astype(bf16),
        g_qk_q=_pad_lanes(p["g_qk_q"][li].reshape(1, QK)), g_qk_k=_pad_lanes(p["g_qk_k"][li].reshape(1, QK)),
        g_kv=p["g_kv_lora"][li].reshape(1, KV_LORA),
        w_k=wk_pad.astype(bf16), w_v=wv.reshape(KV_LORA, NH * HD).astype(bf16),
        w_kt=jnp.pad(wk_t, ((0, 0), (0, LANE - NOPE), (0, 0))),
        w_kt_flat=wk_t.reshape(NH * NOPE, KV_LORA).astype(bf16),
        w_v_bd=w_v_bd.reshape(NH * KV_LORA, NH * HD).astype(bf16),
        g_mla=p["g_mla_out"][li].reshape(1, SSD_W),
        conv_w=p["conv_w"][li], conv_b=p["conv_b"][li].reshape(1, CONV_CH),
        dt_bias=_pad_lanes(p["dt_bias"][li].reshape(1, NH)),
        a_neg=_pad_lanes(-jnp.exp(p["a_log"][li].astype(f32)).reshape(1, NH)),
        d_skip_l=jnp.repeat(p["d_skip"][li], HD).reshape(1, SSD_W), g_ssd=p["g_ssd_norm"][li].reshape(1, SSD_W),
        w_pool=w_pool_bd.reshape(POOL_W, POOL_W).astype(bf16), pool_scale=p["pool_scale"][li].reshape(1, POOL_W),
        g2=p["g_norm2"][li].reshape(1, d), w_out=p["w_out"][li].astype(bf16),
        w_gate=p["w_gate"][li].astype(bf16), w_up=p["w_up"][li].astype(bf16), w_down=p["w_down"][li].astype(bf16),
    )


def _rope_tables(pos):
    half = ROPE // 2
    inv = 1.0 / (ROPE_THETA ** (jnp.arange(half, dtype=f32) / half))
    ang = pos.astype(f32)[:, None] * inv[None, :]
    cos2 = jnp.concatenate([jnp.cos(ang)] * 2, axis=-1)
    sin2 = jnp.concatenate([jnp.sin(ang)] * 2, axis=-1)
    n = pos.shape[0]
    cosq = jnp.concatenate([jnp.ones((n, NOPE), f32), cos2, jnp.zeros((n, LANE - QK), f32)], axis=-1)
    sinq = _pad_lanes(sin2, NOPE)
    return cosq, sinq, _pad_lanes(cos2), _pad_lanes(sin2)


def kernel(x_prompt, x_sample, cache_kv_latent, cache_k_rope, state_ssm, state_conv, state_pool, page_table, c_prompt, c_sample, w_ada, b_ada, g_norm1, w_in, conv_w, conv_b, dt_bias, a_log, d_skip, g_ssd_norm, g_q_lora, w_q_up, g_kv_lora, w_k_up, w_v_up, g_qk_q, g_qk_k, g_mla_out, w_pool, pool_scale, w_out, g_norm2, w_gate, w_up, w_down):
    params = dict(g_norm1=g_norm1, w_in=w_in, conv_w=conv_w, conv_b=conv_b, dt_bias=dt_bias, a_log=a_log, d_skip=d_skip,
                  g_ssd_norm=g_ssd_norm, g_q_lora=g_q_lora, w_q_up=w_q_up, g_kv_lora=g_kv_lora, w_k_up=w_k_up,
                  w_v_up=w_v_up, g_qk_q=g_qk_q, g_qk_k=g_qk_k, g_mla_out=g_mla_out, w_pool=w_pool, pool_scale=pool_scale,
                  w_out=w_out, g_norm2=g_norm2, w_gate=w_gate, w_up=w_up, w_down=w_down)
    depth = w_ada.shape[0]
    bp, seq, d = x_prompt.shape
    bs = x_sample.shape[0]
    past = page_table.shape[1] * PAGE

    mod = _ada_mod(jnp.concatenate([c_prompt, c_sample], axis=0), w_ada, b_ada).reshape(depth, bp + bs, 6, d)
    tabs_p = _rope_tables(jnp.arange(seq, dtype=jnp.int32))
    tabs_s = _rope_tables(jnp.full((1,), past, jnp.int32))
    cache_kpe_t = jnp.swapaxes(cache_k_rope, 2, 3)

    yp = x_prompt
    ys = x_sample.reshape(1, bs, d)
    p_new = [[] for _ in range(5)]
    s_new = [[] for _ in range(5)]
    for li in range(depth):
        wts = _layer_weights(params, li)
        mp = [mod[li, :bp, i].reshape(bp, 1, d) for i in range(6)]
        ms = [mod[li, bp:, i].reshape(1, bs, d) for i in range(6)]

        z, xbc, dt, u, lat, kpe, q, k, v = _inproj(yp, mp[0], mp[1], wts, tabs_p, tm=512, with_kv=True)
        ssd_out, h_t = _ssd_prompt(xbc, z, dt, wts)
        mla_out = _attn_prompt(q, k, v, wts["g_mla"], tq=512)
        pool_out = _pool_prompt(u, wts["w_pool"], wts["pool_scale"])
        yp = _ffn(yp, ssd_out, mla_out, pool_out, (mp[2], mp[3], mp[4], mp[5]), wts, tm=256)
        for lst, val in zip(p_new, (lat, kpe, h_t.reshape(bp, NH, HD, SSD_N), xbc[:, seq - (CONV_K - 1):],
                                    u[:, seq - POOL_BUF:])):
            lst.append(val)

        z, xbc, dt, u, lat, kpe, q, kpe128 = _inproj(ys, ms[0], ms[1], wts, tabs_s, tm=bs, with_kv=False)
        z, xbc, dt, u, lat, kpe, q, kpe128 = (a[0] for a in (z, xbc, dt, u, lat, kpe, q, kpe128))
        conv_buf = state_conv[li]
        xs, bm, cm, xdt_t, da = _ssd_sample_pre(xbc, jnp.transpose(conv_buf, (1, 0, 2)), dt, wts)
        h_new, y_t = _ssd_sample_state(da[:, :8], state_ssm[li].reshape(bs, SSD_W, SSD_N), xdt_t, bm, cm)
        ssd_out = _ssd_sample_post(y_t.T, xs, z, wts)
        qabs, qpe, snew = _mla_sample_pre(q, lat, kpe128, wts)
        qabs = jnp.pad(qabs.reshape(bs, NH, LANE), ((0, 0), (0, HPAD - NH), (0, 0))).astype(bf16)
        qpe = jnp.pad(qpe.reshape(bs, NH, LANE)[:, :, :ROPE], ((0, 0), (0, HPAD - NH), (0, 0))).astype(bf16)
        snew_b = jnp.broadcast_to(jnp.pad(snew[:, :NH], ((0, 0), (0, HPAD - NH)))[:, :, None], (bs, HPAD, LANE))
        o_lat = _attn_sample(page_table, cache_kv_latent, cache_kpe_t, wts["w_kt_flat"], qabs, qpe, snew_b,
                             lat.reshape(bs, 1, KV_LORA), li=li)
        mla_out = _mla_sample_post(o_lat[:, :NH].reshape(bs, NH * KV_LORA), wts["w_v_bd"], wts["g_mla"])
        pool_buf = state_pool[li]
        pool_out = _pool_sample(u, jnp.transpose(pool_buf, (1, 0, 2)), wts["w_pool"], wts["pool_scale"])
        ys = _ffn(ys, ssd_out[None], mla_out[None], pool_out[None], (ms[2], ms[3], ms[4], ms[5]), wts, tm=bs)
        conv_new = jnp.concatenate([conv_buf[:, 1:], xbc[:, None, :]], axis=1)
        pool_new = jnp.concatenate([pool_buf[:, 1:], u[:, None, :]], axis=1)
        for lst, val in zip(s_new, (lat[:, None, :], kpe[:, None, :], h_new.reshape(bs, NH, HD, SSD_N), conv_new, pool_new)):
            lst.append(val)

    outs_p = [jnp.stack(vv, axis=0) for vv in p_new]
    outs_s = [jnp.stack(vv, axis=0) for vv in s_new]
    return (yp, ys.reshape(bs, 1, d), *outs_p, *outs_s)
```

```python
import functools
import math

import jax
import jax.numpy as jnp
import numpy as np
from jax import lax
from jax.experimental import pallas as pl
from jax.experimental.pallas import tpu as pltpu

f32 = jnp.float32
bf16 = jnp.bfloat16
HIGHEST = lax.Precision.HIGHEST

EPS = 1e-6
PAGE = 128
NH = 6
HD = 64
NOPE = 64
ROPE = 32
QK = NOPE + ROPE
SSD_W = NH * HD
SSD_G = 2
SSD_N = 128
CONV_K = 4
CONV_CH = SSD_W + 2 * SSD_G * SSD_N
Q_LORA = 256
KV_LORA = 128
POOL_W = 256
POOL_BUF = 15
ROPE_THETA = 10000.0
ATTN_SCALE = QK ** -0.5
PROMPT_Q_SCALE = ATTN_SCALE * math.log2(math.e)
LANE = 128
HPAD = 16
VMEM_LIMIT = 56 * 1024 * 1024

_C_Z, _C_XBC, _C_CQ, _C_CKV, _C_U, _C_KA, _C_KB, _C_DT, _C_END = 0, 384, 1280, 1536, 1664, 1920, 2048, 2176, 2304


def _cparams(sem):
    return pltpu.CompilerParams(dimension_semantics=sem, vmem_limit_bytes=VMEM_LIMIT)


def _silu(x):
    return x * jax.nn.sigmoid(x)


def _softplus(x):
    return jnp.maximum(x, 0.0) + jnp.log1p(jnp.exp(-jnp.abs(x)))


def _rms(x, g, n):
    ms = jnp.sum(x * x, axis=-1, keepdims=True) * (1.0 / n)
    return x * lax.rsqrt(ms + EPS) * g


def _dot(a, b):
    return jnp.dot(a, b, preferred_element_type=f32)


def _dot_nt(a, b):
    return lax.dot_general(a, b, (((1,), (1,)), ((), ())), preferred_element_type=f32)


def _dot_tn(a, b):
    return lax.dot_general(a, b, (((0,), (0,)), ((), ())), preferred_element_type=f32)


def _dot_exact(a, b):
    return jnp.dot(a, b, precision=HIGHEST, preferred_element_type=f32)


def _head_expand():
    r = lax.broadcasted_iota(jnp.int32, (LANE, SSD_W), 0)
    c = lax.broadcasted_iota(jnp.int32, (LANE, SSD_W), 1)
    return jnp.where(c // HD == r, 1.0, 0.0).astype(f32)


def _ada_kernel(c_ref, w_ref, b_ref, o_ref):
    s = _silu(c_ref[...]).astype(bf16)
    o_ref[0] = _dot(s, w_ref[0].astype(bf16)) + b_ref[0]


def _ada_mod(c_all, w_ada, b_ada):
    depth, d, n6 = w_ada.shape
    rows = c_all.shape[0]
    tn = 1536
    return pl.pallas_call(
        _ada_kernel,
        grid=(depth, n6 // tn),
        in_specs=[pl.BlockSpec((rows, d), lambda l, j: (0, 0)),
                  pl.BlockSpec((1, d, tn), lambda l, j: (l, 0, j)),
                  pl.BlockSpec((1, 1, tn), lambda l, j: (l, 0, j))],
        out_specs=pl.BlockSpec((1, rows, tn), lambda l, j: (l, 0, j)),
        out_shape=jax.ShapeDtypeStruct((depth, rows, n6), f32),
        compiler_params=_cparams(("arbitrary", "arbitrary")),
        name="ada_mod",
    )(c_all, w_ada, b_ada.reshape(depth, 1, n6))


def _inproj_kernel(*refs, with_kv):
    (x_ref, sh_ref, sc_ref, g1_ref, w_ref, gq_ref, wq_ref, cosq_ref, sinq_ref, gqk_ref, gkv_ref,
     cosk_ref, sink_ref) = refs[:13]
    if with_kv:
        wk_ref, gk_ref, wv_ref = refs[13:16]
        z_ref, xbc_ref, dt_ref, u_ref, lat_ref, kpe_ref, q_ref, k_ref, v_ref = refs[16:]
    else:
        z_ref, xbc_ref, dt_ref, u_ref, lat_ref, kpe_ref, q_ref, kpe128_ref = refs[13:]
    d = x_ref.shape[-1]
    h = _rms(x_ref[0], g1_ref[...], d) * (1.0 + sc_ref[0]) + sh_ref[0]
    proj = _dot(h.astype(bf16), w_ref[...])
    z_ref[0] = proj[:, _C_Z:_C_XBC]
    xbc_ref[0] = proj[:, _C_XBC:_C_CQ]
    u_ref[0] = proj[:, _C_U:_C_KA]
    dt_ref[0] = proj[:, _C_DT:_C_END]
    cqn = _rms(proj[:, _C_CQ:_C_CKV], gq_ref[...], Q_LORA).astype(bf16)
    qq = _dot(cqn, wq_ref[...])
    cosq = cosq_ref[...]
    sinq = sinq_ref[...]
    gqk = gqk_ref[...]
    q_scale = PROMPT_Q_SCALE if with_kv else 1.0
    for hh in range(NH):
        qh = qq[:, LANE * hh:LANE * (hh + 1)] * cosq + qq[:, NH * LANE + LANE * hh:NH * LANE + LANE * (hh + 1)] * sinq
        q_ref[0, :, LANE * hh:LANE * (hh + 1)] = (_rms(qh, gqk, QK) * q_scale).astype(q_ref.dtype)
    lat = _rms(proj[:, _C_CKV:_C_U], gkv_ref[...], KV_LORA)
    lat_ref[0] = lat
    kper = proj[:, _C_KA:_C_KB] * cosk_ref[...] + proj[:, _C_KB:_C_DT] * sink_ref[...]
    kpe_ref[0] = kper[:, :ROPE]
    if with_kv:
        latb = lat.astype(bf16)
        kn = _dot(latb, wk_ref[...])
        kp = pltpu.roll(kper, NOPE, 1)
        gk = gk_ref[...]
        for hh in range(NH):
            kh = kn[:, LANE * hh:LANE * (hh + 1)] + kp
            k_ref[0, :, LANE * hh:LANE * (hh + 1)] = _rms(kh, gk, QK).astype(bf16)
        v_ref[0] = _dot_nt(wv_ref[...], latb).astype(bf16)
    else:
        kpe128_ref[0] = kper


def _inproj(x, shift, scale, wts, tabs, *, tm, with_kv):
    g_, t_, d = x.shape
    tmod = shift.shape[1]
    nt = t_ // tm
    per_row = tmod != 1
    mod_spec = pl.BlockSpec((1, tm if per_row else 1, d), (lambda g, t: (g, t, 0)) if per_row else (lambda g, t: (g, 0, 0)))
    cosq, sinq, cosk, sink = tabs
    tab_rows = cosq.shape[0] != 1
    tab_spec = pl.BlockSpec((tm if tab_rows else 1, LANE), (lambda g, t: (t, 0)) if tab_rows else (lambda g, t: (0, 0)))

    def full(a):
        return pl.BlockSpec(a.shape, lambda g, t: (0,) * a.ndim)

    def row(c):
        return pl.BlockSpec((1, tm, c), lambda g, t: (g, t, 0))

    ins = [x, shift, scale, wts["g1"], wts["w_in"], wts["g_q"], wts["w_q"], cosq, sinq, wts["g_qk_q"], wts["g_kv"], cosk, sink]
    specs = [row(d), mod_spec, mod_spec, full(wts["g1"]), full(wts["w_in"]), full(wts["g_q"]), full(wts["w_q"]), tab_spec,
             tab_spec, full(wts["g_qk_q"]), full(wts["g_kv"]), tab_spec, tab_spec]
    widths = [SSD_W, CONV_CH, LANE, POOL_W, KV_LORA, ROPE, NH * LANE]
    dtypes = [f32, f32, f32, f32, f32, f32, bf16 if with_kv else f32]
    if with_kv:
        ins += [wts["w_k"], wts["g_qk_k"], wts["w_v"]]
        specs += [full(wts["w_k"]), full(wts["g_qk_k"]), full(wts["w_v"])]
        widths += [NH * LANE, SSD_W]
        dtypes += [bf16, bf16]
    else:
        widths += [LANE]
        dtypes += [f32]
    out_specs = [row(c) for c in widths]
    out_shape = [jax.ShapeDtypeStruct((g_, t_, c), dt) for c, dt in zip(widths, dtypes)]
    if with_kv:
        out_specs[-1] = pl.BlockSpec((1, SSD_W, tm), lambda g, t: (g, 0, t))
        out_shape[-1] = jax.ShapeDtypeStruct((g_, SSD_W, t_), bf16)
    return pl.pallas_call(
        functools.partial(_inproj_kernel, with_kv=with_kv),
        grid=(g_, nt),
        in_specs=specs,
        out_specs=out_specs,
        out_shape=out_shape,
        compiler_params=_cparams(("arbitrary", "arbitrary")),
        name="inproj_kv" if with_kv else "inproj",
    )(*ins)


def _ssd_prompt_kernel(xbc_ref, z_ref, dt_ref, cw_ref, cb_ref, dtb_ref, a_ref, dsk_ref, gn_ref,
                       y_ref, st_ref, ext_ref):
    c = pl.program_id(1)
    t_ = xbc_ref.shape[1]

    @pl.when(c == 0)
    def _():
        ext_ref[0:8, :] = jnp.zeros((8, CONV_CH), f32)
        st_ref[...] = jnp.zeros(st_ref.shape, f32)

    x = xbc_ref[0]
    ext_ref[8:8 + t_, :] = x
    cw = cw_ref[...]
    conv = cb_ref[...] + cw[3:4] * x
    for k in range(CONV_K - 1):
        conv = conv + cw[k:k + 1] * ext_ref[5 + k:5 + k + t_, :]
    ext_ref[0:8, :] = x[t_ - 8:, :]
    act = _silu(conv)
    xs = act[:, :SSD_W]
    bm = [act[:, SSD_W + SSD_N * g:SSD_W + SSD_N * (g + 1)].astype(bf16) for g in range(SSD_G)]
    cm = [act[:, SSD_W + SSD_N * (SSD_G + g):SSD_W + SSD_N * (SSD_G + g + 1)].astype(bf16) for g in range(SSD_G)]

    dt = _softplus(dt_ref[0] + dtb_ref[...])
    a = dt * a_ref[...]
    ri = lax.broadcasted_iota(jnp.int32, (t_, t_), 0)
    ci = lax.broadcasted_iota(jnp.int32, (t_, t_), 1)
    tril = ri >= ci
    acum = _dot_exact(jnp.where(tril, 1.0, 0.0).astype(f32), a)
    acum_t = acum.T
    expand = _head_expand()
    dt_l = _dot_exact(dt, expand)
    acum_l = _dot_exact(acum, expand)
    last_l = acum_l[t_ - 1:t_, :]
    e_l = jnp.exp(acum_l)
    decay_l = jnp.exp(last_l - acum_l)
    xdt = xs * dt_l
    xdt_w = xdt * decay_l

    cb = [_dot_nt(cm[g], bm[g]) for g in range(SSD_G)]
    lane = lax.broadcasted_iota(jnp.int32, (t_, LANE), 1)
    rowi = lax.broadcasted_iota(jnp.int32, (LANE, SSD_N), 0)
    first = lane < HD
    rfirst = rowi < HD
    y_cols = []
    for j in range(NH // 2):
        sl = slice(LANE * j, LANE * (j + 1))
        h0, h1 = 2 * j, 2 * j + 1
        g0, g1 = h0 // (NH // SSD_G), h1 // (NH // SSD_G)
        xj = xdt[:, sl].astype(bf16)
        yd = []
        for hh, gg in ((h0, g0), (h1, g1)):
            seg = jnp.where(tril, jnp.exp(jnp.minimum(acum[:, hh:hh + 1] - acum_t[hh:hh + 1, :], 0.0)), 0.0)
            yd.append(_dot((cb[gg] * seg).astype(bf16), xj))
        y_diag = jnp.where(first, yd[0], yd[1])
        hp = st_ref[0, sl, :]
        hpb = hp.astype(bf16)
        xw = xdt_w[:, sl].astype(bf16)
        if g0 == g1:
            y_off = _dot_nt(cm[g0], hpb)
            s_new = _dot_tn(xw, bm[g0])
        else:
            y_off = jnp.where(first, _dot_nt(cm[g0], hpb), _dot_nt(cm[g1], hpb))
            s_new = jnp.where(rfirst, _dot_tn(xw, bm[g0]), _dot_tn(xw, bm[g1]))
        tot = jnp.where(rfirst, jnp.exp(acum[t_ - 1:t_, h0:h0 + 1]), jnp.exp(acum[t_ - 1:t_, h1:h1 + 1]))
        st_ref[0, sl, :] = tot * hp + s_new
        y_cols.append(y_diag + y_off * e_l[:, sl])
    y = jnp.concatenate(y_cols, axis=-1) + dsk_ref[...] * xs
    v = y * _silu(z_ref[0])
    gl = lax.broadcasted_iota(jnp.int32, (t_, SSD_W), 1) < SSD_W // SSD_G
    v2 = v * v
    ss0 = jnp.sum(jnp.where(gl, v2, 0.0), axis=-1, keepdims=True)
    ss1 = jnp.sum(jnp.where(gl, 0.0, v2), axis=-1, keepdims=True)
    gw = SSD_W // SSD_G
    rinv = jnp.where(gl, lax.rsqrt(ss0 * (1.0 / gw) + EPS), lax.rsqrt(ss1 * (1.0 / gw) + EPS))
    y_ref[0] = v * rinv * gn_ref[...]


def _ssd_prompt(xbc, z, dt, wts):
    b_, l_, _ = xbc.shape
    t_ = 128
    nc = l_ // t_

    def full(a):
        return pl.BlockSpec(a.shape, lambda b, c: (0,) * a.ndim)

    def row(w):
        return pl.BlockSpec((1, t_, w), lambda b, c: (b, c, 0))

    small = [wts["conv_w"], wts["conv_b"], wts["dt_bias"], wts["a_neg"], wts["d_skip_l"], wts["g_ssd"]]
    return pl.pallas_call(
        _ssd_prompt_kernel,
        grid=(b_, nc),
        in_specs=[row(CONV_CH), row(SSD_W), row(LANE)] + [full(a) for a in small],
        out_specs=[row(SSD_W), pl.BlockSpec((1, SSD_W, SSD_N), lambda b, c: (b, 0, 0))],
        out_shape=[jax.ShapeDtypeStruct((b_, l_, SSD_W), f32), jax.ShapeDtypeStruct((b_, SSD_W, SSD_N), f32)],
        scratch_shapes=[pltpu.VMEM((8 + t_, CONV_CH), f32)],
        compiler_params=_cparams(("arbitrary", "arbitrary")),
        name="ssd_prompt",
    )(xbc, z, dt, *small)


def _attn_prompt_kernel(q_ref, k_ref, vt_ref, g_ref, o_ref, m_ref, l_ref, acc_ref, *, tq):
    qi = pl.program_id(1)
    m_ref[...] = jnp.full(m_ref.shape, -1e30, f32)
    l_ref[...] = jnp.zeros(l_ref.shape, f32)
    acc_ref[...] = jnp.zeros(acc_ref.shape, f32)

    def tile(j, diagonal):
        off = pl.multiple_of(j * tq, tq)
        kblk = k_ref[0, pl.ds(off, tq), :]
        vblk = vt_ref[0, :, pl.ds(off, tq)]
        if diagonal:
            masked = lax.broadcasted_iota(jnp.int32, (tq, tq), 0) > lax.broadcasted_iota(jnp.int32, (tq, tq), 1)

        def qk(hh):
            return _dot_nt(kblk[:, LANE * hh:LANE * (hh + 1)], q_ref[0, :, LANE * hh:LANE * (hh + 1)])

        s_next = qk(0)
        for hh in range(NH):
            s = s_next
            if hh + 1 < NH:
                s_next = qk(hh + 1)
            if diagonal:
                s = jnp.where(masked, -1e30, s)
            m_old = m_ref[hh]
            m_new = jnp.maximum(m_old, jnp.max(s, axis=0, keepdims=True))
            alpha = jnp.exp2(m_old - m_new)
            pe = jnp.exp2(s - m_new)
            l_ref[hh] = alpha * l_ref[hh] + jnp.sum(pe.reshape(tq // 8, 8, tq), axis=0)
            m_ref[hh] = m_new
            rows = slice(HD * hh, HD * (hh + 1))
            acc_ref[rows, :] = acc_ref[rows, :] * alpha + _dot(vblk[rows, :], pe.astype(bf16))

    def body(j, carry):
        tile(j, False)
        return carry

    lax.fori_loop(0, qi, body, 0)
    tile(qi, True)
    parts = []
    for hh in range(NH):
        l = jnp.sum(l_ref[hh], axis=0, keepdims=True)
        parts.append(acc_ref[HD * hh:HD * (hh + 1), :] * (1.0 / l))
    o_t = jnp.concatenate(parts, axis=0)
    ms = jnp.sum(o_t * o_t, axis=0, keepdims=True) * (1.0 / SSD_W)
    o_ref[0] = (o_t * lax.rsqrt(ms + EPS)).T * g_ref[...]


def _attn_prompt(q, k, v_t, g_out, *, tq):
    b_, l_, _ = q.shape
    return pl.pallas_call(
        functools.partial(_attn_prompt_kernel, tq=tq),
        grid=(b_, l_ // tq),
        in_specs=[pl.BlockSpec((1, tq, NH * LANE), lambda b, i: (b, i, 0)),
                  pl.BlockSpec((1, l_, NH * LANE), lambda b, i: (b, 0, 0)),
                  pl.BlockSpec((1, SSD_W, l_), lambda b, i: (b, 0, 0)),
                  pl.BlockSpec(g_out.shape, lambda b, i: (0, 0))],
        out_specs=pl.BlockSpec((1, tq, SSD_W), lambda b, i: (b, i, 0)),
        out_shape=jax.ShapeDtypeStruct((b_, l_, SSD_W), f32),
        scratch_shapes=[pltpu.VMEM((NH, 1, tq), f32), pltpu.VMEM((NH, 8, tq), f32), pltpu.VMEM((SSD_W, tq), f32)],
        compiler_params=_cparams(("arbitrary", "arbitrary")),
        name="attn_prompt",
    )(q, k, v_t, g_out)


def _pool_select(lane, a, b, c, d):
    return jnp.where(lane < 64, a, jnp.where(lane < 128, b, jnp.where(lane < 192, c, d)))


def _pool_prompt_kernel(u_ref, w_ref, sc_ref, o_ref, e1, e2, e4, e8, *, rt):
    t_ = u_ref.shape[1]
    hist = 16
    for e in (e1, e2, e4, e8):
        e[0:hist, :] = jnp.zeros((hist, POOL_W), f32)
    e1[hist:hist + t_, :] = u_ref[0]
    lane = lax.broadcasted_iota(jnp.int32, (rt, POOL_W), 1)
    win = _pool_select(lane, 2, 4, 8, 16)
    for i in range(t_ // rt):
        r0 = hist + i * rt
        a = e1[r0:r0 + rt, :]
        s2 = a + e1[r0 - 1:r0 - 1 + rt, :]
        e2[r0:r0 + rt, :] = s2
        s4 = s2 + e2[r0 - 2:r0 - 2 + rt, :]
        e4[r0:r0 + rt, :] = s4
        s8 = s4 + e4[r0 - 4:r0 - 4 + rt, :]
        e8[r0:r0 + rt, :] = s8
        s16 = s8 + e8[r0 - 8:r0 - 8 + rt, :]
        pos = lax.broadcasted_iota(jnp.int32, (rt, POOL_W), 0) + i * rt
        cnt = jnp.minimum(pos + 1, win).astype(f32)
        pooled = _pool_select(lane, s2, s4, s8, s16) / cnt - a
        o_ref[0, i * rt:(i + 1) * rt, :] = _dot(pooled.astype(bf16), w_ref[...]) * sc_ref[...]


def _pool_prompt(u, w_bd, scale):
    b_, l_, _ = u.shape
    return pl.pallas_call(
        functools.partial(_pool_prompt_kernel, rt=256),
        grid=(b_,),
        in_specs=[pl.BlockSpec((1, l_, POOL_W), lambda b: (b, 0, 0)),
                  pl.BlockSpec(w_bd.shape, lambda b: (0, 0)),
                  pl.BlockSpec(scale.shape, lambda b: (0, 0))],
        out_specs=pl.BlockSpec((1, l_, POOL_W), lambda b: (b, 0, 0)),
        out_shape=jax.ShapeDtypeStruct((b_, l_, POOL_W), f32),
        scratch_shapes=[pltpu.VMEM((16 + l_, POOL_W), f32)] * 4,
        compiler_params=_cparams(("arbitrary",)),
        name="pool_prompt",
    )(u, w_bd, scale)


def _pool_sample_kernel(u_ref, buf_ref, w_ref, sc_ref, o_ref):
    u = u_ref[...]
    run = u
    sums = {}
    for i in range(1, 16):
        run = run + buf_ref[POOL_BUF - i]
        if i + 1 in (2, 4, 8, 16):
            sums[i + 1] = run * (1.0 / (i + 1))
    lane = lax.broadcasted_iota(jnp.int32, u.shape, 1)
    pooled = _pool_select(lane, sums[2], sums[4], sums[8], sums[16]) - u
    o_ref[...] = _dot(pooled.astype(bf16), w_ref[...]) * sc_ref[...]


def _pool_sample(u, buf_t, w_bd, scale):
    return pl.pallas_call(
        _pool_sample_kernel,
        out_shape=jax.ShapeDtypeStruct(u.shape, f32),
        compiler_params=pltpu.CompilerParams(vmem_limit_bytes=VMEM_LIMIT),
        name="pool_sample",
    )(u, buf_t, w_bd, scale)


def _ffn_kernel(x_ref, ssd_ref, mla_ref, pool_ref, g1_ref, sh_ref, sc_ref, g2_ref, gn_ref,
                wo_ref, wg_ref, wu_ref, wd_ref, o_ref, *, nch):
    d = x_ref.shape[-1]
    mix = _dot(ssd_ref[0].astype(bf16), wo_ref[0:SSD_W, :])
    mix = mix + _dot(mla_ref[0].astype(bf16), wo_ref[SSD_W:2 * SSD_W, :])
    mix = mix + _dot(pool_ref[0].astype(bf16), wo_ref[2 * SSD_W:, :])
    x1 = x_ref[0] + g1_ref[0] * mix
    h2 = (_rms(x1, gn_ref[...], d) * (1.0 + sc_ref[0]) + sh_ref[0]).astype(bf16)
    hc = wg_ref.shape[1] // nch
    acc = jnp.zeros(x1.shape, f32)
    for c in range(nch):
        gate = _dot(h2, wg_ref[:, hc * c:hc * (c + 1)])
        up = _dot(h2, wu_ref[:, hc * c:hc * (c + 1)])
        acc = acc + _dot((_silu(gate) * up).astype(bf16), wd_ref[hc * c:hc * (c + 1), :])
    o_ref[0] = x1 + g2_ref[0] * acc


def _ffn(x, ssd, mla, pool, mods, wts, *, tm):
    g_, t_, d = x.shape
    gate1, shift2, scale2, gate2 = mods
    per_row = gate1.shape[1] != 1
    mod_spec = pl.BlockSpec((1, tm if per_row else 1, d), (lambda g, t: (g, t, 0)) if per_row else (lambda g, t: (g, 0, 0)))

    def full(a):
        return pl.BlockSpec(a.shape, lambda g, t: (0,) * a.ndim, pipeline_mode=pl.Buffered(1))

    def row(c):
        return pl.BlockSpec((1, tm, c), lambda g, t: (g, t, 0))

    ws = [wts["g2"], wts["w_out"], wts["w_gate"], wts["w_up"], wts["w_down"]]
    return pl.pallas_call(
        functools.partial(_ffn_kernel, nch=2),
        grid=(g_, t_ // tm),
        in_specs=[row(d), row(SSD_W), row(SSD_W), row(POOL_W), mod_spec, mod_spec, mod_spec, mod_spec] + [full(a) for a in ws],
        out_specs=row(d),
        out_shape=jax.ShapeDtypeStruct((g_, t_, d), f32),
        compiler_params=_cparams(("arbitrary", "arbitrary")),
        name="outproj_ffn",
    )(x, ssd, mla, pool, gate1, shift2, scale2, gate2, *ws)


def _ssd_sample_pre_kernel(xbc_ref, buf_ref, dt_ref, cw_ref, cb_ref, dtb_ref, a_ref,
                           xs_ref, b_ref, c_ref, xdt_t_ref, da_ref):
    cw = cw_ref[...]
    conv = cb_ref[...] + cw[3:4] * xbc_ref[...]
    for k in range(CONV_K - 1):
        conv = conv + cw[k:k + 1] * buf_ref[k]
    act = _silu(conv)
    xs = act[:, :SSD_W]
    xs_ref[...] = xs
    b_ref[...] = act[:, SSD_W:SSD_W + SSD_G * SSD_N]
    c_ref[...] = act[:, SSD_W + SSD_G * SSD_N:]
    dt = _softplus(dt_ref[...] + dtb_ref[...])
    da_ref[...] = jnp.exp(dt * a_ref[...])
    xdt = xs * _dot_exact(dt, _head_expand())
    for j in range(SSD_W // LANE):
        xdt_t_ref[LANE * j:LANE * (j + 1), :] = xdt[:, LANE * j:LANE * (j + 1)].T


def _ssd_sample_pre(xbc, buf_t, dt, wts):
    b_ = xbc.shape[0]
    shapes = [(b_, SSD_W), (b_, SSD_G * SSD_N), (b_, SSD_G * SSD_N), (SSD_W, b_), (b_, LANE)]
    return pl.pallas_call(
        _ssd_sample_pre_kernel,
        out_shape=[jax.ShapeDtypeStruct(s, f32) for s in shapes],
        compiler_params=pltpu.CompilerParams(vmem_limit_bytes=VMEM_LIMIT),
        name="ssd_sample_pre",
    )(xbc, buf_t, dt, wts["conv_w"], wts["conv_b"], wts["dt_bias"], wts["a_neg"])


def _ssd_sample_state_kernel(da_ref, st_ref, xdt_t_ref, b_ref, c_ref, new_ref, y_t_ref, *, tb):
    i = pl.program_id(0)

    @pl.when(i == 0)
    def _():
        y_t_ref[...] = jnp.zeros(y_t_ref.shape, f32)

    nb = xdt_t_ref.shape[1]
    lane = lax.broadcasted_iota(jnp.int32, (HD, nb), 1)
    for bl in range(tb):
        b = i * tb + bl
        sel = lane == b
        for hh in range(NH):
            g = hh // (NH // SSD_G)
            rs = slice(HD * hh, HD * (hh + 1))
            col = jnp.sum(jnp.where(sel, xdt_t_ref[rs, :], 0.0), axis=1, keepdims=True)
            brow = b_ref[bl:bl + 1, SSD_N * g:SSD_N * (g + 1)]
            crow = c_ref[bl:bl + 1, SSD_N * g:SSD_N * (g + 1)]
            new = da_ref[b, hh] * st_ref[bl, rs, :] + col * brow
            new_ref[bl, rs, :] = new
            ycol = jnp.sum(new * crow, axis=1, keepdims=True)
            y_t_ref[rs, :] = jnp.where(sel, ycol, y_t_ref[rs, :])


def _ssd_sample_state(da, state, xdt_t, bm, cm, *, tb=8):
    b_ = state.shape[0]
    return pl.pallas_call(
        functools.partial(_ssd_sample_state_kernel, tb=tb),
        grid=(b_ // tb,),
        in_specs=[pl.BlockSpec(memory_space=pltpu.SMEM),
                  pl.BlockSpec((tb, SSD_W, SSD_N), lambda i: (i, 0, 0)),
                  pl.BlockSpec(xdt_t.shape, lambda i: (0, 0)),
                  pl.BlockSpec((tb, SSD_G * SSD_N), lambda i: (i, 0)),
                  pl.BlockSpec((tb, SSD_G * SSD_N), lambda i: (i, 0))],
        out_specs=[pl.BlockSpec((tb, SSD_W, SSD_N), lambda i: (i, 0, 0)),
                   pl.BlockSpec((SSD_W, b_), lambda i: (0, 0))],
        out_shape=[jax.ShapeDtypeStruct(state.shape, f32), jax.ShapeDtypeStruct((SSD_W, b_), f32)],
        compiler_params=_cparams(("arbitrary",)),
        name="ssd_sample_state",
    )(da, state, xdt_t, bm, cm)


def _ssd_sample_post_kernel(y_ref, xs_ref, z_ref, dsk_ref, gn_ref, o_ref):
    y = y_ref[...] + dsk_ref[...] * xs_ref[...]
    v = y * _silu(z_ref[...])
    gl = lax.broadcasted_iota(jnp.int32, v.shape, 1) < SSD_W // SSD_G
    v2 = v * v
    gw = SSD_W // SSD_G
    ss0 = jnp.sum(jnp.where(gl, v2, 0.0), axis=-1, keepdims=True)
    ss1 = jnp.sum(jnp.where(gl, 0.0, v2), axis=-1, keepdims=True)
    rinv = jnp.where(gl, lax.rsqrt(ss0 * (1.0 / gw) + EPS), lax.rsqrt(ss1 * (1.0 / gw) + EPS))
    o_ref[...] = v * rinv * gn_ref[...]


def _ssd_sample_post(y, xs, z, wts):
    return pl.pallas_call(
        _ssd_sample_post_kernel,
        out_shape=jax.ShapeDtypeStruct(y.shape, f32),
        name="ssd_sample_post",
    )(y, xs, z, wts["d_skip_l"], wts["g_ssd"])


def _mla_sample_pre_kernel(q_ref, lat_ref, kpe_ref, wk_ref, wkt_ref, gk_ref, qabs_ref, qpe_ref, snew_ref):
    kn = _dot(lat_ref[...].astype(bf16), wk_ref[...])
    kp = pltpu.roll(kpe_ref[...], NOPE, 1)
    gk = gk_ref[...]
    lane = lax.broadcasted_iota(jnp.int32, snew_ref.shape, 1)
    snew = jnp.zeros(snew_ref.shape, f32)
    for hh in range(NH):
        sl = slice(LANE * hh, LANE * (hh + 1))
        qh = q_ref[:, sl]
        knew = _rms(kn[:, sl] + kp, gk, QK)
        snew = jnp.where(lane == hh, jnp.sum(qh * knew, axis=-1, keepdims=True) * ATTN_SCALE, snew)
        qg = qh * gk
        qabs_ref[:, sl] = _dot_exact(qg, wkt_ref[hh])
        qpe_ref[:, sl] = pltpu.roll(qg, NOPE, 1)
    snew_ref[...] = snew


def _mla_sample_pre(q, lat, kpe128, wts):
    b_ = q.shape[0]
    shapes = [(b_, NH * LANE), (b_, NH * LANE), (b_, LANE)]
    return pl.pallas_call(
        _mla_sample_pre_kernel,
        out_shape=[jax.ShapeDtypeStruct(s, f32) for s in shapes],
        name="mla_sample_pre",
    )(q, lat, kpe128, wts["w_k"], wts["w_kt"], wts["g_qk_k"])


def _attn_sample_kernel(pt_ref, lat_hbm, kpe_hbm, wkt_ref, qabs_ref, qpe_ref, snew_ref, latnew_ref, o_ref,
                        lat_buf, kpe_buf, sem, wext, latb_ref, s_ref, *, li, pages, nchunk):
    b = pl.program_id(0)
    nb = pl.num_programs(0)
    rc = pages * PAGE
    slot = b % 2

    def aligned(idx, size):
        return idx * size if isinstance(idx, int) else pl.multiple_of(idx * size, size)

    def copies(bb, c, i, sl):
        page_idx = c * pages + i
        page = pt_ref[bb, page_idx]
        off = aligned(page_idx, PAGE)
        return (pltpu.make_async_copy(lat_hbm.at[li, page], lat_buf.at[sl, pl.ds(off, PAGE), :], sem.at[0, sl, c]),
                pltpu.make_async_copy(kpe_hbm.at[li, page], kpe_buf.at[sl, :, pl.ds(off, PAGE)], sem.at[1, sl, c]))

    def start_chunk(bb, c, sl):
        for i in range(pages):
            for cp in copies(bb, c, i, sl):
                cp.start()

    @pl.when(b == 0)
    def _():
        def issue(c, carry):
            start_chunk(0, c, 0)
            return carry
        lax.fori_loop(0, nchunk, issue, 0)
        wext[0:NH * HD, :] = wkt_ref[...]

    wext[NH * HD:NH * HD + HPAD, :] = qabs_ref[0]
    qpe = qpe_ref[0]
    rowi = lax.broadcasted_iota(jnp.int32, (HPAD, rc), 0)

    def scores(c, carry):
        for i in range(pages):
            for cp in copies(b, c, i, slot):
                cp.wait()

        @pl.when(b + 1 < nb)
        def _():
            start_chunk(b + 1, c, 1 - slot)

        off = aligned(c, rc)
        latb = lat_buf[slot, pl.ds(off, rc), :].astype(bf16)
        latb_ref[pl.ds(off, rc), :] = latb
        kpe = kpe_buf[slot, :, pl.ds(off, rc)]
        a = _dot_nt(wext[...], latb)
        nsq = jnp.zeros((HPAD, rc), f32)
        for hh in range(NH):
            kh = a[HD * hh:HD * (hh + 1), :]
            nsq = jnp.where(rowi == hh, jnp.sum(kh * kh, axis=0, keepdims=True), nsq)
        s_pe = _dot(qpe, kpe.astype(bf16))
        ksq = jnp.sum(kpe * kpe, axis=0, keepdims=True)
        s_ref[:, pl.ds(off, rc)] = (a[NH * HD:, :] + s_pe) * lax.rsqrt((nsq + ksq) * (1.0 / QK) + EPS) * ATTN_SCALE
        return carry

    lax.fori_loop(0, nchunk, scores, 0)

    s = s_ref[...]
    s_new = snew_ref[0][:, 0:1]
    m = jnp.maximum(jnp.max(s, axis=-1, keepdims=True), s_new)
    pe = jnp.exp(s - m)
    p_new = jnp.exp(s_new - m)
    l = jnp.sum(pe, axis=-1, keepdims=True) + p_new
    acc = _dot(pe.astype(bf16), latb_ref[...]) + p_new * latnew_ref[0]
    o_ref[0] = acc / l


def _attn_sample(page_table, cache_lat, cache_kpe_t, wkt, qabs, qpe, snew, lat_new, *, li, pages=64):
    b_, npages = page_table.shape
    nchunk = npages // pages
    assert npages % pages == 0
    seq = npages * PAGE
    grid_spec = pltpu.PrefetchScalarGridSpec(
        num_scalar_prefetch=1,
        grid=(b_,),
        in_specs=[pl.BlockSpec(memory_space=pl.ANY),
                  pl.BlockSpec(memory_space=pl.ANY),
                  pl.BlockSpec(wkt.shape, lambda b, pt: (0, 0)),
                  pl.BlockSpec((1, HPAD, KV_LORA), lambda b, pt: (b, 0, 0)),
                  pl.BlockSpec((1, HPAD, ROPE), lambda b, pt: (b, 0, 0)),
                  pl.BlockSpec((1, HPAD, LANE), lambda b, pt: (b, 0, 0)),
                  pl.BlockSpec((1, 1, KV_LORA), lambda b, pt: (b, 0, 0))],
        out_specs=pl.BlockSpec((1, HPAD, KV_LORA), lambda b, pt: (b, 0, 0)),
        scratch_shapes=[pltpu.VMEM((2, seq, KV_LORA), f32),
                        pltpu.VMEM((2, ROPE, seq), f32),
                        pltpu.SemaphoreType.DMA((2, 2, nchunk)),
                        pltpu.VMEM((NH * HD + HPAD, KV_LORA), bf16),
                        pltpu.VMEM((seq, KV_LORA), bf16),
                        pltpu.VMEM((HPAD, seq), f32)],
    )
    return pl.pallas_call(
        functools.partial(_attn_sample_kernel, li=li, pages=pages, nchunk=nchunk),
        grid_spec=grid_spec,
        out_shape=jax.ShapeDtypeStruct((b_, HPAD, KV_LORA), f32),
        compiler_params=_cparams(("arbitrary",)),
        name="attn_sample",
    )(page_table, cache_lat, cache_kpe_t, wkt, qabs, qpe, snew, lat_new)


def _mla_sample_post_kernel(o_ref, wv_ref, g_ref, out_ref):
    out_ref[...] = _rms(_dot(o_ref[...].astype(bf16), wv_ref[...]), g_ref[...], SSD_W)


def _mla_sample_post(o_lat, w_v_bd, g_out):
    return pl.pallas_call(
        _mla_sample_post_kernel,
        out_shape=jax.ShapeDtypeStruct((o_lat.shape[0], SSD_W), f32),
        name="mla_sample_post",
    )(o_lat, w_v_bd, g_out)


def _pad_heads(w, lo, hi):
    pad = [(0, 0)] * (w.ndim - 1) + [(lo, LANE - hi)]
    w = jnp.pad(w, pad)
    return w.reshape(w.shape[:-2] + (NH * LANE,))


def _rot_cols(w):
    half = ROPE // 2
    return jnp.concatenate([-w[..., half:], w[..., :half]], axis=-1)


def _pad_lanes(a, lo=0):
    return jnp.pad(a, [(0, 0)] * (a.ndim - 1) + [(lo, LANE - lo - a.shape[-1])])


def _layer_weights(p, li):
    d = p["w_in"].shape[1]
    w_in = p["w_in"][li]
    offs = np.cumsum([SSD_W, CONV_CH, NH, Q_LORA, KV_LORA, ROPE, POOL_W])
    wz, wxbc, wdt, wcq, wckv, wkpe, wu = jnp.split(w_in, offs[:-1].tolist(), axis=1)
    w_big = jnp.concatenate([wz, wxbc, wcq, wckv, wu, _pad_lanes(wkpe), _pad_lanes(_rot_cols(wkpe)), _pad_lanes(wdt)],
                            axis=1).astype(bf16)
    wq = p["w_q_up"][li]
    wq_pe = wq[..., NOPE:]
    wq_plain = _pad_heads(wq, 0, QK)
    wq_rot = _pad_heads(_rot_cols(wq_pe), NOPE, QK)
    wk = p["w_k_up"][li]
    wk_pad = _pad_heads(wk, 0, NOPE)
    wk_t = jnp.transpose(wk, (1, 2, 0))
    wv = p["w_v_up"][li]
    w_v_bd = jnp.zeros((NH, KV_LORA, NH, HD), f32)
    w_v_bd = w_v_bd.at[jnp.arange(NH), :, jnp.arange(NH), :].set(jnp.transpose(wv, (1, 0, 2)))
    wp = p["w_pool"][li]
    ng = wp.shape[0]
    w_pool_bd = jnp.zeros((ng, HD, ng, HD), f32).at[jnp.arange(ng), :, jnp.arange(ng), :].set(wp)
    return dict(
        g1=p["g_norm1"][li].reshape(1, d), w_in=w_big,
        g_q=p["g_q_lora"][li].reshape(1, Q_LORA), w_q=jnp.concatenate([wq_plain, wq_rot], axis=1).astype(bf16),
        g_qk_q=_pad_lanes(p["g_qk_q"][li].reshape(1, QK)), g_qk_k=_pad_lanes(p["g_qk_k"][li].reshape(1, QK)),
        g_kv=p["g_kv_lora"][li].reshape(1, KV_LORA),
        w_k=wk_pad.astype(bf16), w_v=wv.reshape(KV_LORA, NH * HD).T.astype(bf16),
        w_kt=jnp.pad(wk_t, ((0, 0), (0, LANE - NOPE), (0, 0))),
        w_kt_flat=wk_t.reshape(NH * NOPE, KV_LORA).astype(bf16),
        w_v_bd=w_v_bd.reshape(NH * KV_LORA, NH * HD).astype(bf16),
        g_mla=p["g_mla_out"][li].reshape(1, SSD_W),
        conv_w=p["conv_w"][li], conv_b=p["conv_b"][li].reshape(1, CONV_CH),
        dt_bias=_pad_lanes(p["dt_bias"][li].reshape(1, NH)),
        a_neg=_pad_lanes(-jnp.exp(p["a_log"][li].astype(f32)).reshape(1, NH)),
        d_skip_l=jnp.repeat(p["d_skip"][li], HD).reshape(1, SSD_W), g_ssd=p["g_ssd_norm"][li].reshape(1, SSD_W),
        w_pool=w_pool_bd.reshape(POOL_W, POOL_W).astype(bf16), pool_scale=p["pool_scale"][li].reshape(1, POOL_W),
        g2=p["g_norm2"][li].reshape(1, d), w_out=p["w_out"][li].astype(bf16),
        w_gate=p["w_gate"][li].astype(bf16), w_up=p["w_up"][li].astype(bf16), w_down=p["w_down"][li].astype(bf16),
    )


def _rope_tables(pos):
    half = ROPE // 2
    inv = 1.0 / (ROPE_THETA ** (jnp.arange(half, dtype=f32) / half))
    ang = pos.astype(f32)[:, None] * inv[None, :]
    cos2 = jnp.concatenate([jnp.cos(ang)] * 2, axis=-1)
    sin2 = jnp.concatenate([jnp.sin(ang)] * 2, axis=-1)
    n = pos.shape[0]
    cosq = jnp.concatenate([jnp.ones((n, NOPE), f32), cos2, jnp.zeros((n, LANE - QK), f32)], axis=-1)
    sinq = _pad_lanes(sin2, NOPE)
    return cosq, sinq, _pad_lanes(cos2), _pad_lanes(sin2)


def kernel(x_prompt, x_sample, cache_kv_latent, cache_k_rope, state_ssm, state_conv, state_pool, page_table, c_prompt, c_sample, w_ada, b_ada, g_norm1, w_in, conv_w, conv_b, dt_bias, a_log, d_skip, g_ssd_norm, g_q_lora, w_q_up, g_kv_lora, w_k_up, w_v_up, g_qk_q, g_qk_k, g_mla_out, w_pool, pool_scale, w_out, g_norm2, w_gate, w_up, w_down):
    params = dict(g_norm1=g_norm1, w_in=w_in, conv_w=conv_w, conv_b=conv_b, dt_bias=dt_bias, a_log=a_log, d_skip=d_skip,
                  g_ssd_norm=g_ssd_norm, g_q_lora=g_q_lora, w_q_up=w_q_up, g_kv_lora=g_kv_lora, w_k_up=w_k_up,
                  w_v_up=w_v_up, g_qk_q=g_qk_q, g_qk_k=g_qk_k, g_mla_out=g_mla_out, w_pool=w_pool, pool_scale=pool_scale,
                  w_out=w_out, g_norm2=g_norm2, w_gate=w_gate, w_up=w_up, w_down=w_down)
    depth = w_ada.shape[0]
    bp, seq, d = x_prompt.shape
    bs = x_sample.shape[0]
    past = page_table.shape[1] * PAGE

    mod = _ada_mod(jnp.concatenate([c_prompt, c_sample], axis=0), w_ada, b_ada).reshape(depth, bp + bs, 6, d)
    tabs_p = _rope_tables(jnp.arange(seq, dtype=jnp.int32))
    tabs_s = _rope_tables(jnp.full((1,), past, jnp.int32))
    cache_kpe_t = jnp.swapaxes(cache_k_rope, 2, 3)

    yp = x_prompt
    ys = x_sample.reshape(1, bs, d)
    p_new = [[] for _ in range(5)]
    s_new = [[] for _ in range(5)]
    for li in range(depth):
        wts = _layer_weights(params, li)
        mp = [mod[li, :bp, i].reshape(bp, 1, d) for i in range(6)]
        ms = [mod[li, bp:, i].reshape(1, bs, d) for i in range(6)]

        z, xbc, dt, u, lat, kpe, q, k, v = _inproj(yp, mp[0], mp[1], wts, tabs_p, tm=512, with_kv=True)
        ssd_out, h_t = _ssd_prompt(xbc, z, dt, wts)
        mla_out = _attn_prompt(q, k, v, wts["g_mla"], tq=512)
        pool_out = _pool_prompt(u, wts["w_pool"], wts["pool_scale"])
        yp = _ffn(yp, ssd_out, mla_out, pool_out, (mp[2], mp[3], mp[4], mp[5]), wts, tm=256)
        for lst, val in zip(p_new, (lat, kpe, h_t.reshape(bp, NH, HD, SSD_N), xbc[:, seq - (CONV_K - 1):],
                                    u[:, seq - POOL_BUF:])):
            lst.append(val)

        z, xbc, dt, u, lat, kpe, q, kpe128 = _inproj(ys, ms[0], ms[1], wts, tabs_s, tm=bs, with_kv=False)
        z, xbc, dt, u, lat, kpe, q, kpe128 = (a[0] for a in (z, xbc, dt, u, lat, kpe, q, kpe128))
        conv_buf = state_conv[li]
        xs, bm, cm, xdt_t, da = _ssd_sample_pre(xbc, jnp.transpose(conv_buf, (1, 0, 2)), dt, wts)
        h_new, y_t = _ssd_sample_state(da[:, :8], state_ssm[li].reshape(bs, SSD_W, SSD_N), xdt_t, bm, cm)
        ssd_out = _ssd_sample_post(y_t.T, xs, z, wts)
        qabs, qpe, snew = _mla_sample_pre(q, lat, kpe128, wts)
        qabs = jnp.pad(qabs.reshape(bs, NH, LANE), ((0, 0), (0, HPAD - NH), (0, 0))).astype(bf16)
        qpe = jnp.pad(qpe.reshape(bs, NH, LANE)[:, :, :ROPE], ((0, 0), (0, HPAD - NH), (0, 0))).astype(bf16)
        snew_b = jnp.broadcast_to(jnp.pad(snew[:, :NH], ((0, 0), (0, HPAD - NH)))[:, :, None], (bs, HPAD, LANE))
        o_lat = _attn_sample(page_table, cache_kv_latent, cache_kpe_t, wts["w_kt_flat"], qabs, qpe, snew_b,
                             lat.reshape(bs, 1, KV_LORA), li=li)
        mla_out = _mla_sample_post(o_lat[:, :NH].reshape(bs, NH * KV_LORA), wts["w_v_bd"], wts["g_mla"])
        pool_buf = state_pool[li]
        pool_out = _pool_sample(u, jnp.transpose(pool_buf, (1, 0, 2)), wts["w_pool"], wts["pool_scale"])
        ys = _ffn(ys, ssd_out[None], mla_out[None], pool_out[None], (ms[2], ms[3], ms[4], ms[5]), wts, tm=bs)
        conv_new = jnp.concatenate([conv_buf[:, 1:], xbc[:, None, :]], axis=1)
        pool_new = jnp.concatenate([pool_buf[:, 1:], u[:, None, :]], axis=1)
        for lst, val in zip(s_new, (lat[:, None, :], kpe[:, None, :], h_new.reshape(bs, NH, HD, SSD_N), conv_new, pool_new)):
            lst.append(val)

    outs_p = [jnp.stack(vv, axis=0) for vv in p_new]
    outs_s = [jnp.stack(vv, axis=0) for vv in s_new]
    return (yp, ys.reshape(bs, 1, d), *outs_p, *outs_s)
```

```python
import functools
import math

import jax
import jax.numpy as jnp
import numpy as np
from jax import lax
from jax.experimental import pallas as pl
from jax.experimental.pallas import tpu as pltpu

f32 = jnp.float32
bf16 = jnp.bfloat16
HIGHEST = lax.Precision.HIGHEST

EPS = 1e-6
PAGE = 128
NH = 6
HD = 64
NOPE = 64
ROPE = 32
QK = NOPE + ROPE
SSD_W = NH * HD
SSD_G = 2
SSD_N = 128
CONV_K = 4
CONV_CH = SSD_W + 2 * SSD_G * SSD_N
Q_LORA = 256
KV_LORA = 128
POOL_W = 256
POOL_BUF = 15
ROPE_THETA = 10000.0
ATTN_SCALE = QK ** -0.5
PROMPT_Q_SCALE = ATTN_SCALE * math.log2(math.e)
LANE = 128
HPAD = 16
VMEM_LIMIT = 56 * 1024 * 1024

_C_Z, _C_XBC, _C_CQ, _C_CKV, _C_U, _C_KA, _C_KB, _C_DT, _C_END = 0, 384, 1280, 1536, 1664, 1920, 2048, 2176, 2304


def _cparams(sem):
    return pltpu.CompilerParams(dimension_semantics=sem, vmem_limit_bytes=VMEM_LIMIT)


def _silu(x):
    return x * jax.nn.sigmoid(x)


def _softplus(x):
    return jnp.maximum(x, 0.0) + jnp.log1p(jnp.exp(-jnp.abs(x)))


def _rms(x, g, n):
    ms = jnp.sum(x * x, axis=-1, keepdims=True) * (1.0 / n)
    return x * lax.rsqrt(ms + EPS) * g


def _dot(a, b):
    return jnp.dot(a, b, preferred_element_type=f32)


def _dot_nt(a, b):
    return lax.dot_general(a, b, (((1,), (1,)), ((), ())), preferred_element_type=f32)


def _dot_tn(a, b):
    return lax.dot_general(a, b, (((0,), (0,)), ((), ())), preferred_element_type=f32)


def _dot_exact(a, b):
    return jnp.dot(a, b, precision=HIGHEST, preferred_element_type=f32)


def _head_expand():
    r = lax.broadcasted_iota(jnp.int32, (LANE, SSD_W), 0)
    c = lax.broadcasted_iota(jnp.int32, (LANE, SSD_W), 1)
    return jnp.where(c // HD == r, 1.0, 0.0).astype(f32)


def _ada_kernel(c_ref, w_ref, b_ref, op_ref, os_ref):
    s = _silu(c_ref[...]).astype(bf16)
    r = _dot(s, w_ref[0].astype(bf16)) + b_ref[0]
    rp = op_ref.shape[2]
    op_ref[0, 0] = r[:rp]
    os_ref[0, 0] = r[rp:]


def _ada_mod(c_prompt, c_sample, w_ada, b_ada):
    depth, d, n6 = w_ada.shape
    nf = n6 // d
    bp, bs = c_prompt.shape[0], c_sample.shape[0]
    rp = -(-bp // 8) * 8
    c_all = jnp.concatenate([c_prompt, jnp.zeros((rp - bp, d), f32), c_sample], axis=0)
    return pl.pallas_call(
        _ada_kernel,
        grid=(depth, nf),
        in_specs=[pl.BlockSpec((rp + bs, d), lambda l, j: (0, 0)),
                  pl.BlockSpec((1, d, d), lambda l, j: (l, 0, j)),
                  pl.BlockSpec((1, 1, d), lambda l, j: (l, 0, j))],
        out_specs=[pl.BlockSpec((1, 1, rp, d), lambda l, j: (l, j, 0, 0)),
                   pl.BlockSpec((1, 1, bs, d), lambda l, j: (l, j, 0, 0))],
        out_shape=[jax.ShapeDtypeStruct((depth, nf, rp, d), f32), jax.ShapeDtypeStruct((depth, nf, bs, d), f32)],
        compiler_params=_cparams(("arbitrary", "arbitrary")),
        name="ada_mod",
    )(c_all, w_ada, b_ada.reshape(depth, 1, n6))


def _mod_rows(mod_ref, field, per_row, tm):
    if per_row:
        return mod_ref[0, field, pl.ds(pl.multiple_of(pl.program_id(1) * tm, tm), tm), :]
    return mod_ref[0, field, pl.ds(pl.program_id(0), 1), :]


def _inproj_kernel(*refs, with_kv, per_row):
    (x_ref, mod_ref, g1_ref, w_ref, gq_ref, wq_ref, cosq_ref, sinq_ref, gqk_ref, gkv_ref,
     cosk_ref, sink_ref) = refs[:12]
    if with_kv:
        wk_ref, gk_ref, wv_ref = refs[12:15]
        z_ref, xbc_ref, dt_ref, u_ref, lat_ref, kpe_ref, q_ref, k_ref, v_ref = refs[15:]
    else:
        z_ref, xbc_ref, dt_ref, u_ref, lat_ref, kpe_ref, q_ref, kpe128_ref = refs[12:]
    tm, d = x_ref.shape[1:]
    shift, scale = (_mod_rows(mod_ref, f, per_row, tm) for f in (0, 1))
    gqk = gqk_ref[...]
    q_scale = PROMPT_Q_SCALE if with_kv else 1.0
    nsplit = 2 if tm % 512 == 0 else 1
    hm = tm // nsplit

    def rows_of(a, r):
        return a if a.shape[0] == 1 else a[r:r + hm]

    def project(r):
        h = _rms(x_ref[0, r:r + hm, :], g1_ref[...], d) * (1.0 + rows_of(scale, r)) + rows_of(shift, r)
        return _dot(h.astype(bf16), w_ref[0])

    def finish(r, proj):
        rs = slice(r, r + hm)
        z_ref[0, rs, :] = proj[:, _C_Z:_C_XBC]
        xbc_ref[0, rs, :] = proj[:, _C_XBC:_C_CQ]
        u_ref[0, rs, :] = proj[:, _C_U:_C_KA]
        dt_ref[0, rs, :] = proj[:, _C_DT:_C_END]
        cqn = _rms(proj[:, _C_CQ:_C_CKV], gq_ref[...], Q_LORA).astype(bf16)
        qq = _dot(cqn, wq_ref[0])
        cosq = rows_of(cosq_ref[...], r)
        sinq = rows_of(sinq_ref[...], r)
        for hh in range(NH):
            qh = qq[:, LANE * hh:LANE * (hh + 1)] * cosq + qq[:, NH * LANE + LANE * hh:NH * LANE + LANE * (hh + 1)] * sinq
            q_ref[0, rs, LANE * hh:LANE * (hh + 1)] = (_rms(qh, gqk, QK) * q_scale).astype(q_ref.dtype)
        lat = _rms(proj[:, _C_CKV:_C_U], gkv_ref[...], KV_LORA)
        lat_ref[0, rs, :] = lat
        kper = proj[:, _C_KA:_C_KB] * rows_of(cosk_ref[...], r) + proj[:, _C_KB:_C_DT] * rows_of(sink_ref[...], r)
        kpe_ref[0, rs, :] = kper[:, :ROPE]
        if with_kv:
            latb = lat.astype(bf16)
            kn = _dot(latb, wk_ref[...])
            kp = pltpu.roll(kper, NOPE, 1)
            gk = gk_ref[...]
            for hh in range(NH):
                kh = kn[:, LANE * hh:LANE * (hh + 1)] + kp
                k_ref[0, rs, LANE * hh:LANE * (hh + 1)] = _rms(kh, gk, QK).astype(bf16)
            v_ref[0, :, rs] = _dot_nt(wv_ref[...], latb).astype(bf16)
        else:
            kpe128_ref[0, rs, :] = kper

    nxt = project(0)
    for i in range(nsplit):
        proj = nxt
        if i + 1 < nsplit:
            nxt = project((i + 1) * hm)
        finish(i * hm, proj)


def _inproj(x, mod, wts, tabs, *, li, tm, with_kv, per_row):
    g_, t_, d = x.shape
    nt = t_ // tm
    cosq, sinq, cosk, sink = tabs
    tab_rows = cosq.shape[0] != 1
    tab_spec = pl.BlockSpec((tm if tab_rows else 1, LANE), (lambda g, t: (t, 0)) if tab_rows else (lambda g, t: (0, 0)))

    def full(a):
        return pl.BlockSpec(a.shape, lambda g, t: (0,) * a.ndim)

    def layer(a):
        return pl.BlockSpec((1,) + a.shape[1:], lambda g, t: (li,) + (0,) * (a.ndim - 1))

    def row(c):
        return pl.BlockSpec((1, tm, c), lambda g, t: (g, t, 0))

    ins = [x, mod, wts["g1"], wts["w_in"], wts["g_q"], wts["w_q"], cosq, sinq, wts["g_qk_q"], wts["g_kv"], cosk, sink]
    specs = [row(d), layer(mod), full(wts["g1"]), layer(wts["w_in"]), full(wts["g_q"]), layer(wts["w_q"]), tab_spec,
             tab_spec, full(wts["g_qk_q"]), full(wts["g_kv"]), tab_spec, tab_spec]
    widths = [SSD_W, CONV_CH, LANE, POOL_W, KV_LORA, ROPE, NH * LANE]
    dtypes = [f32, f32, f32, f32, f32, f32, bf16 if with_kv else f32]
    if with_kv:
        ins += [wts["w_k"], wts["g_qk_k"], wts["w_v"]]
        specs += [full(wts["w_k"]), full(wts["g_qk_k"]), full(wts["w_v"])]
        widths += [NH * LANE, SSD_W]
        dtypes += [bf16, bf16]
    else:
        widths += [LANE]
        dtypes += [f32]
    out_specs = [row(c) for c in widths]
    out_shape = [jax.ShapeDtypeStruct((g_, t_, c), dt) for c, dt in zip(widths, dtypes)]
    if with_kv:
        out_specs[-1] = pl.BlockSpec((1, SSD_W, tm), lambda g, t: (g, 0, t))
        out_shape[-1] = jax.ShapeDtypeStruct((g_, SSD_W, t_), bf16)
    return pl.pallas_call(
        functools.partial(_inproj_kernel, with_kv=with_kv, per_row=per_row),
        grid=(g_, nt),
        in_specs=specs,
        out_specs=out_specs,
        out_shape=out_shape,
        compiler_params=_cparams(("arbitrary", "arbitrary")),
        name="inproj_kv" if with_kv else "inproj",
    )(*ins)


def _ssd_prompt_kernel(xbc_ref, z_ref, dt_ref, cw_ref, cb_ref, dtb_ref, a_ref, dsk_ref, gn_ref,
                       y_ref, st_ref, ext_ref):
    c = pl.program_id(1)
    t_ = xbc_ref.shape[1]

    @pl.when(c == 0)
    def _():
        ext_ref[0:8, :] = jnp.zeros((8, CONV_CH), f32)
        st_ref[...] = jnp.zeros(st_ref.shape, f32)

    x = xbc_ref[0]
    ext_ref[8:8 + t_, :] = x
    cw = cw_ref[...]
    conv = cb_ref[...] + cw[3:4] * x
    for k in range(CONV_K - 1):
        conv = conv + cw[k:k + 1] * ext_ref[5 + k:5 + k + t_, :]
    ext_ref[0:8, :] = x[t_ - 8:, :]
    act = _silu(conv)
    xs = act[:, :SSD_W]
    bm = [act[:, SSD_W + SSD_N * g:SSD_W + SSD_N * (g + 1)].astype(bf16) for g in range(SSD_G)]
    cm = [act[:, SSD_W + SSD_N * (SSD_G + g):SSD_W + SSD_N * (SSD_G + g + 1)].astype(bf16) for g in range(SSD_G)]

    dt = _softplus(dt_ref[0] + dtb_ref[...])
    a = dt * a_ref[...]
    ri = lax.broadcasted_iota(jnp.int32, (t_, t_), 0)
    ci = lax.broadcasted_iota(jnp.int32, (t_, t_), 1)
    tril = ri >= ci
    acum = _dot_exact(jnp.where(tril, 1.0, 0.0).astype(f32), a)
    acum_t = acum.T
    expand = _head_expand()
    dt_l = _dot_exact(dt, expand)
    acum_l = _dot_exact(acum, expand)
    last_l = acum_l[t_ - 1:t_, :]
    e_l = jnp.exp(acum_l)
    decay_l = jnp.exp(last_l - acum_l)
    xdt = xs * dt_l
    xdt_w = xdt * decay_l

    cb = [_dot_nt(cm[g], bm[g]) for g in range(SSD_G)]
    lane = lax.broadcasted_iota(jnp.int32, (t_, LANE), 1)
    rowi = lax.broadcasted_iota(jnp.int32, (LANE, SSD_N), 0)
    first = lane < HD
    rfirst = rowi < HD
    y_cols = []
    for j in range(NH // 2):
        sl = slice(LANE * j, LANE * (j + 1))
        h0, h1 = 2 * j, 2 * j + 1
        g0, g1 = h0 // (NH // SSD_G), h1 // (NH // SSD_G)
        xj = xdt[:, sl].astype(bf16)
        yd = []
        for hh, gg in ((h0, g0), (h1, g1)):
            seg = jnp.where(tril, jnp.exp(jnp.minimum(acum[:, hh:hh + 1] - acum_t[hh:hh + 1, :], 0.0)), 0.0)
            yd.append(_dot((cb[gg] * seg).astype(bf16), xj))
        y_diag = jnp.where(first, yd[0], yd[1])
        hp = st_ref[0, sl, :]
        hpb = hp.astype(bf16)
        xw = xdt_w[:, sl].astype(bf16)
        if g0 == g1:
            y_off = _dot_nt(cm[g0], hpb)
            s_new = _dot_tn(xw, bm[g0])
        else:
            y_off = jnp.where(first, _dot_nt(cm[g0], hpb), _dot_nt(cm[g1], hpb))
            s_new = jnp.where(rfirst, _dot_tn(xw, bm[g0]), _dot_tn(xw, bm[g1]))
        tot = jnp.where(rfirst, jnp.exp(acum[t_ - 1:t_, h0:h0 + 1]), jnp.exp(acum[t_ - 1:t_, h1:h1 + 1]))
        st_ref[0, sl, :] = tot * hp + s_new
        y_cols.append(y_diag + y_off * e_l[:, sl])
    y = jnp.concatenate(y_cols, axis=-1) + dsk_ref[...] * xs
    v = y * _silu(z_ref[0])
    gl = lax.broadcasted_iota(jnp.int32, (t_, SSD_W), 1) < SSD_W // SSD_G
    v2 = v * v
    ss0 = jnp.sum(jnp.where(gl, v2, 0.0), axis=-1, keepdims=True)
    ss1 = jnp.sum(jnp.where(gl, 0.0, v2), axis=-1, keepdims=True)
    gw = SSD_W // SSD_G
    rinv = jnp.where(gl, lax.rsqrt(ss0 * (1.0 / gw) + EPS), lax.rsqrt(ss1 * (1.0 / gw) + EPS))
    y_ref[0] = v * rinv * gn_ref[...]


def _ssd_prompt(xbc, z, dt, wts):
    b_, l_, _ = xbc.shape
    t_ = 128
    nc = l_ // t_

    def full(a):
        return pl.BlockSpec(a.shape, lambda b, c: (0,) * a.ndim)

    def row(w):
        return pl.BlockSpec((1, t_, w), lambda b, c: (b, c, 0))

    small = [wts["conv_w"], wts["conv_b"], wts["dt_bias"], wts["a_neg"], wts["d_skip_l"], wts["g_ssd"]]
    return pl.pallas_call(
        _ssd_prompt_kernel,
        grid=(b_, nc),
        in_specs=[row(CONV_CH), row(SSD_W), row(LANE)] + [full(a) for a in small],
        out_specs=[row(SSD_W), pl.BlockSpec((1, SSD_W, SSD_N), lambda b, c: (b, 0, 0))],
        out_shape=[jax.ShapeDtypeStruct((b_, l_, SSD_W), f32), jax.ShapeDtypeStruct((b_, SSD_W, SSD_N), f32)],
        scratch_shapes=[pltpu.VMEM((8 + t_, CONV_CH), f32)],
        compiler_params=_cparams(("arbitrary", "arbitrary")),
        name="ssd_prompt",
    )(xbc, z, dt, *small)


def _attn_prompt_kernel(q_ref, k_ref, vt_ref, g_ref, o_ref, m_ref, l_ref, acc_ref, *, tq):
    qi = pl.program_id(1)
    m_ref[...] = jnp.full(m_ref.shape, -1e30, f32)
    l_ref[...] = jnp.zeros(l_ref.shape, f32)
    acc_ref[...] = jnp.zeros(acc_ref.shape, f32)

    def tile(j, diagonal):
        off = pl.multiple_of(j * tq, tq)
        kblk = k_ref[0, pl.ds(off, tq), :]
        vblk = vt_ref[0, :, pl.ds(off, tq)]
        if diagonal:
            masked = lax.broadcasted_iota(jnp.int32, (tq, tq), 0) > lax.broadcasted_iota(jnp.int32, (tq, tq), 1)

        def qk(hh):
            return _dot_nt(kblk[:, LANE * hh:LANE * (hh + 1)], q_ref[0, :, LANE * hh:LANE * (hh + 1)])

        s_next = qk(0)
        for hh in range(NH):
            s = s_next
            if hh + 1 < NH:
                s_next = qk(hh + 1)
            if diagonal:
                s = jnp.where(masked, -1e30, s)
            m_old = m_ref[hh]
            m_new = jnp.maximum(m_old, jnp.max(s, axis=0, keepdims=True))
            alpha = jnp.exp2(m_old - m_new)
            pe = jnp.exp2(s - m_new)
            l_ref[hh] = alpha * l_ref[hh] + jnp.sum(pe.reshape(tq // 8, 8, tq), axis=0)
            m_ref[hh] = m_new
            rows = slice(HD * hh, HD * (hh + 1))
            acc_ref[rows, :] = acc_ref[rows, :] * alpha + _dot(vblk[rows, :], pe.astype(bf16))

    def body(j, carry):
        tile(j, False)
        return carry

    lax.fori_loop(0, qi, body, 0)
    tile(qi, True)
    parts = []
    for hh in range(NH):
        l = jnp.sum(l_ref[hh], axis=0, keepdims=True)
        parts.append(acc_ref[HD * hh:HD * (hh + 1), :] * (1.0 / l))
    o_t = jnp.concatenate(parts, axis=0)
    ms = jnp.sum(o_t * o_t, axis=0, keepdims=True) * (1.0 / SSD_W)
    o_ref[0] = (o_t * lax.rsqrt(ms + EPS)).T * g_ref[...]


def _attn_prompt(q, k, v_t, g_out, *, tq):
    b_, l_, _ = q.shape
    return pl.pallas_call(
        functools.partial(_attn_prompt_kernel, tq=tq),
        grid=(b_, l_ // tq),
        in_specs=[pl.BlockSpec((1, tq, NH * LANE), lambda b, i: (b, i, 0)),
                  pl.BlockSpec((1, l_, NH * LANE), lambda b, i: (b, 0, 0)),
                  pl.BlockSpec((1, SSD_W, l_), lambda b, i: (b, 0, 0)),
                  pl.BlockSpec(g_out.shape, lambda b, i: (0, 0))],
        out_specs=pl.BlockSpec((1, tq, SSD_W), lambda b, i: (b, i, 0)),
        out_shape=jax.ShapeDtypeStruct((b_, l_, SSD_W), f32),
        scratch_shapes=[pltpu.VMEM((NH, 1, tq), f32), pltpu.VMEM((NH, 8, tq), f32), pltpu.VMEM((SSD_W, tq), f32)],
        compiler_params=_cparams(("arbitrary", "arbitrary")),
        name="attn_prompt",
    )(q, k, v_t, g_out)


def _pool_select(lane, a, b, c, d):
    return jnp.where(lane < 64, a, jnp.where(lane < 128, b, jnp.where(lane < 192, c, d)))


def _pool_prompt_kernel(u_ref, w_ref, sc_ref, o_ref, e1, e2, e4, e8, *, rt):
    t_ = u_ref.shape[1]
    hist = 16
    for e in (e1, e2, e4, e8):
        e[0:hist, :] = jnp.zeros((hist, POOL_W), f32)
    e1[hist:hist + t_, :] = u_ref[0]
    lane = lax.broadcasted_iota(jnp.int32, (rt, POOL_W), 1)
    win = _pool_select(lane, 2, 4, 8, 16)
    for i in range(t_ // rt):
        r0 = hist + i * rt
        a = e1[r0:r0 + rt, :]
        s2 = a + e1[r0 - 1:r0 - 1 + rt, :]
        e2[r0:r0 + rt, :] = s2
        s4 = s2 + e2[r0 - 2:r0 - 2 + rt, :]
        e4[r0:r0 + rt, :] = s4
        s8 = s4 + e4[r0 - 4:r0 - 4 + rt, :]
        e8[r0:r0 + rt, :] = s8
        s16 = s8 + e8[r0 - 8:r0 - 8 + rt, :]
        pos = lax.broadcasted_iota(jnp.int32, (rt, POOL_W), 0) + i * rt
        cnt = jnp.minimum(pos + 1, win).astype(f32)
        pooled = _pool_select(lane, s2, s4, s8, s16) / cnt - a
        o_ref[0, i * rt:(i + 1) * rt, :] = _dot(pooled.astype(bf16), w_ref[...]) * sc_ref[...]


def _pool_prompt(u, w_bd, scale):
    b_, l_, _ = u.shape
    return pl.pallas_call(
        functools.partial(_pool_prompt_kernel, rt=256),
        grid=(b_,),
        in_specs=[pl.BlockSpec((1, l_, POOL_W), lambda b: (b, 0, 0)),
                  pl.BlockSpec(w_bd.shape, lambda b: (0, 0)),
                  pl.BlockSpec(scale.shape, lambda b: (0, 0))],
        out_specs=pl.BlockSpec((1, l_, POOL_W), lambda b: (b, 0, 0)),
        out_shape=jax.ShapeDtypeStruct((b_, l_, POOL_W), f32),
        scratch_shapes=[pltpu.VMEM((16 + l_, POOL_W), f32)] * 4,
        compiler_params=_cparams(("arbitrary",)),
        name="pool_prompt",
    )(u, w_bd, scale)


def _pool_sample_kernel(u_ref, buf_ref, w_ref, sc_ref, o_ref):
    u = u_ref[...]
    run = u
    sums = {}
    for i in range(1, 16):
        run = run + buf_ref[POOL_BUF - i]
        if i + 1 in (2, 4, 8, 16):
            sums[i + 1] = run * (1.0 / (i + 1))
    lane = lax.broadcasted_iota(jnp.int32, u.shape, 1)
    pooled = _pool_select(lane, sums[2], sums[4], sums[8], sums[16]) - u
    o_ref[...] = _dot(pooled.astype(bf16), w_ref[...]) * sc_ref[...]


def _pool_sample(u, buf_t, w_bd, scale):
    return pl.pallas_call(
        _pool_sample_kernel,
        out_shape=jax.ShapeDtypeStruct(u.shape, f32),
        compiler_params=pltpu.CompilerParams(vmem_limit_bytes=VMEM_LIMIT),
        name="pool_sample",
    )(u, buf_t, w_bd, scale)


def _ffn_kernel(x_ref, ssd_ref, mla_ref, pool_ref, mod_ref, gn_ref, wo_ref, wg_ref, wu_ref, wd_ref, o_ref,
                *, nch, per_row):
    tm, d = x_ref.shape[1:]
    gate1, shift2, scale2, gate2 = (_mod_rows(mod_ref, f, per_row, tm) for f in (2, 3, 4, 5))
    mixed = jnp.concatenate([ssd_ref[0].astype(bf16), mla_ref[0].astype(bf16), pool_ref[0].astype(bf16)], axis=-1)
    x1 = x_ref[0] + gate1 * _dot(mixed, wo_ref[0])
    h2 = (_rms(x1, gn_ref[...], d) * (1.0 + scale2) + shift2).astype(bf16)
    hc = wg_ref.shape[2] // nch

    def gate_up(c):
        return _dot(h2, wg_ref[0, :, hc * c:hc * (c + 1)]), _dot(h2, wu_ref[0, :, hc * c:hc * (c + 1)])

    acc = jnp.zeros(x1.shape, f32)
    nxt = gate_up(0)
    for c in range(nch):
        gate, up = nxt
        if c + 1 < nch:
            nxt = gate_up(c + 1)
        acc = acc + _dot((_silu(gate) * up).astype(bf16), wd_ref[0, hc * c:hc * (c + 1), :])
    o_ref[0] = x1 + gate2 * acc


def _ffn(x, ssd, mla, pool, mod, wts, *, li, tm, per_row):
    g_, t_, d = x.shape

    def layer(a):
        return pl.BlockSpec((1,) + a.shape[1:], lambda g, t: (li,) + (0,) * (a.ndim - 1), pipeline_mode=pl.Buffered(1))

    def row(c):
        return pl.BlockSpec((1, tm, c), lambda g, t: (g, t, 0))

    ws = [wts["w_out"], wts["w_gate"], wts["w_up"], wts["w_down"]]
    return pl.pallas_call(
        functools.partial(_ffn_kernel, nch=11, per_row=per_row),
        grid=(g_, t_ // tm),
        in_specs=[row(d), row(SSD_W), row(SSD_W), row(POOL_W), layer(mod),
                  pl.BlockSpec(wts["g2"].shape, lambda g, t: (0, 0))] + [layer(a) for a in ws],
        out_specs=row(d),
        out_shape=jax.ShapeDtypeStruct((g_, t_, d), f32),
        compiler_params=_cparams(("arbitrary", "arbitrary")),
        name="outproj_ffn",
    )(x, ssd, mla, pool, mod, wts["g2"], *ws)


def _ssd_sample_pre_kernel(xbc_ref, buf_ref, dt_ref, cw_ref, cb_ref, dtb_ref, a_ref,
                           xs_ref, b_ref, c_ref, xdt_t_ref, da_ref):
    cw = cw_ref[...]
    conv = cb_ref[...] + cw[3:4] * xbc_ref[...]
    for k in range(CONV_K - 1):
        conv = conv + cw[k:k + 1] * buf_ref[k]
    act = _silu(conv)
    xs = act[:, :SSD_W]
    xs_ref[...] = xs
    b_ref[...] = act[:, SSD_W:SSD_W + SSD_G * SSD_N]
    c_ref[...] = act[:, SSD_W + SSD_G * SSD_N:]
    dt = _softplus(dt_ref[...] + dtb_ref[...])
    da_ref[...] = jnp.exp(dt * a_ref[...])
    xdt = xs * _dot_exact(dt, _head_expand())
    for j in range(SSD_W // LANE):
        xdt_t_ref[LANE * j:LANE * (j + 1), :] = xdt[:, LANE * j:LANE * (j + 1)].T


def _ssd_sample_pre(xbc, buf_t, dt, wts):
    b_ = xbc.shape[0]
    shapes = [(b_, SSD_W), (b_, SSD_G * SSD_N), (b_, SSD_G * SSD_N), (SSD_W, b_), (b_, LANE)]
    return pl.pallas_call(
        _ssd_sample_pre_kernel,
        out_shape=[jax.ShapeDtypeStruct(s, f32) for s in shapes],
        compiler_params=pltpu.CompilerParams(vmem_limit_bytes=VMEM_LIMIT),
        name="ssd_sample_pre",
    )(xbc, buf_t, dt, wts["conv_w"], wts["conv_b"], wts["dt_bias"], wts["a_neg"])


def _ssd_sample_state_kernel(da_ref, st_ref, xdt_t_ref, b_ref, c_ref, new_ref, y_t_ref, *, tb):
    i = pl.program_id(0)

    @pl.when(i == 0)
    def _():
        y_t_ref[...] = jnp.zeros(y_t_ref.shape, f32)

    nb = xdt_t_ref.shape[1]
    lane = lax.broadcasted_iota(jnp.int32, (HD, nb), 1)
    for bl in range(tb):
        b = i * tb + bl
        sel = lane == b
        for hh in range(NH):
            g = hh // (NH // SSD_G)
            rs = slice(HD * hh, HD * (hh + 1))
            col = jnp.sum(jnp.where(sel, xdt_t_ref[rs, :], 0.0), axis=1, keepdims=True)
            brow = b_ref[bl:bl + 1, SSD_N * g:SSD_N * (g + 1)]
            crow = c_ref[bl:bl + 1, SSD_N * g:SSD_N * (g + 1)]
            new = da_ref[b, hh] * st_ref[bl, rs, :] + col * brow
            new_ref[bl, rs, :] = new
            ycol = jnp.sum(new * crow, axis=1, keepdims=True)
            y_t_ref[rs, :] = jnp.where(sel, ycol, y_t_ref[rs, :])


def _ssd_sample_state(da, state_all, xdt_t, bm, cm, *, li, tb=8):
    b_ = xdt_t.shape[1]
    nblk = b_ // tb
    return pl.pallas_call(
        functools.partial(_ssd_sample_state_kernel, tb=tb),
        grid=(nblk,),
        in_specs=[pl.BlockSpec(memory_space=pltpu.SMEM),
                  pl.BlockSpec((tb, SSD_W, SSD_N), lambda i: (li * nblk + i, 0, 0)),
                  pl.BlockSpec(xdt_t.shape, lambda i: (0, 0)),
                  pl.BlockSpec((tb, SSD_G * SSD_N), lambda i: (i, 0)),
                  pl.BlockSpec((tb, SSD_G * SSD_N), lambda i: (i, 0))],
        out_specs=[pl.BlockSpec((tb, SSD_W, SSD_N), lambda i: (i, 0, 0)),
                   pl.BlockSpec((SSD_W, b_), lambda i: (0, 0))],
        out_shape=[jax.ShapeDtypeStruct((b_, SSD_W, SSD_N), f32), jax.ShapeDtypeStruct((SSD_W, b_), f32)],
        compiler_params=_cparams(("arbitrary",)),
        name="ssd_sample_state",
    )(da, state_all, xdt_t, bm, cm)


def _ssd_sample_post_kernel(y_ref, xs_ref, z_ref, dsk_ref, gn_ref, o_ref):
    y = y_ref[...] + dsk_ref[...] * xs_ref[...]
    v = y * _silu(z_ref[...])
    gl = lax.broadcasted_iota(jnp.int32, v.shape, 1) < SSD_W // SSD_G
    v2 = v * v
    gw = SSD_W // SSD_G
    ss0 = jnp.sum(jnp.where(gl, v2, 0.0), axis=-1, keepdims=True)
    ss1 = jnp.sum(jnp.where(gl, 0.0, v2), axis=-1, keepdims=True)
    rinv = jnp.where(gl, lax.rsqrt(ss0 * (1.0 / gw) + EPS), lax.rsqrt(ss1 * (1.0 / gw) + EPS))
    o_ref[...] = v * rinv * gn_ref[...]


def _ssd_sample_post(y, xs, z, wts):
    return pl.pallas_call(
        _ssd_sample_post_kernel,
        out_shape=jax.ShapeDtypeStruct(y.shape, f32),
        name="ssd_sample_post",
    )(y, xs, z, wts["d_skip_l"], wts["g_ssd"])


def _mla_sample_pre_kernel(q_ref, lat_ref, kpe_ref, wk_ref, wkt_ref, gk_ref, qabs_ref, qpe_ref, snew_ref):
    kn = _dot(lat_ref[...].astype(bf16), wk_ref[...])
    kp = pltpu.roll(kpe_ref[...], NOPE, 1)
    gk = gk_ref[...]
    lane = lax.broadcasted_iota(jnp.int32, snew_ref.shape, 1)
    snew = jnp.zeros(snew_ref.shape, f32)
    for hh in range(NH):
        sl = slice(LANE * hh, LANE * (hh + 1))
        qh = q_ref[:, sl]
        knew = _rms(kn[:, sl] + kp, gk, QK)
        snew = jnp.where(lane == hh, jnp.sum(qh * knew, axis=-1, keepdims=True) * ATTN_SCALE, snew)
        qg = qh * gk
        qabs_ref[:, sl] = _dot_exact(qg, wkt_ref[hh])
        qpe_ref[:, sl] = pltpu.roll(qg, NOPE, 1)
    snew_ref[...] = snew


def _mla_sample_pre(q, lat, kpe128, wts):
    b_ = q.shape[0]
    shapes = [(b_, NH * LANE), (b_, NH * LANE), (b_, LANE)]
    return pl.pallas_call(
        _mla_sample_pre_kernel,
        out_shape=[jax.ShapeDtypeStruct(s, f32) for s in shapes],
        name="mla_sample_pre",
    )(q, lat, kpe128, wts["w_k"], wts["w_kt"], wts["g_qk_k"])


def _attn_sample_kernel(pt_ref, lat_hbm, kpe_hbm, wkt_ref, qabs_ref, qpe_ref, snew_ref, latnew_ref, o_ref,
                        lat_buf, kpe_buf, sem, wext, latb_ref, s_ref, *, li, pages, nchunk):
    b = pl.program_id(0)
    nb = pl.num_programs(0)
    rc = pages * PAGE
    slot = b % 2

    def aligned(idx, size):
        return idx * size if isinstance(idx, int) else pl.multiple_of(idx * size, size)

    def copies(bb, c, i, sl):
        page_idx = c * pages + i
        page = pt_ref[bb, page_idx]
        off = aligned(page_idx, PAGE)
        return (pltpu.make_async_copy(lat_hbm.at[li, page], lat_buf.at[sl, pl.ds(off, PAGE), :], sem.at[0, sl, c]),
                pltpu.make_async_copy(kpe_hbm.at[li, page], kpe_buf.at[sl, :, pl.ds(off, PAGE)], sem.at[1, sl, c]))

    def start_chunk(bb, c, sl):
        for i in range(pages):
            for cp in copies(bb, c, i, sl):
                cp.start()

    @pl.when(b == 0)
    def _():
        def issue(c, carry):
            start_chunk(0, c, 0)
            start_chunk(jnp.minimum(1, nb - 1), c, 1)
            return carry
        lax.fori_loop(0, nchunk, issue, 0)
        wext[0:NH * HD, :] = wkt_ref[...]

    wext[NH * HD:NH * HD + HPAD, :] = qabs_ref[0]
    qpe = qpe_ref[0]
    rowi = lax.broadcasted_iota(jnp.int32, (HPAD, rc), 0)

    def scores(c, carry):
        for i in range(pages):
            for cp in copies(b, c, i, slot):
                cp.wait()
        off = aligned(c, rc)
        latb = lat_buf[slot, pl.ds(off, rc), :].astype(bf16)
        latb_ref[pl.ds(off, rc), :] = latb
        kpe = kpe_buf[slot, :, pl.ds(off, rc)]
        a = _dot_nt(wext[...], latb)
        nsq = jnp.zeros((HPAD, rc), f32)
        for hh in range(NH):
            kh = a[HD * hh:HD * (hh + 1), :]
            nsq = jnp.where(rowi == hh, jnp.sum(kh * kh, axis=0, keepdims=True), nsq)
        s_pe = _dot(qpe, kpe.astype(bf16))
        ksq = jnp.sum(kpe * kpe, axis=0, keepdims=True)
        s_ref[:, pl.ds(off, rc)] = (a[NH * HD:, :] + s_pe) * lax.rsqrt((nsq + ksq) * (1.0 / QK) + EPS) * ATTN_SCALE
        return carry

    lax.fori_loop(0, nchunk, scores, 0)

    b_ahead = jnp.minimum(b + 2, nb - 1)
    for c in range(nchunk):
        start_chunk(b_ahead, c, slot)

    s = s_ref[...]
    s_new = snew_ref[0][:, 0:1]
    m = jnp.maximum(jnp.max(s, axis=-1, keepdims=True), s_new)
    pe = jnp.exp(s - m)
    p_new = jnp.exp(s_new - m)
    l = jnp.sum(pe, axis=-1, keepdims=True) + p_new
    acc = _dot(pe.astype(bf16), latb_ref[...]) + p_new * latnew_ref[0]
    o_ref[0] = acc / l

    @pl.when(b == nb - 1)
    def _():
        def drain(c, carry):
            for sl in range(2):
                for i in range(pages):
                    for cp in copies(b, c, i, sl):
                        cp.wait()
            return carry
        lax.fori_loop(0, nchunk, drain, 0)


def _attn_sample(page_table, cache_lat, cache_kpe_t, wkt, qabs, qpe, snew, lat_new, *, li, pages=64):
    b_, npages = page_table.shape
    nchunk = npages // pages
    assert npages % pages == 0
    seq = npages * PAGE
    grid_spec = pltpu.PrefetchScalarGridSpec(
        num_scalar_prefetch=1,
        grid=(b_,),
        in_specs=[pl.BlockSpec(memory_space=pl.ANY),
                  pl.BlockSpec(memory_space=pl.ANY),
                  pl.BlockSpec(wkt.shape, lambda b, pt: (0, 0)),
                  pl.BlockSpec((1, HPAD, KV_LORA), lambda b, pt: (b, 0, 0)),
                  pl.BlockSpec((1, HPAD, ROPE), lambda b, pt: (b, 0, 0)),
                  pl.BlockSpec((1, HPAD, LANE), lambda b, pt: (b, 0, 0)),
                  pl.BlockSpec((1, 1, KV_LORA), lambda b, pt: (b, 0, 0))],
        out_specs=pl.BlockSpec((1, HPAD, KV_LORA), lambda b, pt: (b, 0, 0)),
        scratch_shapes=[pltpu.VMEM((2, seq, KV_LORA), f32),
                        pltpu.VMEM((2, ROPE, seq), f32),
                        pltpu.SemaphoreType.DMA((2, 2, nchunk)),
                        pltpu.VMEM((NH * HD + HPAD, KV_LORA), bf16),
                        pltpu.VMEM((seq, KV_LORA), bf16),
                        pltpu.VMEM((HPAD, seq), f32)],
    )
    return pl.pallas_call(
        functools.partial(_attn_sample_kernel, li=li, pages=pages, nchunk=nchunk),
        grid_spec=grid_spec,
        out_shape=jax.ShapeDtypeStruct((b_, HPAD, KV_LORA), f32),
        compiler_params=_cparams(("arbitrary",)),
        name="attn_sample",
    )(page_table, cache_lat, cache_kpe_t, wkt, qabs, qpe, snew, lat_new)


def _mla_sample_post_kernel(o_ref, wv_ref, g_ref, out_ref):
    out_ref[...] = _rms(_dot(o_ref[...].astype(bf16), wv_ref[...]), g_ref[...], SSD_W)


def _mla_sample_post(o_lat, w_v_bd, g_out):
    return pl.pallas_call(
        _mla_sample_post_kernel,
        out_shape=jax.ShapeDtypeStruct((o_lat.shape[0], SSD_W), f32),
        name="mla_sample_post",
    )(o_lat, w_v_bd, g_out)


def _pad_heads(w, lo, hi):
    pad = [(0, 0)] * (w.ndim - 1) + [(lo, LANE - hi)]
    w = jnp.pad(w, pad)
    return w.reshape(w.shape[:-2] + (NH * LANE,))


def _rot_cols(w):
    half = ROPE // 2
    return jnp.concatenate([-w[..., half:], w[..., :half]], axis=-1)


def _pad_lanes(a, lo=0):
    return jnp.pad(a, [(0, 0)] * (a.ndim - 1) + [(lo, LANE - lo - a.shape[-1])])


def _stacked_weights(p):
    offs = np.cumsum([SSD_W, CONV_CH, NH, Q_LORA, KV_LORA, ROPE, POOL_W])
    wz, wxbc, wdt, wcq, wckv, wkpe, wu = jnp.split(p["w_in"], offs[:-1].tolist(), axis=2)
    w_in = jnp.concatenate([wz, wxbc, wcq, wckv, wu, _pad_lanes(wkpe), _pad_lanes(_rot_cols(wkpe)), _pad_lanes(wdt)],
                           axis=2).astype(bf16)
    wq = p["w_q_up"]
    w_q = jnp.concatenate([_pad_heads(wq, 0, QK), _pad_heads(_rot_cols(wq[..., NOPE:]), NOPE, QK)], axis=2).astype(bf16)
    return dict(w_in=w_in, w_q=w_q, w_out=p["w_out"].astype(bf16), w_gate=p["w_gate"].astype(bf16),
                w_up=p["w_up"].astype(bf16), w_down=p["w_down"].astype(bf16))


def _layer_weights(p, li):
    d = p["w_in"].shape[1]
    wk = p["w_k_up"][li]
    wk_pad = _pad_heads(wk, 0, NOPE)
    wk_t = jnp.transpose(wk, (1, 2, 0))
    wv = p["w_v_up"][li]
    w_v_bd = jnp.zeros((NH, KV_LORA, NH, HD), f32)
    w_v_bd = w_v_bd.at[jnp.arange(NH), :, jnp.arange(NH), :].set(jnp.transpose(wv, (1, 0, 2)))
    wp = p["w_pool"][li]
    ng = wp.shape[0]
    w_pool_bd = jnp.zeros((ng, HD, ng, HD), f32).at[jnp.arange(ng), :, jnp.arange(ng), :].set(wp)
    return dict(
        g1=p["g_norm1"][li].reshape(1, d),
        g_q=p["g_q_lora"][li].reshape(1, Q_LORA),
        g_qk_q=_pad_lanes(p["g_qk_q"][li].reshape(1, QK)), g_qk_k=_pad_lanes(p["g_qk_k"][li].reshape(1, QK)),
        g_kv=p["g_kv_lora"][li].reshape(1, KV_LORA),
        w_k=wk_pad.astype(bf16), w_v=wv.reshape(KV_LORA, NH * HD).T.astype(bf16),
        w_kt=jnp.pad(wk_t, ((0, 0), (0, LANE - NOPE), (0, 0))),
        w_kt_flat=wk_t.reshape(NH * NOPE, KV_LORA).astype(bf16),
        w_v_bd=w_v_bd.reshape(NH * KV_LORA, NH * HD).astype(bf16),
        g_mla=p["g_mla_out"][li].reshape(1, SSD_W),
        conv_w=p["conv_w"][li], conv_b=p["conv_b"][li].reshape(1, CONV_CH),
        dt_bias=_pad_lanes(p["dt_bias"][li].reshape(1, NH)),
        a_neg=_pad_lanes(-jnp.exp(p["a_log"][li].astype(f32)).reshape(1, NH)),
        d_skip_l=jnp.repeat(p["d_skip"][li], HD).reshape(1, SSD_W), g_ssd=p["g_ssd_norm"][li].reshape(1, SSD_W),
        w_pool=w_pool_bd.reshape(POOL_W, POOL_W).astype(bf16), pool_scale=p["pool_scale"][li].reshape(1, POOL_W),
        g2=p["g_norm2"][li].reshape(1, d),
    )


def _rope_tables(pos):
    half = ROPE // 2
    inv = 1.0 / (ROPE_THETA ** (jnp.arange(half, dtype=f32) / half))
    ang = pos.astype(f32)[:, None] * inv[None, :]
    cos2 = jnp.concatenate([jnp.cos(ang)] * 2, axis=-1)
    sin2 = jnp.concatenate([jnp.sin(ang)] * 2, axis=-1)
    n = pos.shape[0]
    cosq = jnp.concatenate([jnp.ones((n, NOPE), f32), cos2, jnp.zeros((n, LANE - QK), f32)], axis=-1)
    sinq = _pad_lanes(sin2, NOPE)
    return cosq, sinq, _pad_lanes(cos2), _pad_lanes(sin2)


def kernel(x_prompt, x_sample, cache_kv_latent, cache_k_rope, state_ssm, state_conv, state_pool, page_table, c_prompt, c_sample, w_ada, b_ada, g_norm1, w_in, conv_w, conv_b, dt_bias, a_log, d_skip, g_ssd_norm, g_q_lora, w_q_up, g_kv_lora, w_k_up, w_v_up, g_qk_q, g_qk_k, g_mla_out, w_pool, pool_scale, w_out, g_norm2, w_gate, w_up, w_down):
    params = dict(g_norm1=g_norm1, w_in=w_in, conv_w=conv_w, conv_b=conv_b, dt_bias=dt_bias, a_log=a_log, d_skip=d_skip,
                  g_ssd_norm=g_ssd_norm, g_q_lora=g_q_lora, w_q_up=w_q_up, g_kv_lora=g_kv_lora, w_k_up=w_k_up,
                  w_v_up=w_v_up, g_qk_q=g_qk_q, g_qk_k=g_qk_k, g_mla_out=g_mla_out, w_pool=w_pool, pool_scale=pool_scale,
                  w_out=w_out, g_norm2=g_norm2, w_gate=w_gate, w_up=w_up, w_down=w_down)
    depth = w_ada.shape[0]
    bp, seq, d = x_prompt.shape
    bs = x_sample.shape[0]
    past = page_table.shape[1] * PAGE

    mod_p, mod_s = _ada_mod(c_prompt, c_sample, w_ada, b_ada)
    big = _stacked_weights(params)
    state_all = state_ssm.reshape(depth * bs, SSD_W, SSD_N)
    tabs_p = _rope_tables(jnp.arange(seq, dtype=jnp.int32))
    tabs_s = _rope_tables(jnp.full((1,), past, jnp.int32))
    cache_kpe_t = jnp.swapaxes(cache_k_rope, 2, 3)

    yp = x_prompt
    ys = x_sample.reshape(1, bs, d)
    p_new = [[] for _ in range(5)]
    s_new = [[] for _ in range(5)]
    for li in range(depth):
        wts = dict(_layer_weights(params, li), **big)

        z, xbc, dt, u, lat, kpe, q, k, v = _inproj(yp, mod_p, wts, tabs_p, li=li, tm=512, with_kv=True, per_row=False)
        ssd_out, h_t = _ssd_prompt(xbc, z, dt, wts)
        mla_out = _attn_prompt(q, k, v, wts["g_mla"], tq=512)
        pool_out = _pool_prompt(u, wts["w_pool"], wts["pool_scale"])
        yp = _ffn(yp, ssd_out, mla_out, pool_out, mod_p, wts, li=li, tm=256, per_row=False)
        for lst, val in zip(p_new, (lat, kpe, h_t.reshape(bp, NH, HD, SSD_N), xbc[:, seq - (CONV_K - 1):],
                                    u[:, seq - POOL_BUF:])):
            lst.append(val)

        z, xbc, dt, u, lat, kpe, q, kpe128 = _inproj(ys, mod_s, wts, tabs_s, li=li, tm=bs, with_kv=False, per_row=True)
        z, xbc, dt, u, lat, kpe, q, kpe128 = (a[0] for a in (z, xbc, dt, u, lat, kpe, q, kpe128))
        conv_buf = state_conv[li]
        xs, bm, cm, xdt_t, da = _ssd_sample_pre(xbc, jnp.transpose(conv_buf, (1, 0, 2)), dt, wts)
        h_new, y_t = _ssd_sample_state(da[:, :8], state_all, xdt_t, bm, cm, li=li)
        ssd_out = _ssd_sample_post(y_t.T, xs, z, wts)
        qabs, qpe, snew = _mla_sample_pre(q, lat, kpe128, wts)
        qabs = jnp.pad(qabs.reshape(bs, NH, LANE), ((0, 0), (0, HPAD - NH), (0, 0))).astype(bf16)
        qpe = jnp.pad(qpe.reshape(bs, NH, LANE)[:, :, :ROPE], ((0, 0), (0, HPAD - NH), (0, 0))).astype(bf16)
        snew_b = jnp.broadcast_to(jnp.pad(snew[:, :NH], ((0, 0), (0, HPAD - NH)))[:, :, None], (bs, HPAD, LANE))
        o_lat = _attn_sample(page_table, cache_kv_latent, cache_kpe_t, wts["w_kt_flat"], qabs, qpe, snew_b,
                             lat.reshape(bs, 1, KV_LORA), li=li)
        mla_out = _mla_sample_post(o_lat[:, :NH].reshape(bs, NH * KV_LORA), wts["w_v_bd"], wts["g_mla"])
        pool_buf = state_pool[li]
        pool_out = _pool_sample(u, jnp.transpose(pool_buf, (1, 0, 2)), wts["w_pool"], wts["pool_scale"])
        ys = _ffn(ys, ssd_out[None], mla_out[None], pool_out[None], mod_s, wts, li=li, tm=bs, per_row=True)
        conv_new = jnp.concatenate([conv_buf[:, 1:], xbc[:, None, :]], axis=1)
        pool_new = jnp.concatenate([pool_buf[:, 1:], u[:, None, :]], axis=1)
        for lst, val in zip(s_new, (lat[:, None, :], kpe[:, None, :], h_new.reshape(bs, NH, HD, SSD_N), conv_new, pool_new)):
            lst.append(val)

    outs_p = [jnp.stack(vv, axis=0) for vv in p_new]
    outs_s = [jnp.stack(vv, axis=0) for vv in s_new]
    return (yp, ys.reshape(bs, 1, d), *outs_p, *outs_s)
```

```python
import functools
import math

import jax
import jax.numpy as jnp
import numpy as np
from jax import lax
from jax.experimental import pallas as pl
from jax.experimental.pallas import tpu as pltpu

f32 = jnp.float32
bf16 = jnp.bfloat16
HIGHEST = lax.Precision.HIGHEST

EPS = 1e-6
PAGE = 128
NH = 6
HD = 64
NOPE = 64
ROPE = 32
QK = NOPE + ROPE
SSD_W = NH * HD
SSD_G = 2
SSD_N = 128
CONV_K = 4
CONV_CH = SSD_W + 2 * SSD_G * SSD_N
Q_LORA = 256
KV_LORA = 128
POOL_W = 256
POOL_BUF = 15
ROPE_THETA = 10000.0
ATTN_SCALE = QK ** -0.5
PROMPT_Q_SCALE = ATTN_SCALE * math.log2(math.e)
LANE = 128
HPAD = 16
VMEM_LIMIT = 56 * 1024 * 1024

_C_Z, _C_XBC, _C_CQ, _C_CKV, _C_U, _C_KA, _C_KB, _C_DT, _C_END = 0, 384, 1280, 1536, 1664, 1920, 2048, 2176, 2304


def _cparams(sem):
    return pltpu.CompilerParams(dimension_semantics=sem, vmem_limit_bytes=VMEM_LIMIT)


def _silu(x):
    return x * jax.nn.sigmoid(x)


def _softplus(x):
    return jnp.maximum(x, 0.0) + jnp.log1p(jnp.exp(-jnp.abs(x)))


def _rms(x, g, n):
    ms = jnp.sum(x * x, axis=-1, keepdims=True) * (1.0 / n)
    return x * lax.rsqrt(ms + EPS) * g


def _dot(a, b):
    return jnp.dot(a, b, preferred_element_type=f32)


def _dot_nt(a, b):
    return lax.dot_general(a, b, (((1,), (1,)), ((), ())), preferred_element_type=f32)


def _dot_tn(a, b):
    return lax.dot_general(a, b, (((0,), (0,)), ((), ())), preferred_element_type=f32)


def _dot_exact(a, b):
    return jnp.dot(a, b, precision=HIGHEST, preferred_element_type=f32)


def _head_expand():
    r = lax.broadcasted_iota(jnp.int32, (LANE, SSD_W), 0)
    c = lax.broadcasted_iota(jnp.int32, (LANE, SSD_W), 1)
    return jnp.where(c // HD == r, 1.0, 0.0).astype(f32)


def _ada_kernel(c_ref, w_ref, b_ref, op_ref, os_ref):
    s = _silu(c_ref[...]).astype(bf16)
    r = _dot(s, w_ref[0].astype(bf16)) + b_ref[0]
    rp = op_ref.shape[2]
    op_ref[0, 0] = r[:rp]
    os_ref[0, 0] = r[rp:]


def _ada_mod(c_prompt, c_sample, w_ada, b_ada):
    depth, d, n6 = w_ada.shape
    nf = n6 // d
    bp, bs = c_prompt.shape[0], c_sample.shape[0]
    rp = -(-bp // 8) * 8
    c_all = jnp.concatenate([c_prompt, jnp.zeros((rp - bp, d), f32), c_sample], axis=0)
    return pl.pallas_call(
        _ada_kernel,
        grid=(depth, nf),
        in_specs=[pl.BlockSpec((rp + bs, d), lambda l, j: (0, 0)),
                  pl.BlockSpec((1, d, d), lambda l, j: (l, 0, j)),
                  pl.BlockSpec((1, 1, d), lambda l, j: (l, 0, j))],
        out_specs=[pl.BlockSpec((1, 1, rp, d), lambda l, j: (l, j, 0, 0)),
                   pl.BlockSpec((1, 1, bs, d), lambda l, j: (l, j, 0, 0))],
        out_shape=[jax.ShapeDtypeStruct((depth, nf, rp, d), f32), jax.ShapeDtypeStruct((depth, nf, bs, d), f32)],
        compiler_params=_cparams(("arbitrary", "arbitrary")),
        name="ada_mod",
    )(c_all, w_ada, b_ada.reshape(depth, 1, n6))


def _mod_rows(mod_ref, field, per_row, tm):
    if per_row:
        return mod_ref[0, field, pl.ds(pl.multiple_of(pl.program_id(1) * tm, tm), tm), :]
    return mod_ref[0, field, pl.ds(pl.program_id(0), 1), :]


def _inproj_kernel(*refs, with_kv, per_row):
    (x_ref, mod_ref, g1_ref, w_ref, gq_ref, wq_ref, cosq_ref, sinq_ref, gqk_ref, gkv_ref,
     cosk_ref, sink_ref) = refs[:12]
    if with_kv:
        wk_ref, gk_ref, wv_ref = refs[12:15]
        z_ref, xbc_ref, dt_ref, u_ref, lat_ref, kpe_ref, q_ref, k_ref, v_ref = refs[15:]
    else:
        z_ref, xbc_ref, dt_ref, u_ref, lat_ref, kpe_ref, q_ref, kpe128_ref = refs[12:]
    tm, d = x_ref.shape[1:]
    shift, scale = (_mod_rows(mod_ref, f, per_row, tm) for f in (0, 1))
    gqk = gqk_ref[...]
    q_scale = PROMPT_Q_SCALE if with_kv else 1.0
    nsplit = 2 if tm % 512 == 0 else 1
    hm = tm // nsplit

    def rows_of(a, r):
        return a if a.shape[0] == 1 else a[r:r + hm]

    def project(r):
        h = _rms(x_ref[0, r:r + hm, :], g1_ref[...], d) * (1.0 + rows_of(scale, r)) + rows_of(shift, r)
        return _dot(h.astype(bf16), w_ref[0])

    def finish(r, proj):
        rs = slice(r, r + hm)
        z_ref[0, rs, :] = proj[:, _C_Z:_C_XBC]
        xbc_ref[0, rs, :] = proj[:, _C_XBC:_C_CQ]
        u_ref[0, rs, :] = proj[:, _C_U:_C_KA]
        dt_ref[0, rs, :] = proj[:, _C_DT:_C_END]
        cqn = _rms(proj[:, _C_CQ:_C_CKV], gq_ref[...], Q_LORA).astype(bf16)
        qq = _dot(cqn, wq_ref[0])
        cosq = rows_of(cosq_ref[...], r)
        sinq = rows_of(sinq_ref[...], r)
        for hh in range(NH):
            qh = qq[:, LANE * hh:LANE * (hh + 1)] * cosq + qq[:, NH * LANE + LANE * hh:NH * LANE + LANE * (hh + 1)] * sinq
            q_ref[0, rs, LANE * hh:LANE * (hh + 1)] = (_rms(qh, gqk, QK) * q_scale).astype(q_ref.dtype)
        lat = _rms(proj[:, _C_CKV:_C_U], gkv_ref[...], KV_LORA)
        lat_ref[0, rs, :] = lat
        kper = proj[:, _C_KA:_C_KB] * rows_of(cosk_ref[...], r) + proj[:, _C_KB:_C_DT] * rows_of(sink_ref[...], r)
        kpe_ref[0, rs, :] = kper[:, :ROPE]
        if with_kv:
            latb = lat.astype(bf16)
            kn = _dot(latb, wk_ref[...])
            kp = pltpu.roll(kper, NOPE, 1)
            gk = gk_ref[...]
            for hh in range(NH):
                kh = kn[:, LANE * hh:LANE * (hh + 1)] + kp
                k_ref[0, rs, LANE * hh:LANE * (hh + 1)] = _rms(kh, gk, QK).astype(bf16)
            v_ref[0, :, rs] = _dot_nt(wv_ref[...], latb).astype(bf16)
        else:
            kpe128_ref[0, rs, :] = kper

    nxt = project(0)
    for i in range(nsplit):
        proj = nxt
        if i + 1 < nsplit:
            nxt = project((i + 1) * hm)
        finish(i * hm, proj)


def _inproj(x, mod, wts, tabs, *, li, tm, with_kv, per_row):
    g_, t_, d = x.shape
    nt = t_ // tm
    cosq, sinq, cosk, sink = tabs
    tab_rows = cosq.shape[0] != 1
    tab_spec = pl.BlockSpec((tm if tab_rows else 1, LANE), (lambda g, t: (t, 0)) if tab_rows else (lambda g, t: (0, 0)))

    def full(a):
        return pl.BlockSpec(a.shape, lambda g, t: (0,) * a.ndim)

    def layer(a):
        return pl.BlockSpec((1,) + a.shape[1:], lambda g, t: (li,) + (0,) * (a.ndim - 1))

    def row(c):
        return pl.BlockSpec((1, tm, c), lambda g, t: (g, t, 0))

    ins = [x, mod, wts["g1"], wts["w_in"], wts["g_q"], wts["w_q"], cosq, sinq, wts["g_qk_q"], wts["g_kv"], cosk, sink]
    specs = [row(d), layer(mod), full(wts["g1"]), layer(wts["w_in"]), full(wts["g_q"]), layer(wts["w_q"]), tab_spec,
             tab_spec, full(wts["g_qk_q"]), full(wts["g_kv"]), tab_spec, tab_spec]
    widths = [SSD_W, CONV_CH, LANE, POOL_W, KV_LORA, ROPE, NH * LANE]
    dtypes = [f32, f32, f32, f32, f32, f32, bf16 if with_kv else f32]
    if with_kv:
        ins += [wts["w_k"], wts["g_qk_k"], wts["w_v"]]
        specs += [full(wts["w_k"]), full(wts["g_qk_k"]), full(wts["w_v"])]
        widths += [NH * LANE, SSD_W]
        dtypes += [bf16, bf16]
    else:
        widths += [LANE]
        dtypes += [f32]
    out_specs = [row(c) for c in widths]
    out_shape = [jax.ShapeDtypeStruct((g_, t_, c), dt) for c, dt in zip(widths, dtypes)]
    if with_kv:
        out_specs[-1] = pl.BlockSpec((1, SSD_W, tm), lambda g, t: (g, 0, t))
        out_shape[-1] = jax.ShapeDtypeStruct((g_, SSD_W, t_), bf16)
    return pl.pallas_call(
        functools.partial(_inproj_kernel, with_kv=with_kv, per_row=per_row),
        grid=(g_, nt),
        in_specs=specs,
        out_specs=out_specs,
        out_shape=out_shape,
        compiler_params=_cparams(("arbitrary", "arbitrary")),
        name="inproj_kv" if with_kv else "inproj",
    )(*ins)


def _ssd_prompt_kernel(xbc_ref, z_ref, dt_ref, cw_ref, cb_ref, dtb_ref, a_ref, dsk_ref, gn_ref,
                       y_ref, st_ref, ext_ref):
    c = pl.program_id(1)
    t_ = xbc_ref.shape[1]

    @pl.when(c == 0)
    def _():
        ext_ref[0:8, :] = jnp.zeros((8, CONV_CH), f32)
        st_ref[...] = jnp.zeros(st_ref.shape, f32)

    x = xbc_ref[0]
    ext_ref[8:8 + t_, :] = x
    cw = cw_ref[...]
    conv = cb_ref[...] + cw[3:4] * x
    for k in range(CONV_K - 1):
        conv = conv + cw[k:k + 1] * ext_ref[5 + k:5 + k + t_, :]
    ext_ref[0:8, :] = x[t_ - 8:, :]
    act = _silu(conv)
    xs = act[:, :SSD_W]
    bm = [act[:, SSD_W + SSD_N * g:SSD_W + SSD_N * (g + 1)].astype(bf16) for g in range(SSD_G)]
    cm = [act[:, SSD_W + SSD_N * (SSD_G + g):SSD_W + SSD_N * (SSD_G + g + 1)].astype(bf16) for g in range(SSD_G)]

    dt = _softplus(dt_ref[0] + dtb_ref[...])
    a = dt * a_ref[...]
    ri = lax.broadcasted_iota(jnp.int32, (t_, t_), 0)
    ci = lax.broadcasted_iota(jnp.int32, (t_, t_), 1)
    tril = ri >= ci
    acum = _dot_exact(jnp.where(tril, 1.0, 0.0).astype(f32), a)
    acum_t = acum.T
    expand = _head_expand()
    dt_l = _dot_exact(dt, expand)
    acum_l = _dot_exact(acum, expand)
    last_l = acum_l[t_ - 1:t_, :]
    e_l = jnp.exp(acum_l)
    decay_l = jnp.exp(last_l - acum_l)
    xdt = xs * dt_l
    xdt_w = xdt * decay_l

    cb = [_dot_nt(cm[g], bm[g]) for g in range(SSD_G)]
    lane = lax.broadcasted_iota(jnp.int32, (t_, LANE), 1)
    rowi = lax.broadcasted_iota(jnp.int32, (LANE, SSD_N), 0)
    first = lane < HD
    rfirst = rowi < HD
    y_cols = []
    for j in range(NH // 2):
        sl = slice(LANE * j, LANE * (j + 1))
        h0, h1 = 2 * j, 2 * j + 1
        g0, g1 = h0 // (NH // SSD_G), h1 // (NH // SSD_G)
        xj = xdt[:, sl].astype(bf16)
        yd = []
        for hh, gg in ((h0, g0), (h1, g1)):
            seg = jnp.where(tril, jnp.exp(jnp.minimum(acum[:, hh:hh + 1] - acum_t[hh:hh + 1, :], 0.0)), 0.0)
            yd.append(_dot((cb[gg] * seg).astype(bf16), xj))
        y_diag = jnp.where(first, yd[0], yd[1])
        hp = st_ref[0, sl, :]
        hpb = hp.astype(bf16)
        xw = xdt_w[:, sl].astype(bf16)
        if g0 == g1:
            y_off = _dot_nt(cm[g0], hpb)
            s_new = _dot_tn(xw, bm[g0])
        else:
            y_off = jnp.where(first, _dot_nt(cm[g0], hpb), _dot_nt(cm[g1], hpb))
            s_new = jnp.where(rfirst, _dot_tn(xw, bm[g0]), _dot_tn(xw, bm[g1]))
        tot = jnp.where(rfirst, jnp.exp(acum[t_ - 1:t_, h0:h0 + 1]), jnp.exp(acum[t_ - 1:t_, h1:h1 + 1]))
        st_ref[0, sl, :] = tot * hp + s_new
        y_cols.append(y_diag + y_off * e_l[:, sl])
    y = jnp.concatenate(y_cols, axis=-1) + dsk_ref[...] * xs
    v = y * _silu(z_ref[0])
    gl = lax.broadcasted_iota(jnp.int32, (t_, SSD_W), 1) < SSD_W // SSD_G
    v2 = v * v
    ss0 = jnp.sum(jnp.where(gl, v2, 0.0), axis=-1, keepdims=True)
    ss1 = jnp.sum(jnp.where(gl, 0.0, v2), axis=-1, keepdims=True)
    gw = SSD_W // SSD_G
    rinv = jnp.where(gl, lax.rsqrt(ss0 * (1.0 / gw) + EPS), lax.rsqrt(ss1 * (1.0 / gw) + EPS))
    y_ref[0] = v * rinv * gn_ref[...]


def _ssd_prompt(xbc, z, dt, wts):
    b_, l_, _ = xbc.shape
    t_ = 128
    nc = l_ // t_

    def full(a):
        return pl.BlockSpec(a.shape, lambda b, c: (0,) * a.ndim)

    def row(w):
        return pl.BlockSpec((1, t_, w), lambda b, c: (b, c, 0))

    small = [wts["conv_w"], wts["conv_b"], wts["dt_bias"], wts["a_neg"], wts["d_skip_l"], wts["g_ssd"]]
    return pl.pallas_call(
        _ssd_prompt_kernel,
        grid=(b_, nc),
        in_specs=[row(CONV_CH), row(SSD_W), row(LANE)] + [full(a) for a in small],
        out_specs=[row(SSD_W), pl.BlockSpec((1, SSD_W, SSD_N), lambda b, c: (b, 0, 0))],
        out_shape=[jax.ShapeDtypeStruct((b_, l_, SSD_W), f32), jax.ShapeDtypeStruct((b_, SSD_W, SSD_N), f32)],
        scratch_shapes=[pltpu.VMEM((8 + t_, CONV_CH), f32)],
        compiler_params=_cparams(("arbitrary", "arbitrary")),
        name="ssd_prompt",
    )(xbc, z, dt, *small)


def _attn_prompt_kernel(q_ref, k_ref, vt_ref, g_ref, o_ref, m_ref, l_ref, acc_ref, *, tq):
    qi = pl.program_id(1)
    m_ref[...] = jnp.full(m_ref.shape, -1e30, f32)
    l_ref[...] = jnp.zeros(l_ref.shape, f32)
    acc_ref[...] = jnp.zeros(acc_ref.shape, f32)

    def tile(key0, nk, q0, nq, diagonal):
        kblk = k_ref[0, pl.ds(key0, nk), :]
        vblk = vt_ref[0, :, pl.ds(key0, nk)]
        qs = slice(q0, q0 + nq)
        if diagonal:
            masked = lax.broadcasted_iota(jnp.int32, (nk, nq), 0) > lax.broadcasted_iota(jnp.int32, (nk, nq), 1)

        def qk(hh):
            return _dot_nt(kblk[:, LANE * hh:LANE * (hh + 1)], q_ref[0, qs, LANE * hh:LANE * (hh + 1)])

        s_next = qk(0)
        for hh in range(NH):
            s = s_next
            if hh + 1 < NH:
                s_next = qk(hh + 1)
            if diagonal:
                s = jnp.where(masked, -1e30, s)
            m_old = m_ref[hh, :, qs]
            m_new = jnp.maximum(m_old, jnp.max(s, axis=0, keepdims=True))
            alpha = jnp.exp2(m_old - m_new)
            pe = jnp.exp2(s - m_new)
            l_ref[hh, :, qs] = alpha * l_ref[hh, :, qs] + jnp.sum(pe.reshape(nk // 8, 8, nq), axis=0)
            m_ref[hh, :, qs] = m_new
            rows = slice(HD * hh, HD * (hh + 1))
            acc_ref[rows, qs] = acc_ref[rows, qs] * alpha + _dot(vblk[rows, :], pe.astype(bf16))

    def body(j, carry):
        tile(pl.multiple_of(j * tq, tq), tq, 0, tq, False)
        return carry

    lax.fori_loop(0, qi, body, 0)
    half = tq // 2
    base = pl.multiple_of(qi * tq, tq)
    tile(base, half, 0, tq, True)
    tile(base + half, half, half, half, True)
    parts = []
    for hh in range(NH):
        l = jnp.sum(l_ref[hh], axis=0, keepdims=True)
        parts.append(acc_ref[HD * hh:HD * (hh + 1), :] * (1.0 / l))
    o_t = jnp.concatenate(parts, axis=0)
    ms = jnp.sum(o_t * o_t, axis=0, keepdims=True) * (1.0 / SSD_W)
    o_ref[0] = (o_t * lax.rsqrt(ms + EPS)).T * g_ref[...]


def _attn_prompt(q, k, v_t, g_out, *, tq):
    b_, l_, _ = q.shape
    return pl.pallas_call(
        functools.partial(_attn_prompt_kernel, tq=tq),
        grid=(b_, l_ // tq),
        in_specs=[pl.BlockSpec((1, tq, NH * LANE), lambda b, i: (b, i, 0)),
                  pl.BlockSpec((1, l_, NH * LANE), lambda b, i: (b, 0, 0)),
                  pl.BlockSpec((1, SSD_W, l_), lambda b, i: (b, 0, 0)),
                  pl.BlockSpec(g_out.shape, lambda b, i: (0, 0))],
        out_specs=pl.BlockSpec((1, tq, SSD_W), lambda b, i: (b, i, 0)),
        out_shape=jax.ShapeDtypeStruct((b_, l_, SSD_W), f32),
        scratch_shapes=[pltpu.VMEM((NH, 1, tq), f32), pltpu.VMEM((NH, 8, tq), f32), pltpu.VMEM((SSD_W, tq), f32)],
        compiler_params=_cparams(("arbitrary", "arbitrary")),
        name="attn_prompt",
    )(q, k, v_t, g_out)


def _pool_select(lane, a, b, c, d):
    return jnp.where(lane < 64, a, jnp.where(lane < 128, b, jnp.where(lane < 192, c, d)))


def _pool_prompt_kernel(u_ref, w_ref, sc_ref, o_ref, e1, e2, e4, e8, *, rt):
    t_ = u_ref.shape[1]
    hist = 16
    for e in (e1, e2, e4, e8):
        e[0:hist, :] = jnp.zeros((hist, POOL_W), f32)
    e1[hist:hist + t_, :] = u_ref[0]
    lane = lax.broadcasted_iota(jnp.int32, (rt, POOL_W), 1)
    win = _pool_select(lane, 2, 4, 8, 16)
    for i in range(t_ // rt):
        r0 = hist + i * rt
        a = e1[r0:r0 + rt, :]
        s2 = a + e1[r0 - 1:r0 - 1 + rt, :]
        e2[r0:r0 + rt, :] = s2
        s4 = s2 + e2[r0 - 2:r0 - 2 + rt, :]
        e4[r0:r0 + rt, :] = s4
        s8 = s4 + e4[r0 - 4:r0 - 4 + rt, :]
        e8[r0:r0 + rt, :] = s8
        s16 = s8 + e8[r0 - 8:r0 - 8 + rt, :]
        pos = lax.broadcasted_iota(jnp.int32, (rt, POOL_W), 0) + i * rt
        cnt = jnp.minimum(pos + 1, win).astype(f32)
        pooled = _pool_select(lane, s2, s4, s8, s16) / cnt - a
        o_ref[0, i * rt:(i + 1) * rt, :] = _dot(pooled.astype(bf16), w_ref[...]) * sc_ref[...]


def _pool_prompt(u, w_bd, scale):
    b_, l_, _ = u.shape
    return pl.pallas_call(
        functools.partial(_pool_prompt_kernel, rt=256),
        grid=(b_,),
        in_specs=[pl.BlockSpec((1, l_, POOL_W), lambda b: (b, 0, 0)),
                  pl.BlockSpec(w_bd.shape, lambda b: (0, 0)),
                  pl.BlockSpec(scale.shape, lambda b: (0, 0))],
        out_specs=pl.BlockSpec((1, l_, POOL_W), lambda b: (b, 0, 0)),
        out_shape=jax.ShapeDtypeStruct((b_, l_, POOL_W), f32),
        scratch_shapes=[pltpu.VMEM((16 + l_, POOL_W), f32)] * 4,
        compiler_params=_cparams(("arbitrary",)),
        name="pool_prompt",
    )(u, w_bd, scale)


def _pool_sample_kernel(u_ref, buf_ref, w_ref, sc_ref, o_ref):
    u = u_ref[...]
    run = u
    sums = {}
    for i in range(1, 16):
        run = run + buf_ref[POOL_BUF - i]
        if i + 1 in (2, 4, 8, 16):
            sums[i + 1] = run * (1.0 / (i + 1))
    lane = lax.broadcasted_iota(jnp.int32, u.shape, 1)
    pooled = _pool_select(lane, sums[2], sums[4], sums[8], sums[16]) - u
    o_ref[...] = _dot(pooled.astype(bf16), w_ref[...]) * sc_ref[...]


def _pool_sample(u, buf_t, w_bd, scale):
    return pl.pallas_call(
        _pool_sample_kernel,
        out_shape=jax.ShapeDtypeStruct(u.shape, f32),
        compiler_params=pltpu.CompilerParams(vmem_limit_bytes=VMEM_LIMIT),
        name="pool_sample",
    )(u, buf_t, w_bd, scale)


def _ffn_kernel(x_ref, ssd_ref, mla_ref, pool_ref, mod_ref, gn_ref, wo_ref, wg_ref, wu_ref, wd_ref, o_ref,
                *, nch, per_row):
    tm, d = x_ref.shape[1:]
    gate1, shift2, scale2, gate2 = (_mod_rows(mod_ref, f, per_row, tm) for f in (2, 3, 4, 5))
    hc = wg_ref.shape[2] // nch
    nsplit = 2 if tm % 512 == 0 else 1
    hm = tm // nsplit

    def rows_of(a, r):
        return a if a.shape[0] == 1 else a[r:r + hm]

    def out_proj(r):
        rs = slice(r, r + hm)
        mixed = jnp.concatenate([ssd_ref[0, rs, :].astype(bf16), mla_ref[0, rs, :].astype(bf16),
                                 pool_ref[0, rs, :].astype(bf16)], axis=-1)
        return _dot(mixed, wo_ref[0])

    def residual_norm(r, mix):
        x1 = x_ref[0, r:r + hm, :] + rows_of(gate1, r) * mix
        return x1, (_rms(x1, gn_ref[...], d) * (1.0 + rows_of(scale2, r)) + rows_of(shift2, r)).astype(bf16)

    def gate_up(h2, c):
        return _dot(h2, wg_ref[0, :, hc * c:hc * (c + 1)]), _dot(h2, wu_ref[0, :, hc * c:hc * (c + 1)])

    mixes = [out_proj(i * hm) for i in range(nsplit)]
    x1, h2 = residual_norm(0, mixes[0])
    for i in range(nsplit):
        r = i * hm
        acc = jnp.zeros(x1.shape, f32)
        nxt = gate_up(h2, 0)
        if i + 1 < nsplit:
            x1_next, h2_next = residual_norm(r + hm, mixes[i + 1])
        for c in range(nch):
            gate, up = nxt
            if c + 1 < nch:
                nxt = gate_up(h2, c + 1)
            acc = acc + _dot((_silu(gate) * up).astype(bf16), wd_ref[0, hc * c:hc * (c + 1), :])
        o_ref[0, r:r + hm, :] = x1 + rows_of(gate2, r) * acc
        if i + 1 < nsplit:
            x1, h2 = x1_next, h2_next


def _ffn(x, ssd, mla, pool, mod, wts, *, li, tm, per_row):
    g_, t_, d = x.shape

    def layer(a):
        return pl.BlockSpec((1,) + a.shape[1:], lambda g, t: (li,) + (0,) * (a.ndim - 1), pipeline_mode=pl.Buffered(1))

    def row(c):
        return pl.BlockSpec((1, tm, c), lambda g, t: (g, t, 0))

    ws = [wts["w_out"], wts["w_gate"], wts["w_up"], wts["w_down"]]
    return pl.pallas_call(
        functools.partial(_ffn_kernel, nch=11, per_row=per_row),
        grid=(g_, t_ // tm),
        in_specs=[row(d), row(SSD_W), row(SSD_W), row(POOL_W), layer(mod),
                  pl.BlockSpec(wts["g2"].shape, lambda g, t: (0, 0))] + [layer(a) for a in ws],
        out_specs=row(d),
        out_shape=jax.ShapeDtypeStruct((g_, t_, d), f32),
        compiler_params=_cparams(("arbitrary", "arbitrary")),
        name="outproj_ffn",
    )(x, ssd, mla, pool, mod, wts["g2"], *ws)


def _ssd_sample_pre_kernel(xbc_ref, buf_ref, dt_ref, cw_ref, cb_ref, dtb_ref, a_ref,
                           xs_ref, b_ref, c_ref, xdt_t_ref, da_ref):
    cw = cw_ref[...]
    conv = cb_ref[...] + cw[3:4] * xbc_ref[...]
    for k in range(CONV_K - 1):
        conv = conv + cw[k:k + 1] * buf_ref[k]
    act = _silu(conv)
    xs = act[:, :SSD_W]
    xs_ref[...] = xs
    b_ref[...] = act[:, SSD_W:SSD_W + SSD_G * SSD_N]
    c_ref[...] = act[:, SSD_W + SSD_G * SSD_N:]
    dt = _softplus(dt_ref[...] + dtb_ref[...])
    da_ref[...] = jnp.exp(dt * a_ref[...])
    xdt = xs * _dot_exact(dt, _head_expand())
    for j in range(SSD_W // LANE):
        xdt_t_ref[LANE * j:LANE * (j + 1), :] = xdt[:, LANE * j:LANE * (j + 1)].T


def _ssd_sample_pre(xbc, buf_t, dt, wts):
    b_ = xbc.shape[0]
    shapes = [(b_, SSD_W), (b_, SSD_G * SSD_N), (b_, SSD_G * SSD_N), (SSD_W, b_), (b_, LANE)]
    return pl.pallas_call(
        _ssd_sample_pre_kernel,
        out_shape=[jax.ShapeDtypeStruct(s, f32) for s in shapes],
        compiler_params=pltpu.CompilerParams(vmem_limit_bytes=VMEM_LIMIT),
        name="ssd_sample_pre",
    )(xbc, buf_t, dt, wts["conv_w"], wts["conv_b"], wts["dt_bias"], wts["a_neg"])


def _ssd_sample_state_kernel(da_ref, st_ref, xdt_t_ref, b_ref, c_ref, new_ref, y_t_ref, *, tb):
    i = pl.program_id(0)

    @pl.when(i == 0)
    def _():
        y_t_ref[...] = jnp.zeros(y_t_ref.shape, f32)

    nb = xdt_t_ref.shape[1]
    lane = lax.broadcasted_iota(jnp.int32, (HD, nb), 1)
    pairs = [(hh, bl) for hh in range(NH) for bl in range(tb)]
    sels = [lane == i * tb + bl for bl in range(tb)]
    cols = {}
    for hh, bl in pairs:
        cols[hh, bl] = jnp.sum(jnp.where(sels[bl], xdt_t_ref[HD * hh:HD * (hh + 1), :], 0.0), axis=1, keepdims=True)
    ycols = {}
    for hh, bl in pairs:
        g = hh // (NH // SSD_G)
        rs = slice(HD * hh, HD * (hh + 1))
        brow = b_ref[bl:bl + 1, SSD_N * g:SSD_N * (g + 1)]
        crow = c_ref[bl:bl + 1, SSD_N * g:SSD_N * (g + 1)]
        new = da_ref[i * tb + bl, hh] * st_ref[bl, rs, :] + cols[hh, bl] * brow
        new_ref[bl, rs, :] = new
        ycols[hh, bl] = jnp.sum(new * crow, axis=1, keepdims=True)
    for hh in range(NH):
        rs = slice(HD * hh, HD * (hh + 1))
        y_rows = y_t_ref[rs, :]
        for bl in range(tb):
            y_rows = jnp.where(sels[bl], ycols[hh, bl], y_rows)
        y_t_ref[rs, :] = y_rows


def _ssd_sample_state(da, state_all, xdt_t, bm, cm, *, li, tb=8):
    b_ = xdt_t.shape[1]
    nblk = b_ // tb
    return pl.pallas_call(
        functools.partial(_ssd_sample_state_kernel, tb=tb),
        grid=(nblk,),
        in_specs=[pl.BlockSpec(memory_space=pltpu.SMEM),
                  pl.BlockSpec((tb, SSD_W, SSD_N), lambda i: (li * nblk + i, 0, 0)),
                  pl.BlockSpec(xdt_t.shape, lambda i: (0, 0)),
                  pl.BlockSpec((tb, SSD_G * SSD_N), lambda i: (i, 0)),
                  pl.BlockSpec((tb, SSD_G * SSD_N), lambda i: (i, 0))],
        out_specs=[pl.BlockSpec((tb, SSD_W, SSD_N), lambda i: (i, 0, 0)),
                   pl.BlockSpec((SSD_W, b_), lambda i: (0, 0))],
        out_shape=[jax.ShapeDtypeStruct((b_, SSD_W, SSD_N), f32), jax.ShapeDtypeStruct((SSD_W, b_), f32)],
        compiler_params=_cparams(("arbitrary",)),
        name="ssd_sample_state",
    )(da, state_all, xdt_t, bm, cm)


def _ssd_sample_post_kernel(y_ref, xs_ref, z_ref, dsk_ref, gn_ref, o_ref):
    y = y_ref[...] + dsk_ref[...] * xs_ref[...]
    v = y * _silu(z_ref[...])
    gl = lax.broadcasted_iota(jnp.int32, v.shape, 1) < SSD_W // SSD_G
    v2 = v * v
    gw = SSD_W // SSD_G
    ss0 = jnp.sum(jnp.where(gl, v2, 0.0), axis=-1, keepdims=True)
    ss1 = jnp.sum(jnp.where(gl, 0.0, v2), axis=-1, keepdims=True)
    rinv = jnp.where(gl, lax.rsqrt(ss0 * (1.0 / gw) + EPS), lax.rsqrt(ss1 * (1.0 / gw) + EPS))
    o_ref[...] = v * rinv * gn_ref[...]


def _ssd_sample_post(y, xs, z, wts):
    return pl.pallas_call(
        _ssd_sample_post_kernel,
        out_shape=jax.ShapeDtypeStruct(y.shape, f32),
        name="ssd_sample_post",
    )(y, xs, z, wts["d_skip_l"], wts["g_ssd"])


def _mla_sample_pre_kernel(q_ref, lat_ref, kpe_ref, wk_ref, wkt_ref, gk_ref, qabs_ref, qpe_ref, snew_ref):
    kn = _dot(lat_ref[...].astype(bf16), wk_ref[...])
    kp = pltpu.roll(kpe_ref[...], NOPE, 1)
    gk = gk_ref[...]
    lane = lax.broadcasted_iota(jnp.int32, snew_ref.shape, 1)
    snew = jnp.zeros(snew_ref.shape, f32)
    for hh in range(NH):
        sl = slice(LANE * hh, LANE * (hh + 1))
        qh = q_ref[:, sl]
        knew = _rms(kn[:, sl] + kp, gk, QK)
        snew = jnp.where(lane == hh, jnp.sum(qh * knew, axis=-1, keepdims=True) * ATTN_SCALE, snew)
        qg = qh * gk
        qabs_ref[:, sl] = _dot_exact(qg, wkt_ref[hh])
        qpe_ref[:, sl] = pltpu.roll(qg, NOPE, 1)
    snew_ref[...] = snew


def _mla_sample_pre(q, lat, kpe128, wts):
    b_ = q.shape[0]
    shapes = [(b_, NH * LANE), (b_, NH * LANE), (b_, LANE)]
    return pl.pallas_call(
        _mla_sample_pre_kernel,
        out_shape=[jax.ShapeDtypeStruct(s, f32) for s in shapes],
        name="mla_sample_pre",
    )(q, lat, kpe128, wts["w_k"], wts["w_kt"], wts["g_qk_k"])


def _attn_sample_kernel(pt_ref, lat_hbm, kpe_hbm, wkt_ref, qabs_ref, qpe_ref, snew_ref, latnew_ref, o_ref,
                        lat_buf, kpe_buf, sem, wext, latb_ref, s_ref, *, li, pages, nchunk):
    b = pl.program_id(0)
    nb = pl.num_programs(0)
    rc = pages * PAGE
    slot = b % 2

    def aligned(idx, size):
        return idx * size if isinstance(idx, int) else pl.multiple_of(idx * size, size)

    def copies(bb, c, i, sl):
        page_idx = c * pages + i
        page = pt_ref[bb, page_idx]
        off = aligned(page_idx, PAGE)
        return (pltpu.make_async_copy(lat_hbm.at[li, page], lat_buf.at[sl, pl.ds(off, PAGE), :], sem.at[0, sl, c]),
                pltpu.make_async_copy(kpe_hbm.at[li, page], kpe_buf.at[sl, :, pl.ds(off, PAGE)], sem.at[1, sl, c]))

    def start_chunk(bb, c, sl):
        for i in range(pages):
            for cp in copies(bb, c, i, sl):
                cp.start()

    @pl.when(b == 0)
    def _():
        def issue(c, carry):
            start_chunk(0, c, 0)
            start_chunk(jnp.minimum(1, nb - 1), c, 1)
            return carry
        lax.fori_loop(0, nchunk, issue, 0)
        wext[0:NH * HD, :] = wkt_ref[...]

    wext[NH * HD:NH * HD + HPAD, :] = qabs_ref[0]
    qpe = qpe_ref[0]
    rowi = lax.broadcasted_iota(jnp.int32, (HPAD, rc), 0)

    def scores(c, carry):
        for i in range(pages):
            for cp in copies(b, c, i, slot):
                cp.wait()
        off = aligned(c, rc)
        latb = lat_buf[slot, pl.ds(off, rc), :].astype(bf16)
        latb_ref[pl.ds(off, rc), :] = latb
        kpe = kpe_buf[slot, :, pl.ds(off, rc)]
        a = _dot_nt(wext[...], latb)
        nsq = jnp.zeros((HPAD, rc), f32)
        for hh in range(NH):
            kh = a[HD * hh:HD * (hh + 1), :]
            nsq = jnp.where(rowi == hh, jnp.sum(kh * kh, axis=0, keepdims=True), nsq)
        s_pe = _dot(qpe, kpe.astype(bf16))
        ksq = jnp.sum(kpe * kpe, axis=0, keepdims=True)
        s_ref[:, pl.ds(off, rc)] = (a[NH * HD:, :] + s_pe) * lax.rsqrt((nsq + ksq) * (1.0 / QK) + EPS) * ATTN_SCALE
        return carry

    lax.fori_loop(0, nchunk, scores, 0)

    b_ahead = jnp.minimum(b + 2, nb - 1)
    for c in range(nchunk):
        start_chunk(b_ahead, c, slot)

    s = s_ref[...]
    s_new = snew_ref[0][:, 0:1]
    m = jnp.maximum(jnp.max(s, axis=-1, keepdims=True), s_new)
    pe = jnp.exp(s - m)
    p_new = jnp.exp(s_new - m)
    l = jnp.sum(pe, axis=-1, keepdims=True) + p_new
    acc = _dot(pe.astype(bf16), latb_ref[...]) + p_new * latnew_ref[0]
    o_ref[0] = acc / l

    @pl.when(b == nb - 1)
    def _():
        def drain(c, carry):
            for sl in range(2):
                for i in range(pages):
                    for cp in copies(b, c, i, sl):
                        cp.wait()
            return carry
        lax.fori_loop(0, nchunk, drain, 0)


def _attn_sample(page_table, cache_lat, cache_kpe_t, wkt, qabs, qpe, snew, lat_new, *, li, pages=64):
    b_, npages = page_table.shape
    nchunk = npages // pages
    assert npages % pages == 0
    seq = npages * PAGE
    grid_spec = pltpu.PrefetchScalarGridSpec(
        num_scalar_prefetch=1,
        grid=(b_,),
        in_specs=[pl.BlockSpec(memory_space=pl.ANY),
                  pl.BlockSpec(memory_space=pl.ANY),
                  pl.BlockSpec(wkt.shape, lambda b, pt: (0, 0)),
                  pl.BlockSpec((1, HPAD, KV_LORA), lambda b, pt: (b, 0, 0)),
                  pl.BlockSpec((1, HPAD, ROPE), lambda b, pt: (b, 0, 0)),
                  pl.BlockSpec((1, HPAD, LANE), lambda b, pt: (b, 0, 0)),
                  pl.BlockSpec((1, 1, KV_LORA), lambda b, pt: (b, 0, 0))],
        out_specs=pl.BlockSpec((1, HPAD, KV_LORA), lambda b, pt: (b, 0, 0)),
        scratch_shapes=[pltpu.VMEM((2, seq, KV_LORA), f32),
                        pltpu.VMEM((2, ROPE, seq), f32),
                        pltpu.SemaphoreType.DMA((2, 2, nchunk)),
                        pltpu.VMEM((NH * HD + HPAD, KV_LORA), bf16),
                        pltpu.VMEM((seq, KV_LORA), bf16),
                        pltpu.VMEM((HPAD, seq), f32)],
    )
    return pl.pallas_call(
        functools.partial(_attn_sample_kernel, li=li, pages=pages, nchunk=nchunk),
        grid_spec=grid_spec,
        out_shape=jax.ShapeDtypeStruct((b_, HPAD, KV_LORA), f32),
        compiler_params=_cparams(("arbitrary",)),
        name="attn_sample",
    )(page_table, cache_lat, cache_kpe_t, wkt, qabs, qpe, snew, lat_new)


def _mla_sample_post_kernel(o_ref, wv_ref, g_ref, out_ref):
    out_ref[...] = _rms(_dot(o_ref[...].astype(bf16), wv_ref[...]), g_ref[...], SSD_W)


def _mla_sample_post(o_lat, w_v_bd, g_out):
    return pl.pallas_call(
        _mla_sample_post_kernel,
        out_shape=jax.ShapeDtypeStruct((o_lat.shape[0], SSD_W), f32),
        name="mla_sample_post",
    )(o_lat, w_v_bd, g_out)


def _pad_heads(w, lo, hi):
    pad = [(0, 0)] * (w.ndim - 1) + [(lo, LANE - hi)]
    w = jnp.pad(w, pad)
    return w.reshape(w.shape[:-2] + (NH * LANE,))


def _rot_cols(w):
    half = ROPE // 2
    return jnp.concatenate([-w[..., half:], w[..., :half]], axis=-1)


def _pad_lanes(a, lo=0):
    return jnp.pad(a, [(0, 0)] * (a.ndim - 1) + [(lo, LANE - lo - a.shape[-1])])


def _stacked_weights(p):
    offs = np.cumsum([SSD_W, CONV_CH, NH, Q_LORA, KV_LORA, ROPE, POOL_W])
    wz, wxbc, wdt, wcq, wckv, wkpe, wu = jnp.split(p["w_in"], offs[:-1].tolist(), axis=2)
    w_in = jnp.concatenate([wz, wxbc, wcq, wckv, wu, _pad_lanes(wkpe), _pad_lanes(_rot_cols(wkpe)), _pad_lanes(wdt)],
                           axis=2).astype(bf16)
    wq = p["w_q_up"]
    w_q = jnp.concatenate([_pad_heads(wq, 0, QK), _pad_heads(_rot_cols(wq[..., NOPE:]), NOPE, QK)], axis=2).astype(bf16)
    return dict(w_in=w_in, w_q=w_q, w_out=p["w_out"].astype(bf16), w_gate=p["w_gate"].astype(bf16),
                w_up=p["w_up"].astype(bf16), w_down=p["w_down"].astype(bf16))


def _layer_weights(p, li):
    d = p["w_in"].shape[1]
    wk = p["w_k_up"][li]
    wk_pad = _pad_heads(wk, 0, NOPE)
    wk_t = jnp.transpose(wk, (1, 2, 0))
    wv = p["w_v_up"][li]
    w_v_bd = jnp.zeros((NH, KV_LORA, NH, HD), f32)
    w_v_bd = w_v_bd.at[jnp.arange(NH), :, jnp.arange(NH), :].set(jnp.transpose(wv, (1, 0, 2)))
    wp = p["w_pool"][li]
    ng = wp.shape[0]
    w_pool_bd = jnp.zeros((ng, HD, ng, HD), f32).at[jnp.arange(ng), :, jnp.arange(ng), :].set(wp)
    return dict(
        g1=p["g_norm1"][li].reshape(1, d),
        g_q=p["g_q_lora"][li].reshape(1, Q_LORA),
        g_qk_q=_pad_lanes(p["g_qk_q"][li].reshape(1, QK)), g_qk_k=_pad_lanes(p["g_qk_k"][li].reshape(1, QK)),
        g_kv=p["g_kv_lora"][li].reshape(1, KV_LORA),
        w_k=wk_pad.astype(bf16), w_v=wv.reshape(KV_LORA, NH * HD).T.astype(bf16),
        w_kt=jnp.pad(wk_t, ((0, 0), (0, LANE - NOPE), (0, 0))),
        w_kt_flat=wk_t.reshape(NH * NOPE, KV_LORA).astype(bf16),
        w_v_bd=w_v_bd.reshape(NH * KV_LORA, NH * HD).astype(bf16),
        g_mla=p["g_mla_out"][li].reshape(1, SSD_W),
        conv_w=p["conv_w"][li], conv_b=p["conv_b"][li].reshape(1, CONV_CH),
        dt_bias=_pad_lanes(p["dt_bias"][li].reshape(1, NH)),
        a_neg=_pad_lanes(-jnp.exp(p["a_log"][li].astype(f32)).reshape(1, NH)),
        d_skip_l=jnp.repeat(p["d_skip"][li], HD).reshape(1, SSD_W), g_ssd=p["g_ssd_norm"][li].reshape(1, SSD_W),
        w_pool=w_pool_bd.reshape(POOL_W, POOL_W).astype(bf16), pool_scale=p["pool_scale"][li].reshape(1, POOL_W),
        g2=p["g_norm2"][li].reshape(1, d),
    )


def _rope_tables(pos):
    half = ROPE // 2
    inv = 1.0 / (ROPE_THETA ** (jnp.arange(half, dtype=f32) / half))
    ang = pos.astype(f32)[:, None] * inv[None, :]
    cos2 = jnp.concatenate([jnp.cos(ang)] * 2, axis=-1)
    sin2 = jnp.concatenate([jnp.sin(ang)] * 2, axis=-1)
    n = pos.shape[0]
    cosq = jnp.concatenate([jnp.ones((n, NOPE), f32), cos2, jnp.zeros((n, LANE - QK), f32)], axis=-1)
    sinq = _pad_lanes(sin2, NOPE)
    return cosq, sinq, _pad_lanes(cos2), _pad_lanes(sin2)


def kernel(x_prompt, x_sample, cache_kv_latent, cache_k_rope, state_ssm, state_conv, state_pool, page_table, c_prompt, c_sample, w_ada, b_ada, g_norm1, w_in, conv_w, conv_b, dt_bias, a_log, d_skip, g_ssd_norm, g_q_lora, w_q_up, g_kv_lora, w_k_up, w_v_up, g_qk_q, g_qk_k, g_mla_out, w_pool, pool_scale, w_out, g_norm2, w_gate, w_up, w_down):
    params = dict(g_norm1=g_norm1, w_in=w_in, conv_w=conv_w, conv_b=conv_b, dt_bias=dt_bias, a_log=a_log, d_skip=d_skip,
                  g_ssd_norm=g_ssd_norm, g_q_lora=g_q_lora, w_q_up=w_q_up, g_kv_lora=g_kv_lora, w_k_up=w_k_up,
                  w_v_up=w_v_up, g_qk_q=g_qk_q, g_qk_k=g_qk_k, g_mla_out=g_mla_out, w_pool=w_pool, pool_scale=pool_scale,
                  w_out=w_out, g_norm2=g_norm2, w_gate=w_gate, w_up=w_up, w_down=w_down)
    depth = w_ada.shape[0]
    bp, seq, d = x_prompt.shape
    bs = x_sample.shape[0]
    past = page_table.shape[1] * PAGE

    mod_p, mod_s = _ada_mod(c_prompt, c_sample, w_ada, b_ada)
    big = _stacked_weights(params)
    state_all = state_ssm.reshape(depth * bs, SSD_W, SSD_N)
    tabs_p = _rope_tables(jnp.arange(seq, dtype=jnp.int32))
    tabs_s = _rope_tables(jnp.full((1,), past, jnp.int32))
    cache_kpe_t = jnp.swapaxes(cache_k_rope, 2, 3)

    yp = x_prompt
    ys = x_sample.reshape(1, bs, d)
    p_new = [[] for _ in range(5)]
    s_new = [[] for _ in range(5)]
    for li in range(depth):
        wts = dict(_layer_weights(params, li), **big)

        z, xbc, dt, u, lat, kpe, q, k, v = _inproj(yp, mod_p, wts, tabs_p, li=li, tm=512, with_kv=True, per_row=False)
        ssd_out, h_t = _ssd_prompt(xbc, z, dt, wts)
        mla_out = _attn_prompt(q, k, v, wts["g_mla"], tq=512)
        pool_out = _pool_prompt(u, wts["w_pool"], wts["pool_scale"])
        yp = _ffn(yp, ssd_out, mla_out, pool_out, mod_p, wts, li=li, tm=512, per_row=False)
        for lst, val in zip(p_new, (lat, kpe, h_t.reshape(bp, NH, HD, SSD_N), xbc[:, seq - (CONV_K - 1):],
                                    u[:, seq - POOL_BUF:])):
            lst.append(val)

        z, xbc, dt, u, lat, kpe, q, kpe128 = _inproj(ys, mod_s, wts, tabs_s, li=li, tm=bs, with_kv=False, per_row=True)
        z, xbc, dt, u, lat, kpe, q, kpe128 = (a[0] for a in (z, xbc, dt, u, lat, kpe, q, kpe128))
        conv_buf = state_conv[li]
        xs, bm, cm, xdt_t, da = _ssd_sample_pre(xbc, jnp.transpose(conv_buf, (1, 0, 2)), dt, wts)
        h_new, y_t = _ssd_sample_state(da[:, :8], state_all, xdt_t, bm, cm, li=li)
        ssd_out = _ssd_sample_post(y_t.T, xs, z, wts)
        qabs, qpe, snew = _mla_sample_pre(q, lat, kpe128, wts)
        qabs = jnp.pad(qabs.reshape(bs, NH, LANE), ((0, 0), (0, HPAD - NH), (0, 0))).astype(bf16)
        qpe = jnp.pad(qpe.reshape(bs, NH, LANE)[:, :, :ROPE], ((0, 0), (0, HPAD - NH), (0, 0))).astype(bf16)
        snew_b = jnp.broadcast_to(jnp.pad(snew[:, :NH], ((0, 0), (0, HPAD - NH)))[:, :, None], (bs, HPAD, LANE))
        o_lat = _attn_sample(page_table, cache_kv_latent, cache_kpe_t, wts["w_kt_flat"], qabs, qpe, snew_b,
                             lat.reshape(bs, 1, KV_LORA), li=li)
        mla_out = _mla_sample_post(o_lat[:, :NH].reshape(bs, NH * KV_LORA), wts["w_v_bd"], wts["g_mla"])
        pool_buf = state_pool[li]
        pool_out = _pool_sample(u, jnp.transpose(pool_buf, (1, 0, 2)), wts["w_pool"], wts["pool_scale"])
        ys = _ffn(ys, ssd_out[None], mla_out[None], pool_out[None], mod_s, wts, li=li, tm=bs, per_row=True)
        conv_new = jnp.concatenate([conv_buf[:, 1:], xbc[:, None, :]], axis=1)
        pool_new = jnp.concatenate([pool_buf[:, 1:], u[:, None, :]], axis=1)
        for lst, val in zip(s_new, (lat[:, None, :], kpe[:, None, :], h_new.reshape(bs, NH, HD, SSD_N), conv_new, pool_new)):
            lst.append(val)

    outs_p = [jnp.stack(vv, axis=0) for vv in p_new]
    outs_s = [jnp.stack(vv, axis=0) for vv in s_new]
    return (yp, ys.reshape(bs, 1, d), *outs_p, *outs_s)
```

```python
import functools
import math

import jax
import jax.numpy as jnp
import numpy as np
from jax import lax
from jax.experimental import pallas as pl
from jax.experimental.pallas import tpu as pltpu

f32 = jnp.float32
bf16 = jnp.bfloat16
HIGHEST = lax.Precision.HIGHEST

EPS = 1e-6
PAGE = 128
NH = 6
HD = 64
NOPE = 64
ROPE = 32
QK = NOPE + ROPE
SSD_W = NH * HD
SSD_G = 2
SSD_N = 128
CONV_K = 4
CONV_CH = SSD_W + 2 * SSD_G * SSD_N
Q_LORA = 256
KV_LORA = 128
POOL_W = 256
POOL_BUF = 15
ROPE_THETA = 10000.0
ATTN_SCALE = QK ** -0.5
PROMPT_Q_SCALE = ATTN_SCALE * math.log2(math.e)
LANE = 128
HPAD = 16
VMEM_LIMIT = 56 * 1024 * 1024

_C_Z, _C_XBC, _C_CQ, _C_CKV, _C_U, _C_KA, _C_KB, _C_DT, _C_END = 0, 384, 1280, 1536, 1664, 1920, 2048, 2176, 2304


def _cparams(sem):
    return pltpu.CompilerParams(dimension_semantics=sem, vmem_limit_bytes=VMEM_LIMIT)


def _silu(x):
    return x * jax.nn.sigmoid(x)


def _softplus(x):
    return jnp.maximum(x, 0.0) + jnp.log1p(jnp.exp(-jnp.abs(x)))


def _rms(x, g, n):
    ms = jnp.sum(x * x, axis=-1, keepdims=True) * (1.0 / n)
    return x * lax.rsqrt(ms + EPS) * g


def _dot(a, b):
    return jnp.dot(a, b, preferred_element_type=f32)


def _dot_nt(a, b):
    return lax.dot_general(a, b, (((1,), (1,)), ((), ())), preferred_element_type=f32)


def _dot_tn(a, b):
    return lax.dot_general(a, b, (((0,), (0,)), ((), ())), preferred_element_type=f32)


def _dot_exact(a, b):
    return jnp.dot(a, b, precision=HIGHEST, preferred_element_type=f32)


def _head_expand():
    r = lax.broadcasted_iota(jnp.int32, (LANE, SSD_W), 0)
    c = lax.broadcasted_iota(jnp.int32, (LANE, SSD_W), 1)
    return jnp.where(c // HD == r, 1.0, 0.0).astype(bf16)


def _split3(a):
    hi = a.astype(bf16)
    r = a - hi.astype(f32)
    mid = r.astype(bf16)
    return hi, mid, (r - mid.astype(f32)).astype(bf16)


def _dot_sel_right(a, sel):
    hi, mid, lo = _split3(a)
    return _dot(hi, sel) + _dot(mid, sel) + _dot(lo, sel)


def _dot_sel_left(sel, a):
    hi, mid, lo = _split3(a)
    return _dot(sel, hi) + _dot(sel, mid) + _dot(sel, lo)


def _ada_kernel(c_ref, w_ref, b_ref, op_ref, os_ref):
    s = _silu(c_ref[...]).astype(bf16)
    r = _dot(s, w_ref[0].astype(bf16)) + b_ref[0]
    rp = op_ref.shape[2]
    op_ref[0, 0] = r[:rp]
    os_ref[0, 0] = r[rp:]


def _ada_mod(c_prompt, c_sample, w_ada, b_ada):
    depth, d, n6 = w_ada.shape
    nf = n6 // d
    bp, bs = c_prompt.shape[0], c_sample.shape[0]
    rp = -(-bp // 8) * 8
    c_all = jnp.concatenate([c_prompt, jnp.zeros((rp - bp, d), f32), c_sample], axis=0)
    return pl.pallas_call(
        _ada_kernel,
        grid=(depth, nf),
        in_specs=[pl.BlockSpec((rp + bs, d), lambda l, j: (0, 0)),
                  pl.BlockSpec((1, d, d), lambda l, j: (l, 0, j)),
                  pl.BlockSpec((1, 1, d), lambda l, j: (l, 0, j))],
        out_specs=[pl.BlockSpec((1, 1, rp, d), lambda l, j: (l, j, 0, 0)),
                   pl.BlockSpec((1, 1, bs, d), lambda l, j: (l, j, 0, 0))],
        out_shape=[jax.ShapeDtypeStruct((depth, nf, rp, d), f32), jax.ShapeDtypeStruct((depth, nf, bs, d), f32)],
        compiler_params=_cparams(("arbitrary", "arbitrary")),
        name="ada_mod",
    )(c_all, w_ada, b_ada.reshape(depth, 1, n6))


def _mod_rows(mod_ref, field, per_row, tm):
    if per_row:
        return mod_ref[0, field, pl.ds(pl.multiple_of(pl.program_id(1) * tm, tm), tm), :]
    return mod_ref[0, field, pl.ds(pl.program_id(0), 1), :]


def _inproj_kernel(*refs, with_kv, per_row):
    (x_ref, mod_ref, g1_ref, w_ref, gq_ref, wq_ref, cosq_ref, sinq_ref, gqk_ref, gkv_ref,
     cosk_ref, sink_ref) = refs[:12]
    if with_kv:
        wk_ref, gk_ref, wv_ref = refs[12:15]
        z_ref, xbc_ref, dt_ref, u_ref, lat_ref, kpe_ref, q_ref, k_ref, v_ref = refs[15:]
    else:
        z_ref, xbc_ref, dt_ref, u_ref, lat_ref, kpe_ref, q_ref, kpe128_ref = refs[12:]
    tm, d = x_ref.shape[1:]
    shift, scale = (_mod_rows(mod_ref, f, per_row, tm) for f in (0, 1))
    gqk = gqk_ref[...]
    q_scale = PROMPT_Q_SCALE if with_kv else 1.0
    nsplit = 2 if tm % 512 == 0 else 1
    hm = tm // nsplit

    def rows_of(a, r):
        return a if a.shape[0] == 1 else a[r:r + hm]

    def project(r):
        h = _rms(x_ref[0, r:r + hm, :], g1_ref[...], d) * (1.0 + rows_of(scale, r)) + rows_of(shift, r)
        return _dot(h.astype(bf16), w_ref[0])

    def finish(r, proj):
        rs = slice(r, r + hm)
        z_ref[0, rs, :] = proj[:, _C_Z:_C_XBC]
        xbc_ref[0, rs, :] = proj[:, _C_XBC:_C_CQ]
        u_ref[0, rs, :] = proj[:, _C_U:_C_KA]
        dt_ref[0, rs, :] = proj[:, _C_DT:_C_END]
        cqn = _rms(proj[:, _C_CQ:_C_CKV], gq_ref[...], Q_LORA).astype(bf16)
        qq = _dot(cqn, wq_ref[0])
        cosq = rows_of(cosq_ref[...], r)
        sinq = rows_of(sinq_ref[...], r)
        for hh in range(NH):
            qh = qq[:, LANE * hh:LANE * (hh + 1)] * cosq + qq[:, NH * LANE + LANE * hh:NH * LANE + LANE * (hh + 1)] * sinq
            q_ref[0, rs, LANE * hh:LANE * (hh + 1)] = (_rms(qh, gqk, QK) * q_scale).astype(q_ref.dtype)
        lat = _rms(proj[:, _C_CKV:_C_U], gkv_ref[...], KV_LORA)
        lat_ref[0, rs, :] = lat
        kper = proj[:, _C_KA:_C_KB] * rows_of(cosk_ref[...], r) + proj[:, _C_KB:_C_DT] * rows_of(sink_ref[...], r)
        kpe_ref[0, rs, :] = kper[:, :ROPE]
        if with_kv:
            latb = lat.astype(bf16)
            kn = _dot(latb, wk_ref[...])
            kp = pltpu.roll(kper, NOPE, 1)
            gk = gk_ref[...]
            for hh in range(NH):
                kh = kn[:, LANE * hh:LANE * (hh + 1)] + kp
                k_ref[0, rs, LANE * hh:LANE * (hh + 1)] = _rms(kh, gk, QK).astype(bf16)
            v_ref[0, :, rs] = _dot_nt(wv_ref[...], latb).astype(bf16)
        else:
            kpe128_ref[0, rs, :] = kper

    nxt = project(0)
    for i in range(nsplit):
        proj = nxt
        if i + 1 < nsplit:
            nxt = project((i + 1) * hm)
        finish(i * hm, proj)


def _inproj(x, mod, wts, tabs, *, li, tm, with_kv, per_row):
    g_, t_, d = x.shape
    nt = t_ // tm
    cosq, sinq, cosk, sink = tabs
    tab_rows = cosq.shape[0] != 1
    tab_spec = pl.BlockSpec((tm if tab_rows else 1, LANE), (lambda g, t: (t, 0)) if tab_rows else (lambda g, t: (0, 0)))

    def full(a):
        return pl.BlockSpec(a.shape, lambda g, t: (0,) * a.ndim)

    def layer(a):
        return pl.BlockSpec((1,) + a.shape[1:], lambda g, t: (li,) + (0,) * (a.ndim - 1))

    def row(c):
        return pl.BlockSpec((1, tm, c), lambda g, t: (g, t, 0))

    ins = [x, mod, wts["g1"], wts["w_in"], wts["g_q"], wts["w_q"], cosq, sinq, wts["g_qk_q"], wts["g_kv"], cosk, sink]
    specs = [row(d), layer(mod), full(wts["g1"]), layer(wts["w_in"]), full(wts["g_q"]), layer(wts["w_q"]), tab_spec,
             tab_spec, full(wts["g_qk_q"]), full(wts["g_kv"]), tab_spec, tab_spec]
    widths = [SSD_W, CONV_CH, LANE, POOL_W, KV_LORA, ROPE, NH * LANE]
    dtypes = [f32, f32, f32, f32, f32, f32, bf16 if with_kv else f32]
    if with_kv:
        ins += [wts["w_k"], wts["g_qk_k"], wts["w_v"]]
        specs += [full(wts["w_k"]), full(wts["g_qk_k"]), full(wts["w_v"])]
        widths += [NH * LANE, SSD_W]
        dtypes += [bf16, bf16]
    else:
        widths += [LANE]
        dtypes += [f32]
    out_specs = [row(c) for c in widths]
    out_shape = [jax.ShapeDtypeStruct((g_, t_, c), dt) for c, dt in zip(widths, dtypes)]
    if with_kv:
        out_specs[-1] = pl.BlockSpec((1, SSD_W, tm), lambda g, t: (g, 0, t))
        out_shape[-1] = jax.ShapeDtypeStruct((g_, SSD_W, t_), bf16)
    return pl.pallas_call(
        functools.partial(_inproj_kernel, with_kv=with_kv, per_row=per_row),
        grid=(g_, nt),
        in_specs=specs,
        out_specs=out_specs,
        out_shape=out_shape,
        compiler_params=_cparams(("arbitrary", "arbitrary")),
        name="inproj_kv" if with_kv else "inproj",
    )(*ins)


def _ssd_prompt_kernel(xbc_ref, z_ref, dt_ref, cw_ref, cb_ref, dtb_ref, a_ref, dsk_ref, gn_ref, tril_ref, exp_ref,
                       y_ref, st_ref, ext_ref):
    c = pl.program_id(1)
    t_ = xbc_ref.shape[1]

    @pl.when(c == 0)
    def _():
        ext_ref[0:8, :] = jnp.zeros((8, CONV_CH), f32)
        st_ref[...] = jnp.zeros(st_ref.shape, f32)

    x = xbc_ref[0]
    ext_ref[8:8 + t_, :] = x
    cw = cw_ref[...]
    conv = cb_ref[...] + cw[3:4] * x
    for k in range(CONV_K - 1):
        conv = conv + cw[k:k + 1] * ext_ref[5 + k:5 + k + t_, :]
    ext_ref[0:8, :] = x[t_ - 8:, :]
    act = _silu(conv)
    xs = act[:, :SSD_W]
    bm = [act[:, SSD_W + SSD_N * g:SSD_W + SSD_N * (g + 1)].astype(bf16) for g in range(SSD_G)]
    cm = [act[:, SSD_W + SSD_N * (SSD_G + g):SSD_W + SSD_N * (SSD_G + g + 1)].astype(bf16) for g in range(SSD_G)]

    dt = _softplus(dt_ref[0] + dtb_ref[...])
    a = dt * a_ref[...]
    ri = lax.broadcasted_iota(jnp.int32, (t_, t_), 0)
    ci = lax.broadcasted_iota(jnp.int32, (t_, t_), 1)
    tril = ri >= ci
    acum = _dot_sel_left(tril_ref[...], a)
    acum_t = acum.T
    expand = exp_ref[...]
    dt_l = _dot_sel_right(dt, expand)
    acum_l = _dot_sel_right(acum, expand)
    last_l = acum_l[t_ - 1:t_, :]
    e_l = jnp.exp(acum_l)
    decay_l = jnp.exp(last_l - acum_l)
    xdt = xs * dt_l
    xdt_w = xdt * decay_l

    cb = [_dot_nt(cm[g], bm[g]) for g in range(SSD_G)]
    lane = lax.broadcasted_iota(jnp.int32, (t_, LANE), 1)
    rowi = lax.broadcasted_iota(jnp.int32, (LANE, SSD_N), 0)
    first = lane < HD
    rfirst = rowi < HD
    y_cols = []
    for j in range(NH // 2):
        sl = slice(LANE * j, LANE * (j + 1))
        h0, h1 = 2 * j, 2 * j + 1
        g0, g1 = h0 // (NH // SSD_G), h1 // (NH // SSD_G)
        xj = xdt[:, sl].astype(bf16)
        yd = []
        for hh, gg in ((h0, g0), (h1, g1)):
            seg = jnp.where(tril, jnp.exp(acum[:, hh:hh + 1] - acum_t[hh:hh + 1, :]), 0.0)
            yd.append(_dot((cb[gg] * seg).astype(bf16), xj))
        y_diag = jnp.where(first, yd[0], yd[1])
        hp = st_ref[0, sl, :]
        hpb = hp.astype(bf16)
        xw = xdt_w[:, sl].astype(bf16)
        if g0 == g1:
            y_off = _dot_nt(cm[g0], hpb)
            s_new = _dot_tn(xw, bm[g0])
        else:
            y_off = jnp.where(first, _dot_nt(cm[g0], hpb), _dot_nt(cm[g1], hpb))
            s_new = jnp.where(rfirst, _dot_tn(xw, bm[g0]), _dot_tn(xw, bm[g1]))
        tot = jnp.where(rfirst, jnp.exp(acum[t_ - 1:t_, h0:h0 + 1]), jnp.exp(acum[t_ - 1:t_, h1:h1 + 1]))
        st_ref[0, sl, :] = tot * hp + s_new
        y_cols.append(y_diag + y_off * e_l[:, sl])
    y = jnp.concatenate(y_cols, axis=-1) + dsk_ref[...] * xs
    v = y * _silu(z_ref[0])
    gl = lax.broadcasted_iota(jnp.int32, (t_, SSD_W), 1) < SSD_W // SSD_G
    v2 = v * v
    ss0 = jnp.sum(jnp.where(gl, v2, 0.0), axis=-1, keepdims=True)
    ss1 = jnp.sum(jnp.where(gl, 0.0, v2), axis=-1, keepdims=True)
    gw = SSD_W // SSD_G
    rinv = jnp.where(gl, lax.rsqrt(ss0 * (1.0 / gw) + EPS), lax.rsqrt(ss1 * (1.0 / gw) + EPS))
    y_ref[0] = v * rinv * gn_ref[...]


def _ssd_prompt(xbc, z, dt, wts):
    b_, l_, _ = xbc.shape
    t_ = 128
    nc = l_ // t_

    def full(a):
        return pl.BlockSpec(a.shape, lambda b, c: (0,) * a.ndim)

    def row(w):
        return pl.BlockSpec((1, t_, w), lambda b, c: (b, c, 0))

    small = [wts["conv_w"], wts["conv_b"], wts["dt_bias"], wts["a_neg"], wts["d_skip_l"], wts["g_ssd"],
             jnp.tril(jnp.ones((t_, t_), bf16)), _head_expand()]
    return pl.pallas_call(
        _ssd_prompt_kernel,
        grid=(b_, nc),
        in_specs=[row(CONV_CH), row(SSD_W), row(LANE)] + [full(a) for a in small],
        out_specs=[row(SSD_W), pl.BlockSpec((1, SSD_W, SSD_N), lambda b, c: (b, 0, 0))],
        out_shape=[jax.ShapeDtypeStruct((b_, l_, SSD_W), f32), jax.ShapeDtypeStruct((b_, SSD_W, SSD_N), f32)],
        scratch_shapes=[pltpu.VMEM((8 + t_, CONV_CH), f32)],
        compiler_params=_cparams(("arbitrary", "arbitrary")),
        name="ssd_prompt",
    )(xbc, z, dt, *small)


def _attn_prompt_kernel(q_ref, k_ref, vt_ref, g_ref, o_ref, m_ref, l_ref, acc_ref, *, tq):
    qi = pl.program_id(1)
    m_ref[...] = jnp.full(m_ref.shape, -1e30, f32)
    l_ref[...] = jnp.zeros(l_ref.shape, f32)
    acc_ref[...] = jnp.zeros(acc_ref.shape, f32)

    def tile(key0, nk, q0, nq, diagonal):
        kblk = k_ref[0, pl.ds(key0, nk), :]
        vblk = vt_ref[0, :, pl.ds(key0, nk)]
        qs = slice(q0, q0 + nq)
        ones = jnp.ones((16, nk), bf16)
        if diagonal:
            masked = lax.broadcasted_iota(jnp.int32, (nk, nq), 0) > lax.broadcasted_iota(jnp.int32, (nk, nq), 1)

        def qk(hh):
            return _dot_nt(kblk[:, LANE * hh:LANE * (hh + 1)], q_ref[0, qs, LANE * hh:LANE * (hh + 1)])

        ahead = 3
        pending = [qk(hh) for hh in range(ahead)]
        for hh in range(NH):
            s = pending.pop(0)
            if hh + ahead < NH:
                pending.append(qk(hh + ahead))
            if diagonal:
                s = jnp.where(masked, -1e30, s)
            m_old = m_ref[hh, :, qs]
            m_new = jnp.maximum(m_old, jnp.max(s, axis=0, keepdims=True))
            alpha = jnp.exp2(m_old - m_new)
            pe = jnp.exp2(s - m_new).astype(bf16)
            m_ref[hh, :, qs] = m_new
            rows = slice(HD * hh, HD * (hh + 1))
            pv = _dot(jnp.concatenate([vblk[rows, :], ones], axis=0), pe)
            acc_ref[rows, qs] = acc_ref[rows, qs] * alpha + pv[:HD]
            l_ref[hh, :, qs] = alpha * l_ref[hh, :, qs] + pv[HD:HD + 8]

    def body(j, carry):
        tile(pl.multiple_of(j * tq, tq), tq, 0, tq, False)
        return carry

    lax.fori_loop(0, qi, body, 0)
    tile(pl.multiple_of(qi * tq, tq), tq, 0, tq, True)
    parts = []
    for hh in range(NH):
        parts.append(acc_ref[HD * hh:HD * (hh + 1), :] * (1.0 / l_ref[hh, 0:1, :]))
    o_t = jnp.concatenate(parts, axis=0)
    ms = jnp.sum(o_t * o_t, axis=0, keepdims=True) * (1.0 / SSD_W)
    o_ref[0] = (o_t * lax.rsqrt(ms + EPS)).T * g_ref[...]


def _attn_prompt(q, k, v_t, g_out, *, tq):
    b_, l_, _ = q.shape
    return pl.pallas_call(
        functools.partial(_attn_prompt_kernel, tq=tq),
        grid=(b_, l_ // tq),
        in_specs=[pl.BlockSpec((1, tq, NH * LANE), lambda b, i: (b, i, 0)),
                  pl.BlockSpec((1, l_, NH * LANE), lambda b, i: (b, 0, 0)),
                  pl.BlockSpec((1, SSD_W, l_), lambda b, i: (b, 0, 0)),
                  pl.BlockSpec(g_out.shape, lambda b, i: (0, 0))],
        out_specs=pl.BlockSpec((1, tq, SSD_W), lambda b, i: (b, i, 0)),
        out_shape=jax.ShapeDtypeStruct((b_, l_, SSD_W), f32),
        scratch_shapes=[pltpu.VMEM((NH, 1, tq), f32), pltpu.VMEM((NH, 8, tq), f32), pltpu.VMEM((SSD_W, tq), f32)],
        compiler_params=_cparams(("arbitrary", "arbitrary")),
        name="attn_prompt",
    )(q, k, v_t, g_out)


def _pool_select(lane, a, b, c, d):
    return jnp.where(lane < 64, a, jnp.where(lane < 128, b, jnp.where(lane < 192, c, d)))


def _pool_prompt_kernel(u_ref, w_ref, sc_ref, o_ref, e1, e2, e4, e8, *, rt):
    t_ = u_ref.shape[1]
    hist = 16
    for e in (e1, e2, e4, e8):
        e[0:hist, :] = jnp.zeros((hist, POOL_W), f32)
    e1[hist:hist + t_, :] = u_ref[0]
    lane = lax.broadcasted_iota(jnp.int32, (rt, POOL_W), 1)
    win = _pool_select(lane, 2, 4, 8, 16)
    for i in range(t_ // rt):
        r0 = hist + i * rt
        a = e1[r0:r0 + rt, :]
        s2 = a + e1[r0 - 1:r0 - 1 + rt, :]
        e2[r0:r0 + rt, :] = s2
        s4 = s2 + e2[r0 - 2:r0 - 2 + rt, :]
        e4[r0:r0 + rt, :] = s4
        s8 = s4 + e4[r0 - 4:r0 - 4 + rt, :]
        e8[r0:r0 + rt, :] = s8
        s16 = s8 + e8[r0 - 8:r0 - 8 + rt, :]
        pos = lax.broadcasted_iota(jnp.int32, (rt, POOL_W), 0) + i * rt
        cnt = jnp.minimum(pos + 1, win).astype(f32)
        pooled = _pool_select(lane, s2, s4, s8, s16) / cnt - a
        o_ref[0, i * rt:(i + 1) * rt, :] = _dot(pooled.astype(bf16), w_ref[...]) * sc_ref[...]


def _pool_prompt(u, w_bd, scale):
    b_, l_, _ = u.shape
    return pl.pallas_call(
        functools.partial(_pool_prompt_kernel, rt=256),
        grid=(b_,),
        in_specs=[pl.BlockSpec((1, l_, POOL_W), lambda b: (b, 0, 0)),
                  pl.BlockSpec(w_bd.shape, lambda b: (0, 0)),
                  pl.BlockSpec(scale.shape, lambda b: (0, 0))],
        out_specs=pl.BlockSpec((1, l_, POOL_W), lambda b: (b, 0, 0)),
        out_shape=jax.ShapeDtypeStruct((b_, l_, POOL_W), f32),
        scratch_shapes=[pltpu.VMEM((16 + l_, POOL_W), f32)] * 4,
        compiler_params=_cparams(("arbitrary",)),
        name="pool_prompt",
    )(u, w_bd, scale)


def _pool_sample_kernel(u_ref, buf_ref, w_ref, sc_ref, o_ref):
    u = u_ref[...]
    run = u
    sums = {}
    for i in range(1, 16):
        run = run + buf_ref[POOL_BUF - i]
        if i + 1 in (2, 4, 8, 16):
            sums[i + 1] = run * (1.0 / (i + 1))
    lane = lax.broadcasted_iota(jnp.int32, u.shape, 1)
    pooled = _pool_select(lane, sums[2], sums[4], sums[8], sums[16]) - u
    o_ref[...] = _dot(pooled.astype(bf16), w_ref[...]) * sc_ref[...]


def _pool_sample(u, buf_t, w_bd, scale):
    return pl.pallas_call(
        _pool_sample_kernel,
        out_shape=jax.ShapeDtypeStruct(u.shape, f32),
        compiler_params=pltpu.CompilerParams(vmem_limit_bytes=VMEM_LIMIT),
        name="pool_sample",
    )(u, buf_t, w_bd, scale)


def _ffn_kernel(x_ref, ssd_ref, mla_ref, pool_ref, mod_ref, gn_ref, wo_ref, wg_ref, wu_ref, wd_ref, o_ref,
                *, nch, per_row):
    tm, d = x_ref.shape[1:]
    gate1, shift2, scale2, gate2 = (_mod_rows(mod_ref, f, per_row, tm) for f in (2, 3, 4, 5))
    hc = wg_ref.shape[2] // nch
    nsplit = 2 if tm % 512 == 0 else 1
    hm = tm // nsplit

    def rows_of(a, r):
        return a if a.shape[0] == 1 else a[r:r + hm]

    def out_proj(r):
        rs = slice(r, r + hm)
        mixed = jnp.concatenate([ssd_ref[0, rs, :].astype(bf16), mla_ref[0, rs, :].astype(bf16),
                                 pool_ref[0, rs, :].astype(bf16)], axis=-1)
        return _dot(mixed, wo_ref[0])

    def residual_norm(r, mix):
        x1 = x_ref[0, r:r + hm, :] + rows_of(gate1, r) * mix
        return x1, (_rms(x1, gn_ref[...], d) * (1.0 + rows_of(scale2, r)) + rows_of(shift2, r)).astype(bf16)

    def gate_up(h2, c):
        return _dot(h2, wg_ref[0, :, hc * c:hc * (c + 1)]), _dot(h2, wu_ref[0, :, hc * c:hc * (c + 1)])

    mixes = [out_proj(i * hm) for i in range(nsplit)]
    x1, h2 = residual_norm(0, mixes[0])
    for i in range(nsplit):
        r = i * hm
        acc = jnp.zeros(x1.shape, f32)
        nxt = gate_up(h2, 0)
        if i + 1 < nsplit:
            x1_next, h2_next = residual_norm(r + hm, mixes[i + 1])
        for c in range(nch):
            gate, up = nxt
            if c + 1 < nch:
                nxt = gate_up(h2, c + 1)
            acc = acc + _dot((_silu(gate) * up).astype(bf16), wd_ref[0, hc * c:hc * (c + 1), :])
        o_ref[0, r:r + hm, :] = x1 + rows_of(gate2, r) * acc
        if i + 1 < nsplit:
            x1, h2 = x1_next, h2_next


def _ffn(x, ssd, mla, pool, mod, wts, *, li, tm, per_row):
    g_, t_, d = x.shape

    def layer(a):
        return pl.BlockSpec((1,) + a.shape[1:], lambda g, t: (li,) + (0,) * (a.ndim - 1), pipeline_mode=pl.Buffered(1))

    def row(c):
        return pl.BlockSpec((1, tm, c), lambda g, t: (g, t, 0))

    ws = [wts["w_out"], wts["w_gate"], wts["w_up"], wts["w_down"]]
    return pl.pallas_call(
        functools.partial(_ffn_kernel, nch=11, per_row=per_row),
        grid=(g_, t_ // tm),
        in_specs=[row(d), row(SSD_W), row(SSD_W), row(POOL_W), layer(mod),
                  pl.BlockSpec(wts["g2"].shape, lambda g, t: (0, 0))] + [layer(a) for a in ws],
        out_specs=row(d),
        out_shape=jax.ShapeDtypeStruct((g_, t_, d), f32),
        compiler_params=_cparams(("arbitrary", "arbitrary")),
        name="outproj_ffn",
    )(x, ssd, mla, pool, mod, wts["g2"], *ws)


def _ssd_sample_pre_kernel(xbc_ref, buf_ref, dt_ref, cw_ref, cb_ref, dtb_ref, a_ref,
                           xs_ref, b_ref, c_ref, xdt_t_ref, da_ref):
    cw = cw_ref[...]
    conv = cb_ref[...] + cw[3:4] * xbc_ref[...]
    for k in range(CONV_K - 1):
        conv = conv + cw[k:k + 1] * buf_ref[k]
    act = _silu(conv)
    xs = act[:, :SSD_W]
    xs_ref[...] = xs
    b_ref[...] = act[:, SSD_W:SSD_W + SSD_G * SSD_N]
    c_ref[...] = act[:, SSD_W + SSD_G * SSD_N:]
    dt = _softplus(dt_ref[...] + dtb_ref[...])
    da_ref[...] = jnp.exp(dt * a_ref[...])
    xdt = xs * _dot_sel_right(dt, _head_expand())
    for j in range(SSD_W // LANE):
        xdt_t_ref[LANE * j:LANE * (j + 1), :] = xdt[:, LANE * j:LANE * (j + 1)].T


def _ssd_sample_pre(xbc, buf_t, dt, wts):
    b_ = xbc.shape[0]
    shapes = [(b_, SSD_W), (b_, SSD_G * SSD_N), (b_, SSD_G * SSD_N), (SSD_W, b_), (b_, LANE)]
    return pl.pallas_call(
        _ssd_sample_pre_kernel,
        out_shape=[jax.ShapeDtypeStruct(s, f32) for s in shapes],
        compiler_params=pltpu.CompilerParams(vmem_limit_bytes=VMEM_LIMIT),
        name="ssd_sample_pre",
    )(xbc, buf_t, dt, wts["conv_w"], wts["conv_b"], wts["dt_bias"], wts["a_neg"])


def _ssd_sample_state_kernel(da_ref, st_ref, xdt_t_ref, b_ref, c_ref, new_ref, y_t_ref, *, tb):
    i = pl.program_id(0)

    @pl.when(i == 0)
    def _():
        y_t_ref[...] = jnp.zeros(y_t_ref.shape, f32)

    nb = xdt_t_ref.shape[1]
    lane = lax.broadcasted_iota(jnp.int32, (HD, nb), 1)
    pairs = [(hh, bl) for hh in range(NH) for bl in range(tb)]
    sels = [lane == i * tb + bl for bl in range(tb)]
    cols = {}
    for hh, bl in pairs:
        cols[hh, bl] = jnp.sum(jnp.where(sels[bl], xdt_t_ref[HD * hh:HD * (hh + 1), :], 0.0), axis=1, keepdims=True)
    ycols = {}
    for hh, bl in pairs:
        g = hh // (NH // SSD_G)
        rs = slice(HD * hh, HD * (hh + 1))
        brow = b_ref[bl:bl + 1, SSD_N * g:SSD_N * (g + 1)]
        crow = c_ref[bl:bl + 1, SSD_N * g:SSD_N * (g + 1)]
        new = da_ref[i * tb + bl, hh] * st_ref[bl, rs, :] + cols[hh, bl] * brow
        new_ref[bl, rs, :] = new
        ycols[hh, bl] = jnp.sum(new * crow, axis=1, keepdims=True)
    for hh in range(NH):
        rs = slice(HD * hh, HD * (hh + 1))
        y_rows = y_t_ref[rs, :]
        for bl in range(tb):
            y_rows = jnp.where(sels[bl], ycols[hh, bl], y_rows)
        y_t_ref[rs, :] = y_rows


def _ssd_sample_state(da, state_all, xdt_t, bm, cm, *, li, tb=8):
    b_ = xdt_t.shape[1]
    nblk = b_ // tb
    return pl.pallas_call(
        functools.partial(_ssd_sample_state_kernel, tb=tb),
        grid=(nblk,),
        in_specs=[pl.BlockSpec(memory_space=pltpu.SMEM),
                  pl.BlockSpec((tb, SSD_W, SSD_N), lambda i: (li * nblk + i, 0, 0)),
                  pl.BlockSpec(xdt_t.shape, lambda i: (0, 0)),
                  pl.BlockSpec((tb, SSD_G * SSD_N), lambda i: (i, 0)),
                  pl.BlockSpec((tb, SSD_G * SSD_N), lambda i: (i, 0))],
        out_specs=[pl.BlockSpec((tb, SSD_W, SSD_N), lambda i: (i, 0, 0)),
                   pl.BlockSpec((SSD_W, b_), lambda i: (0, 0))],
        out_shape=[jax.ShapeDtypeStruct((b_, SSD_W, SSD_N), f32), jax.ShapeDtypeStruct((SSD_W, b_), f32)],
        compiler_params=_cparams(("arbitrary",)),
        name="ssd_sample_state",
    )(da, state_all, xdt_t, bm, cm)


def _ssd_sample_post_kernel(y_ref, xs_ref, z_ref, dsk_ref, gn_ref, o_ref):
    y = y_ref[...] + dsk_ref[...] * xs_ref[...]
    v = y * _silu(z_ref[...])
    gl = lax.broadcasted_iota(jnp.int32, v.shape, 1) < SSD_W // SSD_G
    v2 = v * v
    gw = SSD_W // SSD_G
    ss0 = jnp.sum(jnp.where(gl, v2, 0.0), axis=-1, keepdims=True)
    ss1 = jnp.sum(jnp.where(gl, 0.0, v2), axis=-1, keepdims=True)
    rinv = jnp.where(gl, lax.rsqrt(ss0 * (1.0 / gw) + EPS), lax.rsqrt(ss1 * (1.0 / gw) + EPS))
    o_ref[...] = v * rinv * gn_ref[...]


def _ssd_sample_post(y, xs, z, wts):
    return pl.pallas_call(
        _ssd_sample_post_kernel,
        out_shape=jax.ShapeDtypeStruct(y.shape, f32),
        name="ssd_sample_post",
    )(y, xs, z, wts["d_skip_l"], wts["g_ssd"])


def _mla_sample_pre_kernel(q_ref, lat_ref, kpe_ref, wk_ref, wkt_ref, gk_ref, qabs_ref, qpe_ref, snew_ref):
    kn = _dot(lat_ref[...].astype(bf16), wk_ref[...])
    kp = pltpu.roll(kpe_ref[...], NOPE, 1)
    gk = gk_ref[...]
    lane = lax.broadcasted_iota(jnp.int32, snew_ref.shape, 1)
    snew = jnp.zeros(snew_ref.shape, f32)
    for hh in range(NH):
        sl = slice(LANE * hh, LANE * (hh + 1))
        qh = q_ref[:, sl]
        knew = _rms(kn[:, sl] + kp, gk, QK)
        snew = jnp.where(lane == hh, jnp.sum(qh * knew, axis=-1, keepdims=True) * ATTN_SCALE, snew)
        qg = qh * gk
        qabs_ref[:, sl] = _dot_exact(qg, wkt_ref[hh])
        qpe_ref[:, sl] = pltpu.roll(qg, NOPE, 1)
    snew_ref[...] = snew


def _mla_sample_pre(q, lat, kpe128, wts):
    b_ = q.shape[0]
    shapes = [(b_, NH * LANE), (b_, NH * LANE), (b_, LANE)]
    return pl.pallas_call(
        _mla_sample_pre_kernel,
        out_shape=[jax.ShapeDtypeStruct(s, f32) for s in shapes],
        name="mla_sample_pre",
    )(q, lat, kpe128, wts["w_k"], wts["w_kt"], wts["g_qk_k"])


def _attn_sample_kernel(pt_ref, lat_hbm, kpe_hbm, wkt_ref, qabs_ref, qpe_ref, snew_ref, latnew_ref, o_ref,
                        lat_buf, kpe_buf, sem, wext, latb_ref, s_ref, *, li, pages, nchunk):
    b = pl.program_id(0)
    nb = pl.num_programs(0)
    rc = pages * PAGE
    slot = b % 2

    def aligned(idx, size):
        return idx * size if isinstance(idx, int) else pl.multiple_of(idx * size, size)

    def copies(bb, c, i, sl):
        page_idx = c * pages + i
        page = pt_ref[bb, page_idx]
        off = aligned(page_idx, PAGE)
        return (pltpu.make_async_copy(lat_hbm.at[li, page], lat_buf.at[sl, pl.ds(off, PAGE), :], sem.at[0, sl, c]),
                pltpu.make_async_copy(kpe_hbm.at[li, page], kpe_buf.at[sl, :, pl.ds(off, PAGE)], sem.at[1, sl, c]))

    def start_chunk(bb, c, sl):
        for i in range(pages):
            for cp in copies(bb, c, i, sl):
                cp.start()

    @pl.when(b == 0)
    def _():
        def issue(c, carry):
            start_chunk(0, c, 0)
            start_chunk(jnp.minimum(1, nb - 1), c, 1)
            return carry
        lax.fori_loop(0, nchunk, issue, 0)
        wext[0:NH * HD, :] = wkt_ref[...]

    wext[NH * HD:NH * HD + HPAD, :] = qabs_ref[0]
    qpe = qpe_ref[0]
    rowi = lax.broadcasted_iota(jnp.int32, (HPAD, rc), 0)

    def scores(c, carry):
        for i in range(pages):
            for cp in copies(b, c, i, slot):
                cp.wait()
        off = aligned(c, rc)
        latb = lat_buf[slot, pl.ds(off, rc), :].astype(bf16)
        latb_ref[pl.ds(off, rc), :] = latb
        kpe = kpe_buf[slot, :, pl.ds(off, rc)]
        a = _dot_nt(wext[...], latb)
        nsq = jnp.zeros((HPAD, rc), f32)
        for hh in range(NH):
            kh = a[HD * hh:HD * (hh + 1), :]
            nsq = jnp.where(rowi == hh, jnp.sum(kh * kh, axis=0, keepdims=True), nsq)
        s_pe = _dot(qpe, kpe.astype(bf16))
        ksq = jnp.sum(kpe * kpe, axis=0, keepdims=True)
        s_ref[:, pl.ds(off, rc)] = (a[NH * HD:, :] + s_pe) * lax.rsqrt((nsq + ksq) * (1.0 / QK) + EPS) * ATTN_SCALE
        return carry

    lax.fori_loop(0, nchunk, scores, 0)

    b_ahead = jnp.minimum(b + 2, nb - 1)
    for c in range(nchunk):
        start_chunk(b_ahead, c, slot)

    s = s_ref[...]
    s_new = snew_ref[0][:, 0:1]
    m = jnp.maximum(jnp.max(s, axis=-1, keepdims=True), s_new)
    pe = jnp.exp(s - m)
    p_new = jnp.exp(s_new - m)
    l = jnp.sum(pe, axis=-1, keepdims=True) + p_new
    acc = _dot(pe.astype(bf16), latb_ref[...]) + p_new * latnew_ref[0]
    o_ref[0] = acc / l

    @pl.when(b == nb - 1)
    def _():
        def drain(c, carry):
            for sl in range(2):
                for i in range(pages):
                    for cp in copies(b, c, i, sl):
                        cp.wait()
            return carry
        lax.fori_loop(0, nchunk, drain, 0)


def _attn_sample(page_table, cache_lat, cache_kpe_t, wkt, qabs, qpe, snew, lat_new, *, li, pages=64):
    b_, npages = page_table.shape
    nchunk = npages // pages
    assert npages % pages == 0
    seq = npages * PAGE
    grid_spec = pltpu.PrefetchScalarGridSpec(
        num_scalar_prefetch=1,
        grid=(b_,),
        in_specs=[pl.BlockSpec(memory_space=pl.ANY),
                  pl.BlockSpec(memory_space=pl.ANY),
                  pl.BlockSpec(wkt.shape, lambda b, pt: (0, 0)),
                  pl.BlockSpec((1, HPAD, KV_LORA), lambda b, pt: (b, 0, 0)),
                  pl.BlockSpec((1, HPAD, ROPE), lambda b, pt: (b, 0, 0)),
                  pl.BlockSpec((1, HPAD, LANE), lambda b, pt: (b, 0, 0)),
                  pl.BlockSpec((1, 1, KV_LORA), lambda b, pt: (b, 0, 0))],
        out_specs=pl.BlockSpec((1, HPAD, KV_LORA), lambda b, pt: (b, 0, 0)),
        scratch_shapes=[pltpu.VMEM((2, seq, KV_LORA), f32),
                        pltpu.VMEM((2, ROPE, seq), f32),
                        pltpu.SemaphoreType.DMA((2, 2, nchunk)),
                        pltpu.VMEM((NH * HD + HPAD, KV_LORA), bf16),
                        pltpu.VMEM((seq, KV_LORA), bf16),
                        pltpu.VMEM((HPAD, seq), f32)],
    )
    return pl.pallas_call(
        functools.partial(_attn_sample_kernel, li=li, pages=pages, nchunk=nchunk),
        grid_spec=grid_spec,
        out_shape=jax.ShapeDtypeStruct((b_, HPAD, KV_LORA), f32),
        compiler_params=_cparams(("arbitrary",)),
        name="attn_sample",
    )(page_table, cache_lat, cache_kpe_t, wkt, qabs, qpe, snew, lat_new)


def _mla_sample_post_kernel(o_ref, wv_ref, g_ref, out_ref):
    out_ref[...] = _rms(_dot(o_ref[...].astype(bf16), wv_ref[...]), g_ref[...], SSD_W)


def _mla_sample_post(o_lat, w_v_bd, g_out):
    return pl.pallas_call(
        _mla_sample_post_kernel,
        out_shape=jax.ShapeDtypeStruct((o_lat.shape[0], SSD_W), f32),
        name="mla_sample_post",
    )(o_lat, w_v_bd, g_out)


def _pad_heads(w, lo, hi):
    pad = [(0, 0)] * (w.ndim - 1) + [(lo, LANE - hi)]
    w = jnp.pad(w, pad)
    return w.reshape(w.shape[:-2] + (NH * LANE,))


def _rot_cols(w):
    half = ROPE // 2
    return jnp.concatenate([-w[..., half:], w[..., :half]], axis=-1)


def _pad_lanes(a, lo=0):
    return jnp.pad(a, [(0, 0)] * (a.ndim - 1) + [(lo, LANE - lo - a.shape[-1])])


def _stacked_weights(p):
    offs = np.cumsum([SSD_W, CONV_CH, NH, Q_LORA, KV_LORA, ROPE, POOL_W])
    wz, wxbc, wdt, wcq, wckv, wkpe, wu = jnp.split(p["w_in"], offs[:-1].tolist(), axis=2)
    w_in = jnp.concatenate([wz, wxbc, wcq, wckv, wu, _pad_lanes(wkpe), _pad_lanes(_rot_cols(wkpe)), _pad_lanes(wdt)],
                           axis=2).astype(bf16)
    wq = p["w_q_up"]
    w_q = jnp.concatenate([_pad_heads(wq, 0, QK), _pad_heads(_rot_cols(wq[..., NOPE:]), NOPE, QK)], axis=2).astype(bf16)
    return dict(w_in=w_in, w_q=w_q, w_out=p["w_out"].astype(bf16), w_gate=p["w_gate"].astype(bf16),
                w_up=p["w_up"].astype(bf16), w_down=p["w_down"].astype(bf16))


def _layer_weights(p, li):
    d = p["w_in"].shape[1]
    wk = p["w_k_up"][li]
    wk_pad = _pad_heads(wk, 0, NOPE)
    wk_t = jnp.transpose(wk, (1, 2, 0))
    wv = p["w_v_up"][li]
    w_v_bd = jnp.zeros((NH, KV_LORA, NH, HD), f32)
    w_v_bd = w_v_bd.at[jnp.arange(NH), :, jnp.arange(NH), :].set(jnp.transpose(wv, (1, 0, 2)))
    wp = p["w_pool"][li]
    ng = wp.shape[0]
    w_pool_bd = jnp.zeros((ng, HD, ng, HD), f32).at[jnp.arange(ng), :, jnp.arange(ng), :].set(wp)
    return dict(
        g1=p["g_norm1"][li].reshape(1, d),
        g_q=p["g_q_lora"][li].reshape(1, Q_LORA),
        g_qk_q=_pad_lanes(p["g_qk_q"][li].reshape(1, QK)), g_qk_k=_pad_lanes(p["g_qk_k"][li].reshape(1, QK)),
        g_kv=p["g_kv_lora"][li].reshape(1, KV_LORA),
        w_k=wk_pad.astype(bf16), w_v=wv.reshape(KV_LORA, NH * HD).T.astype(bf16),
        w_kt=jnp.pad(wk_t, ((0, 0), (0, LANE - NOPE), (0, 0))),
        w_kt_flat=wk_t.reshape(NH * NOPE, KV_LORA).astype(bf16),
        w_v_bd=w_v_bd.reshape(NH * KV_LORA, NH * HD).astype(bf16),
        g_mla=p["g_mla_out"][li].reshape(1, SSD_W),
        conv_w=p["conv_w"][li], conv_b=p["conv_b"][li].reshape(1, CONV_CH),
        dt_bias=_pad_lanes(p["dt_bias"][li].reshape(1, NH)),
        a_neg=_pad_lanes(-jnp.exp(p["a_log"][li].astype(f32)).reshape(1, NH)),
        d_skip_l=jnp.repeat(p["d_skip"][li], HD).reshape(1, SSD_W), g_ssd=p["g_ssd_norm"][li].reshape(1, SSD_W),
        w_pool=w_pool_bd.reshape(POOL_W, POOL_W).astype(bf16), pool_scale=p["pool_scale"][li].reshape(1, POOL_W),
        g2=p["g_norm2"][li].reshape(1, d),
    )


def _rope_tables(pos):
    half = ROPE // 2
    inv = 1.0 / (ROPE_THETA ** (jnp.arange(half, dtype=f32) / half))
    ang = pos.astype(f32)[:, None] * inv[None, :]
    cos2 = jnp.concatenate([jnp.cos(ang)] * 2, axis=-1)
    sin2 = jnp.concatenate([jnp.sin(ang)] * 2, axis=-1)
    n = pos.shape[0]
    cosq = jnp.concatenate([jnp.ones((n, NOPE), f32), cos2, jnp.zeros((n, LANE - QK), f32)], axis=-1)
    sinq = _pad_lanes(sin2, NOPE)
    return cosq, sinq, _pad_lanes(cos2), _pad_lanes(sin2)


def kernel(x_prompt, x_sample, cache_kv_latent, cache_k_rope, state_ssm, state_conv, state_pool, page_table, c_prompt, c_sample, w_ada, b_ada, g_norm1, w_in, conv_w, conv_b, dt_bias, a_log, d_skip, g_ssd_norm, g_q_lora, w_q_up, g_kv_lora, w_k_up, w_v_up, g_qk_q, g_qk_k, g_mla_out, w_pool, pool_scale, w_out, g_norm2, w_gate, w_up, w_down):
    params = dict(g_norm1=g_norm1, w_in=w_in, conv_w=conv_w, conv_b=conv_b, dt_bias=dt_bias, a_log=a_log, d_skip=d_skip,
                  g_ssd_norm=g_ssd_norm, g_q_lora=g_q_lora, w_q_up=w_q_up, g_kv_lora=g_kv_lora, w_k_up=w_k_up,
                  w_v_up=w_v_up, g_qk_q=g_qk_q, g_qk_k=g_qk_k, g_mla_out=g_mla_out, w_pool=w_pool, pool_scale=pool_scale,
                  w_out=w_out, g_norm2=g_norm2, w_gate=w_gate, w_up=w_up, w_down=w_down)
    depth = w_ada.shape[0]
    bp, seq, d = x_prompt.shape
    bs = x_sample.shape[0]
    past = page_table.shape[1] * PAGE

    mod_p, mod_s = _ada_mod(c_prompt, c_sample, w_ada, b_ada)
    big = _stacked_weights(params)
    state_all = state_ssm.reshape(depth * bs, SSD_W, SSD_N)
    tabs_p = _rope_tables(jnp.arange(seq, dtype=jnp.int32))
    tabs_s = _rope_tables(jnp.full((1,), past, jnp.int32))
    cache_kpe_t = jnp.swapaxes(cache_k_rope, 2, 3)

    yp = x_prompt
    ys = x_sample.reshape(1, bs, d)
    p_new = [[] for _ in range(5)]
    s_new = [[] for _ in range(5)]
    for li in range(depth):
        wts = dict(_layer_weights(params, li), **big)

        z, xbc, dt, u, lat, kpe, q, k, v = _inproj(yp, mod_p, wts, tabs_p, li=li, tm=512, with_kv=True, per_row=False)
        ssd_out, h_t = _ssd_prompt(xbc, z, dt, wts)
        mla_out = _attn_prompt(q, k, v, wts["g_mla"], tq=512)
        pool_out = _pool_prompt(u, wts["w_pool"], wts["pool_scale"])
        yp = _ffn(yp, ssd_out, mla_out, pool_out, mod_p, wts, li=li, tm=512, per_row=False)
        for lst, val in zip(p_new, (lat, kpe, h_t.reshape(bp, NH, HD, SSD_N), xbc[:, seq - (CONV_K - 1):],
                                    u[:, seq - POOL_BUF:])):
            lst.append(val)

        z, xbc, dt, u, lat, kpe, q, kpe128 = _inproj(ys, mod_s, wts, tabs_s, li=li, tm=bs, with_kv=False, per_row=True)
        z, xbc, dt, u, lat, kpe, q, kpe128 = (a[0] for a in (z, xbc, dt, u, lat, kpe, q, kpe128))
        conv_buf = state_conv[li]
        xs, bm, cm, xdt_t, da = _ssd_sample_pre(xbc, jnp.transpose(conv_buf, (1, 0, 2)), dt, wts)
        h_new, y_t = _ssd_sample_state(da[:, :8], state_all, xdt_t, bm, cm, li=li)
        ssd_out = _ssd_sample_post(y_t.T, xs, z, wts)
        qabs, qpe, snew = _mla_sample_pre(q, lat, kpe128, wts)
        qabs = jnp.pad(qabs.reshape(bs, NH, LANE), ((0, 0), (0, HPAD - NH), (0, 0))).astype(bf16)
        qpe = jnp.pad(qpe.reshape(bs, NH, LANE)[:, :, :ROPE], ((0, 0), (0, HPAD - NH), (0, 0))).astype(bf16)
        snew_b = jnp.broadcast_to(jnp.pad(snew[:, :NH], ((0, 0), (0, HPAD - NH)))[:, :, None], (bs, HPAD, LANE))
        o_lat = _attn_sample(page_table, cache_kv_latent, cache_kpe_t, wts["w_kt_flat"], qabs, qpe, snew_b,
                             lat.reshape(bs, 1, KV_LORA), li=li)
        mla_out = _mla_sample_post(o_lat[:, :NH].reshape(bs, NH * KV_LORA), wts["w_v_bd"], wts["g_mla"])
        pool_buf = state_pool[li]
        pool_out = _pool_sample(u, jnp.transpose(pool_buf, (1, 0, 2)), wts["w_pool"], wts["pool_scale"])
        ys = _ffn(ys, ssd_out[None], mla_out[None], pool_out[None], mod_s, wts, li=li, tm=bs, per_row=True)
        conv_new = jnp.concatenate([conv_buf[:, 1:], xbc[:, None, :]], axis=1)
        pool_new = jnp.concatenate([pool_buf[:, 1:], u[:, None, :]], axis=1)
        for lst, val in zip(s_new, (lat[:, None, :], kpe[:, None, :], h_new.reshape(bs, NH, HD, SSD_N), conv_new, pool_new)):
            lst.append(val)

    outs_p = [jnp.stack(vv, axis=0) for vv in p_new]
    outs_s = [jnp.stack(vv, axis=0) for vv in s_new]
    return (yp, ys.reshape(bs, 1, d), *outs_p, *outs_s)
```

```python
import functools
import math

import jax
import jax.numpy as jnp
import numpy as np
from jax import lax
from jax.experimental import pallas as pl
from jax.experimental.pallas import tpu as pltpu

f32 = jnp.float32
bf16 = jnp.bfloat16
HIGHEST = lax.Precision.HIGHEST

EPS = 1e-6
PAGE = 128
NH = 6
HD = 64
NOPE = 64
ROPE = 32
QK = NOPE + ROPE
SSD_W = NH * HD
SSD_G = 2
SSD_N = 128
SSD_CHUNK = 128
CONV_K = 4
CONV_CH = SSD_W + 2 * SSD_G * SSD_N
Q_LORA = 256
KV_LORA = 128
POOL_W = 256
POOL_BUF = 15
ROPE_THETA = 10000.0
ATTN_SCALE = QK ** -0.5
PROMPT_Q_SCALE = ATTN_SCALE * math.log2(math.e)
LANE = 128
HPAD = 16
VMEM_LIMIT = 56 * 1024 * 1024

_C_Z, _C_XBC, _C_CQ, _C_CKV, _C_U, _C_KA, _C_KB, _C_DT, _C_END = 0, 384, 1280, 1536, 1664, 1920, 2048, 2176, 2304


def _cparams(sem):
    return pltpu.CompilerParams(dimension_semantics=sem, vmem_limit_bytes=VMEM_LIMIT)


def _silu(x):
    return x * jax.nn.sigmoid(x)


def _softplus(x):
    return jnp.maximum(x, 0.0) + jnp.log1p(jnp.exp(-jnp.abs(x)))


def _rms(x, g, n):
    ms = jnp.sum(x * x, axis=-1, keepdims=True) * (1.0 / n)
    return x * lax.rsqrt(ms + EPS) * g


def _dot(a, b):
    return jnp.dot(a, b, preferred_element_type=f32)


def _dot_nt(a, b):
    return lax.dot_general(a, b, (((1,), (1,)), ((), ())), preferred_element_type=f32)


def _dot_tn(a, b):
    return lax.dot_general(a, b, (((0,), (0,)), ((), ())), preferred_element_type=f32)


def _dot_exact(a, b):
    return jnp.dot(a, b, precision=HIGHEST, preferred_element_type=f32)


def _head_expand():
    r = lax.broadcasted_iota(jnp.int32, (LANE, SSD_W), 0)
    c = lax.broadcasted_iota(jnp.int32, (LANE, SSD_W), 1)
    return jnp.where(c // HD == r, 1.0, 0.0).astype(bf16)


def _split3(a):
    hi = a.astype(bf16)
    r = a - hi.astype(f32)
    mid = r.astype(bf16)
    return hi, mid, (r - mid.astype(f32)).astype(bf16)


def _dot_sel_right(a, sel):
    hi, mid, lo = _split3(a)
    return _dot(hi, sel) + _dot(mid, sel) + _dot(lo, sel)


def _dot_sel_left(sel, a):
    hi, mid, lo = _split3(a)
    return _dot(sel, hi) + _dot(sel, mid) + _dot(sel, lo)


def _ada_kernel(c_ref, w_ref, b_ref, op_ref, os_ref):
    s = _silu(c_ref[...]).astype(bf16)
    r = _dot(s, w_ref[0].astype(bf16)) + b_ref[0]
    rp = op_ref.shape[2]
    op_ref[0, 0] = r[:rp]
    os_ref[0, 0] = r[rp:]


def _ada_mod(c_prompt, c_sample, w_ada, b_ada):
    depth, d, n6 = w_ada.shape
    nf = n6 // d
    bp, bs = c_prompt.shape[0], c_sample.shape[0]
    rp = -(-bp // 8) * 8
    c_all = jnp.concatenate([c_prompt, jnp.zeros((rp - bp, d), f32), c_sample], axis=0)
    return pl.pallas_call(
        _ada_kernel,
        grid=(depth, nf),
        in_specs=[pl.BlockSpec((rp + bs, d), lambda l, j: (0, 0)),
                  pl.BlockSpec((1, d, d), lambda l, j: (l, 0, j)),
                  pl.BlockSpec((1, 1, d), lambda l, j: (l, 0, j))],
        out_specs=[pl.BlockSpec((1, 1, rp, d), lambda l, j: (l, j, 0, 0)),
                   pl.BlockSpec((1, 1, bs, d), lambda l, j: (l, j, 0, 0))],
        out_shape=[jax.ShapeDtypeStruct((depth, nf, rp, d), f32), jax.ShapeDtypeStruct((depth, nf, bs, d), f32)],
        compiler_params=_cparams(("arbitrary", "arbitrary")),
        name="ada_mod",
    )(c_all, w_ada, b_ada.reshape(depth, 1, n6))


def _mod_rows(mod_ref, field, per_row, tm):
    if per_row:
        return mod_ref[0, field, pl.ds(pl.multiple_of(pl.program_id(1) * tm, tm), tm), :]
    return mod_ref[0, field, pl.ds(pl.program_id(0), 1), :]


def _inproj_kernel(*refs, with_kv, per_row):
    (x_ref, mod_ref, g1_ref, w_ref, gq_ref, wq_ref, cosq_ref, sinq_ref, gqk_ref, gkv_ref,
     cosk_ref, sink_ref) = refs[:12]
    if with_kv:
        wk_ref, gk_ref, wv_ref = refs[12:15]
        z_ref, xbc_ref, dt_ref, u_ref, lat_ref, kpe_ref, q_ref, k_ref, v_ref = refs[15:]
    else:
        z_ref, xbc_ref, dt_ref, u_ref, lat_ref, kpe_ref, q_ref, kpe128_ref = refs[12:]
    tm, d = x_ref.shape[1:]
    shift, scale = (_mod_rows(mod_ref, f, per_row, tm) for f in (0, 1))
    gqk = gqk_ref[...]
    q_scale = PROMPT_Q_SCALE if with_kv else 1.0
    nsplit = 2 if tm % 512 == 0 else 1
    hm = tm // nsplit

    def rows_of(a, r):
        return a if a.shape[0] == 1 else a[r:r + hm]

    def project(r):
        h = _rms(x_ref[0, r:r + hm, :], g1_ref[...], d) * (1.0 + rows_of(scale, r)) + rows_of(shift, r)
        return _dot(h.astype(bf16), w_ref[0])

    def finish(r, proj):
        rs = slice(r, r + hm)
        z_ref[0, rs, :] = proj[:, _C_Z:_C_XBC]
        xbc_ref[0, rs, :] = proj[:, _C_XBC:_C_CQ]
        u_ref[0, rs, :] = proj[:, _C_U:_C_KA]
        dt_ref[0, rs, :] = proj[:, _C_DT:_C_END]
        cqn = _rms(proj[:, _C_CQ:_C_CKV], gq_ref[...], Q_LORA).astype(bf16)
        qq = _dot(cqn, wq_ref[0])
        cosq = rows_of(cosq_ref[...], r)
        sinq = rows_of(sinq_ref[...], r)
        for hh in range(NH):
            qh = qq[:, LANE * hh:LANE * (hh + 1)] * cosq + qq[:, NH * LANE + LANE * hh:NH * LANE + LANE * (hh + 1)] * sinq
            q_ref[0, rs, LANE * hh:LANE * (hh + 1)] = (_rms(qh, gqk, QK) * q_scale).astype(q_ref.dtype)
        lat = _rms(proj[:, _C_CKV:_C_U], gkv_ref[...], KV_LORA)
        lat_ref[0, rs, :] = lat
        kper = proj[:, _C_KA:_C_KB] * rows_of(cosk_ref[...], r) + proj[:, _C_KB:_C_DT] * rows_of(sink_ref[...], r)
        kpe_ref[0, rs, :] = kper[:, :ROPE]
        if with_kv:
            latb = lat.astype(bf16)
            kn = _dot(latb, wk_ref[...])
            kp = pltpu.roll(kper, NOPE, 1)
            gk = gk_ref[...]
            for hh in range(NH):
                kh = kn[:, LANE * hh:LANE * (hh + 1)] + kp
                k_ref[0, rs, LANE * hh:LANE * (hh + 1)] = _rms(kh, gk, QK).astype(bf16)
            v_ref[0, :, rs] = _dot_nt(wv_ref[...], latb).astype(bf16)
        else:
            kpe128_ref[0, rs, :] = kper

    nxt = project(0)
    for i in range(nsplit):
        proj = nxt
        if i + 1 < nsplit:
            nxt = project((i + 1) * hm)
        finish(i * hm, proj)


def _inproj(x, mod, wts, tabs, *, li, tm, with_kv, per_row):
    g_, t_, d = x.shape
    nt = t_ // tm
    cosq, sinq, cosk, sink = tabs
    tab_rows = cosq.shape[0] != 1
    tab_spec = pl.BlockSpec((tm if tab_rows else 1, LANE), (lambda g, t: (t, 0)) if tab_rows else (lambda g, t: (0, 0)))

    def full(a):
        return pl.BlockSpec(a.shape, lambda g, t: (0,) * a.ndim)

    def layer(a):
        return pl.BlockSpec((1,) + a.shape[1:], lambda g, t: (li,) + (0,) * (a.ndim - 1))

    def row(c):
        return pl.BlockSpec((1, tm, c), lambda g, t: (g, t, 0))

    ins = [x, mod, wts["g1"], wts["w_in"], wts["g_q"], wts["w_q"], cosq, sinq, wts["g_qk_q"], wts["g_kv"], cosk, sink]
    specs = [row(d), layer(mod), full(wts["g1"]), layer(wts["w_in"]), full(wts["g_q"]), layer(wts["w_q"]), tab_spec,
             tab_spec, full(wts["g_qk_q"]), full(wts["g_kv"]), tab_spec, tab_spec]
    widths = [SSD_W, CONV_CH, LANE, POOL_W, KV_LORA, ROPE, NH * LANE]
    dtypes = [f32, f32, f32, f32, f32, f32, bf16 if with_kv else f32]
    if with_kv:
        ins += [wts["w_k"], wts["g_qk_k"], wts["w_v"]]
        specs += [full(wts["w_k"]), full(wts["g_qk_k"]), full(wts["w_v"])]
        widths += [NH * LANE, SSD_W]
        dtypes += [bf16, bf16]
    else:
        widths += [LANE]
        dtypes += [f32]
    out_specs = [row(c) for c in widths]
    out_shape = [jax.ShapeDtypeStruct((g_, t_, c), dt) for c, dt in zip(widths, dtypes)]
    if with_kv:
        out_specs[-1] = pl.BlockSpec((1, SSD_W, tm), lambda g, t: (g, 0, t))
        out_shape[-1] = jax.ShapeDtypeStruct((g_, SSD_W, t_), bf16)
    return pl.pallas_call(
        functools.partial(_inproj_kernel, with_kv=with_kv, per_row=per_row),
        grid=(g_, nt),
        in_specs=specs,
        out_specs=out_specs,
        out_shape=out_shape,
        compiler_params=_cparams(("arbitrary", "arbitrary")),
        name="inproj_kv" if with_kv else "inproj",
    )(*ins)


def _ssd_prompt_kernel(xbc_ref, z_ref, dt_ref, cw_ref, cb_ref, dtb_ref, a_ref, dsk_ref, gn_ref, tril_ref, exp_ref,
                       y_ref, st_ref, ext_ref):
    c = pl.program_id(1)
    blk = xbc_ref.shape[1]
    t_ = SSD_CHUNK

    @pl.when(c == 0)
    def _():
        ext_ref[0:8, :] = jnp.zeros((8, CONV_CH), f32)
        st_ref[...] = jnp.zeros(st_ref.shape, f32)

    ext_ref[8:8 + blk, :] = xbc_ref[0]
    cw = cw_ref[...]
    tril = lax.broadcasted_iota(jnp.int32, (t_, t_), 0) >= lax.broadcasted_iota(jnp.int32, (t_, t_), 1)
    first = lax.broadcasted_iota(jnp.int32, (t_, LANE), 1) < HD
    rfirst = lax.broadcasted_iota(jnp.int32, (LANE, SSD_N), 0) < HD
    expand = exp_ref[...]

    def front(r):
        conv = cb_ref[...] + cw[3:4] * xbc_ref[0, r:r + t_, :]
        for k in range(CONV_K - 1):
            conv = conv + cw[k:k + 1] * ext_ref[5 + k + r:5 + k + r + t_, :]
        act = _silu(conv)
        xs = act[:, :SSD_W]
        bm = [act[:, SSD_W + SSD_N * g:SSD_W + SSD_N * (g + 1)].astype(bf16) for g in range(SSD_G)]
        cm = [act[:, SSD_W + SSD_N * (SSD_G + g):SSD_W + SSD_N * (SSD_G + g + 1)].astype(bf16) for g in range(SSD_G)]
        dt = _softplus(dt_ref[0, r:r + t_, :] + dtb_ref[...])
        a = dt * a_ref[...]
        acum = _dot_sel_left(tril_ref[...], a)
        acum_t = acum.T
        dt_l = _dot_sel_right(dt, expand)
        acum_l = _dot_sel_right(acum, expand)
        e_l = jnp.exp(acum_l)
        xdt = xs * dt_l
        xdt_w = xdt * jnp.exp(acum_l[t_ - 1:t_, :] - acum_l)
        cb = [_dot_nt(cm[g], bm[g]) for g in range(SSD_G)]
        cols = []
        for j in range(NH // 2):
            sl = slice(LANE * j, LANE * (j + 1))
            h0, h1 = 2 * j, 2 * j + 1
            xj = xdt[:, sl].astype(bf16)
            yd = []
            for hh in (h0, h1):
                seg = jnp.where(tril, jnp.exp(acum[:, hh:hh + 1] - acum_t[hh:hh + 1, :]), 0.0)
                yd.append(_dot((cb[hh // (NH // SSD_G)] * seg).astype(bf16), xj))
            tot = jnp.where(rfirst, jnp.exp(acum[t_ - 1:t_, h0:h0 + 1]), jnp.exp(acum[t_ - 1:t_, h1:h1 + 1]))
            cols.append((jnp.where(first, yd[0], yd[1]), xdt_w[:, sl].astype(bf16), tot, e_l[:, sl]))
        return xs, bm, cm, cols, _silu(z_ref[0, r:r + t_, :])

    def back(r, parts):
        xs, bm, cm, cols, gate = parts
        y_cols = []
        for j, (y_diag, xw, tot, e_j) in enumerate(cols):
            sl = slice(LANE * j, LANE * (j + 1))
            g0, g1 = (2 * j) // (NH // SSD_G), (2 * j + 1) // (NH // SSD_G)
            hp = st_ref[0, sl, :]
            hpb = hp.astype(bf16)
            if g0 == g1:
                y_off = _dot_nt(cm[g0], hpb)
                s_new = _dot_tn(xw, bm[g0])
            else:
                y_off = jnp.where(first, _dot_nt(cm[g0], hpb), _dot_nt(cm[g1], hpb))
                s_new = jnp.where(rfirst, _dot_tn(xw, bm[g0]), _dot_tn(xw, bm[g1]))
            st_ref[0, sl, :] = tot * hp + s_new
            y_cols.append(y_diag + y_off * e_j)
        v = (jnp.concatenate(y_cols, axis=-1) + dsk_ref[...] * xs) * gate
        gl = lax.broadcasted_iota(jnp.int32, (t_, SSD_W), 1) < SSD_W // SSD_G
        v2 = v * v
        ss0 = jnp.sum(jnp.where(gl, v2, 0.0), axis=-1, keepdims=True)
        ss1 = jnp.sum(jnp.where(gl, 0.0, v2), axis=-1, keepdims=True)
        gw = SSD_W // SSD_G
        rinv = jnp.where(gl, lax.rsqrt(ss0 * (1.0 / gw) + EPS), lax.rsqrt(ss1 * (1.0 / gw) + EPS))
        y_ref[0, r:r + t_, :] = v * rinv * gn_ref[...]

    fronts = [front(r) for r in range(0, blk, t_)]
    for i, parts in enumerate(fronts):
        back(i * t_, parts)
    ext_ref[0:8, :] = xbc_ref[0, blk - 8:, :]


def _ssd_prompt(xbc, z, dt, wts, *, blk=4 * SSD_CHUNK):
    b_, l_, _ = xbc.shape
    t_ = SSD_CHUNK
    blk = blk if l_ % blk == 0 else t_
    nc = l_ // blk

    def full(a):
        return pl.BlockSpec(a.shape, lambda b, c: (0,) * a.ndim)

    def row(w):
        return pl.BlockSpec((1, blk, w), lambda b, c: (b, c, 0))

    small = [wts["conv_w"], wts["conv_b"], wts["dt_bias"], wts["a_neg"], wts["d_skip_l"], wts["g_ssd"],
             jnp.tril(jnp.ones((t_, t_), bf16)), _head_expand()]
    return pl.pallas_call(
        _ssd_prompt_kernel,
        grid=(b_, nc),
        in_specs=[row(CONV_CH), row(SSD_W), row(LANE)] + [full(a) for a in small],
        out_specs=[row(SSD_W), pl.BlockSpec((1, SSD_W, SSD_N), lambda b, c: (b, 0, 0))],
        out_shape=[jax.ShapeDtypeStruct((b_, l_, SSD_W), f32), jax.ShapeDtypeStruct((b_, SSD_W, SSD_N), f32)],
        scratch_shapes=[pltpu.VMEM((8 + blk, CONV_CH), f32)],
        compiler_params=_cparams(("arbitrary", "arbitrary")),
        name="ssd_prompt",
    )(xbc, z, dt, *small)


def _attn_prompt_kernel(q_ref, k_ref, vt_ref, g_ref, o_ref, m_ref, l_ref, acc_ref, *, tq):
    qi = pl.program_id(1)
    m_ref[...] = jnp.full(m_ref.shape, -1e30, f32)
    l_ref[...] = jnp.zeros(l_ref.shape, f32)
    acc_ref[...] = jnp.zeros(acc_ref.shape, f32)

    def tile(key0, nk, q0, nq, diagonal):
        kblk = k_ref[0, pl.ds(key0, nk), :]
        vblk = vt_ref[0, :, pl.ds(key0, nk)]
        qs = slice(q0, q0 + nq)
        ones = jnp.ones((16, nk), bf16)
        if diagonal:
            masked = lax.broadcasted_iota(jnp.int32, (nk, nq), 0) > lax.broadcasted_iota(jnp.int32, (nk, nq), 1)

        def qk(hh):
            return _dot_nt(kblk[:, LANE * hh:LANE * (hh + 1)], q_ref[0, qs, LANE * hh:LANE * (hh + 1)])

        ahead = 3
        pending = [qk(hh) for hh in range(ahead)]
        for hh in range(NH):
            s = pending.pop(0)
            if hh + ahead < NH:
                pending.append(qk(hh + ahead))
            if diagonal:
                s = jnp.where(masked, -1e30, s)
            m_old = m_ref[hh, :, qs]
            m_new = jnp.maximum(m_old, jnp.max(s, axis=0, keepdims=True))
            alpha = jnp.exp2(m_old - m_new)
            pe = jnp.exp2(s - m_new).astype(bf16)
            m_ref[hh, :, qs] = m_new
            rows = slice(HD * hh, HD * (hh + 1))
            pv = _dot(jnp.concatenate([vblk[rows, :], ones], axis=0), pe)
            acc_ref[rows, qs] = acc_ref[rows, qs] * alpha + pv[:HD]
            l_ref[hh, :, qs] = alpha * l_ref[hh, :, qs] + pv[HD:HD + 8]

    def body(j, carry):
        tile(pl.multiple_of(j * tq, tq), tq, 0, tq, False)
        return carry

    lax.fori_loop(0, qi, body, 0)
    tile(pl.multiple_of(qi * tq, tq), tq, 0, tq, True)
    parts = []
    for hh in range(NH):
        parts.append(acc_ref[HD * hh:HD * (hh + 1), :] * (1.0 / l_ref[hh, 0:1, :]))
    o_t = jnp.concatenate(parts, axis=0)
    ms = jnp.sum(o_t * o_t, axis=0, keepdims=True) * (1.0 / SSD_W)
    o_ref[0] = (o_t * lax.rsqrt(ms + EPS)).T * g_ref[...]


def _attn_prompt(q, k, v_t, g_out, *, tq):
    b_, l_, _ = q.shape
    return pl.pallas_call(
        functools.partial(_attn_prompt_kernel, tq=tq),
        grid=(b_, l_ // tq),
        in_specs=[pl.BlockSpec((1, tq, NH * LANE), lambda b, i: (b, i, 0)),
                  pl.BlockSpec((1, l_, NH * LANE), lambda b, i: (b, 0, 0)),
                  pl.BlockSpec((1, SSD_W, l_), lambda b, i: (b, 0, 0)),
                  pl.BlockSpec(g_out.shape, lambda b, i: (0, 0))],
        out_specs=pl.BlockSpec((1, tq, SSD_W), lambda b, i: (b, i, 0)),
        out_shape=jax.ShapeDtypeStruct((b_, l_, SSD_W), f32),
        scratch_shapes=[pltpu.VMEM((NH, 1, tq), f32), pltpu.VMEM((NH, 8, tq), f32), pltpu.VMEM((SSD_W, tq), f32)],
        compiler_params=_cparams(("arbitrary", "arbitrary")),
        name="attn_prompt",
    )(q, k, v_t, g_out)


def _pool_select(lane, a, b, c, d):
    return jnp.where(lane < 64, a, jnp.where(lane < 128, b, jnp.where(lane < 192, c, d)))


def _pool_prompt_kernel(u_ref, w_ref, sc_ref, o_ref, e1, e2, e4, e8, *, rt):
    t_ = u_ref.shape[1]
    hist = 16
    for e in (e1, e2, e4, e8):
        e[0:hist, :] = jnp.zeros((hist, POOL_W), f32)
    e1[hist:hist + t_, :] = u_ref[0]
    lane = lax.broadcasted_iota(jnp.int32, (rt, POOL_W), 1)
    win = _pool_select(lane, 2, 4, 8, 16)
    for i in range(t_ // rt):
        r0 = hist + i * rt
        a = e1[r0:r0 + rt, :]
        s2 = a + e1[r0 - 1:r0 - 1 + rt, :]
        e2[r0:r0 + rt, :] = s2
        s4 = s2 + e2[r0 - 2:r0 - 2 + rt, :]
        e4[r0:r0 + rt, :] = s4
        s8 = s4 + e4[r0 - 4:r0 - 4 + rt, :]
        e8[r0:r0 + rt, :] = s8
        s16 = s8 + e8[r0 - 8:r0 - 8 + rt, :]
        pos = lax.broadcasted_iota(jnp.int32, (rt, POOL_W), 0) + i * rt
        cnt = jnp.minimum(pos + 1, win).astype(f32)
        pooled = _pool_select(lane, s2, s4, s8, s16) / cnt - a
        o_ref[0, i * rt:(i + 1) * rt, :] = _dot(pooled.astype(bf16), w_ref[...]) * sc_ref[...]


def _pool_prompt(u, w_bd, scale):
    b_, l_, _ = u.shape
    return pl.pallas_call(
        functools.partial(_pool_prompt_kernel, rt=256),
        grid=(b_,),
        in_specs=[pl.BlockSpec((1, l_, POOL_W), lambda b: (b, 0, 0)),
                  pl.BlockSpec(w_bd.shape, lambda b: (0, 0)),
                  pl.BlockSpec(scale.shape, lambda b: (0, 0))],
        out_specs=pl.BlockSpec((1, l_, POOL_W), lambda b: (b, 0, 0)),
        out_shape=jax.ShapeDtypeStruct((b_, l_, POOL_W), f32),
        scratch_shapes=[pltpu.VMEM((16 + l_, POOL_W), f32)] * 4,
        compiler_params=_cparams(("arbitrary",)),
        name="pool_prompt",
    )(u, w_bd, scale)


def _pool_sample_kernel(u_ref, buf_ref, w_ref, sc_ref, o_ref):
    u = u_ref[...]
    run = u
    sums = {}
    for i in range(1, 16):
        run = run + buf_ref[POOL_BUF - i]
        if i + 1 in (2, 4, 8, 16):
            sums[i + 1] = run * (1.0 / (i + 1))
    lane = lax.broadcasted_iota(jnp.int32, u.shape, 1)
    pooled = _pool_select(lane, sums[2], sums[4], sums[8], sums[16]) - u
    o_ref[...] = _dot(pooled.astype(bf16), w_ref[...]) * sc_ref[...]


def _pool_sample(u, buf_t, w_bd, scale):
    return pl.pallas_call(
        _pool_sample_kernel,
        out_shape=jax.ShapeDtypeStruct(u.shape, f32),
        compiler_params=pltpu.CompilerParams(vmem_limit_bytes=VMEM_LIMIT),
        name="pool_sample",
    )(u, buf_t, w_bd, scale)


def _ffn_kernel(x_ref, ssd_ref, mla_ref, pool_ref, mod_ref, gn_ref, wo_ref, wg_ref, wu_ref, wd_ref, o_ref,
                *, nch, per_row):
    tm, d = x_ref.shape[1:]
    gate1, shift2, scale2, gate2 = (_mod_rows(mod_ref, f, per_row, tm) for f in (2, 3, 4, 5))
    hc = wg_ref.shape[2] // nch
    nsplit = 2 if tm % 512 == 0 else 1
    hm = tm // nsplit

    def rows_of(a, r):
        return a if a.shape[0] == 1 else a[r:r + hm]

    def out_proj(r):
        rs = slice(r, r + hm)
        mixed = jnp.concatenate([ssd_ref[0, rs, :].astype(bf16), mla_ref[0, rs, :].astype(bf16),
                                 pool_ref[0, rs, :].astype(bf16)], axis=-1)
        return _dot(mixed, wo_ref[0])

    def residual_norm(r, mix):
        x1 = x_ref[0, r:r + hm, :] + rows_of(gate1, r) * mix
        return x1, (_rms(x1, gn_ref[...], d) * (1.0 + rows_of(scale2, r)) + rows_of(shift2, r)).astype(bf16)

    def gate_up(h2, c):
        return _dot(h2, wg_ref[0, :, hc * c:hc * (c + 1)]), _dot(h2, wu_ref[0, :, hc * c:hc * (c + 1)])

    mixes = [out_proj(i * hm) for i in range(nsplit)]
    x1, h2 = residual_norm(0, mixes[0])
    for i in range(nsplit):
        r = i * hm
        acc = jnp.zeros(x1.shape, f32)
        nxt = gate_up(h2, 0)
        if i + 1 < nsplit:
            x1_next, h2_next = residual_norm(r + hm, mixes[i + 1])
        for c in range(nch):
            gate, up = nxt
            if c + 1 < nch:
                nxt = gate_up(h2, c + 1)
            acc = acc + _dot((_silu(gate) * up).astype(bf16), wd_ref[0, hc * c:hc * (c + 1), :])
        o_ref[0, r:r + hm, :] = x1 + rows_of(gate2, r) * acc
        if i + 1 < nsplit:
            x1, h2 = x1_next, h2_next


def _ffn(x, ssd, mla, pool, mod, wts, *, li, tm, per_row):
    g_, t_, d = x.shape

    def layer(a):
        return pl.BlockSpec((1,) + a.shape[1:], lambda g, t: (li,) + (0,) * (a.ndim - 1), pipeline_mode=pl.Buffered(1))

    def row(c):
        return pl.BlockSpec((1, tm, c), lambda g, t: (g, t, 0))

    ws = [wts["w_out"], wts["w_gate"], wts["w_up"], wts["w_down"]]
    return pl.pallas_call(
        functools.partial(_ffn_kernel, nch=11, per_row=per_row),
        grid=(g_, t_ // tm),
        in_specs=[row(d), row(SSD_W), row(SSD_W), row(POOL_W), layer(mod),
                  pl.BlockSpec(wts["g2"].shape, lambda g, t: (0, 0))] + [layer(a) for a in ws],
        out_specs=row(d),
        out_shape=jax.ShapeDtypeStruct((g_, t_, d), f32),
        compiler_params=_cparams(("arbitrary", "arbitrary")),
        name="outproj_ffn",
    )(x, ssd, mla, pool, mod, wts["g2"], *ws)


def _ssd_sample_pre_kernel(xbc_ref, buf_ref, dt_ref, cw_ref, cb_ref, dtb_ref, a_ref,
                           xs_ref, b_ref, c_ref, xdt_t_ref, da_ref):
    cw = cw_ref[...]
    conv = cb_ref[...] + cw[3:4] * xbc_ref[...]
    for k in range(CONV_K - 1):
        conv = conv + cw[k:k + 1] * buf_ref[k]
    act = _silu(conv)
    xs = act[:, :SSD_W]
    xs_ref[...] = xs
    b_ref[...] = act[:, SSD_W:SSD_W + SSD_G * SSD_N]
    c_ref[...] = act[:, SSD_W + SSD_G * SSD_N:]
    dt = _softplus(dt_ref[...] + dtb_ref[...])
    da_ref[...] = jnp.exp(dt * a_ref[...])
    xdt = xs * _dot_sel_right(dt, _head_expand())
    for j in range(SSD_W // LANE):
        xdt_t_ref[LANE * j:LANE * (j + 1), :] = xdt[:, LANE * j:LANE * (j + 1)].T


def _ssd_sample_pre(xbc, buf_t, dt, wts):
    b_ = xbc.shape[0]
    shapes = [(b_, SSD_W), (b_, SSD_G * SSD_N), (b_, SSD_G * SSD_N), (SSD_W, b_), (b_, LANE)]
    return pl.pallas_call(
        _ssd_sample_pre_kernel,
        out_shape=[jax.ShapeDtypeStruct(s, f32) for s in shapes],
        compiler_params=pltpu.CompilerParams(vmem_limit_bytes=VMEM_LIMIT),
        name="ssd_sample_pre",
    )(xbc, buf_t, dt, wts["conv_w"], wts["conv_b"], wts["dt_bias"], wts["a_neg"])


def _ssd_sample_state_kernel(da_ref, st_ref, xdt_t_ref, b_ref, c_ref, new_ref, y_t_ref, *, tb):
    i = pl.program_id(0)

    @pl.when(i == 0)
    def _():
        y_t_ref[...] = jnp.zeros(y_t_ref.shape, f32)

    nb = xdt_t_ref.shape[1]
    lane = lax.broadcasted_iota(jnp.int32, (HD, nb), 1)
    pairs = [(hh, bl) for hh in range(NH) for bl in range(tb)]
    sels = [lane == i * tb + bl for bl in range(tb)]
    cols = {}
    for hh, bl in pairs:
        cols[hh, bl] = jnp.sum(jnp.where(sels[bl], xdt_t_ref[HD * hh:HD * (hh + 1), :], 0.0), axis=1, keepdims=True)
    ycols = {}
    for hh, bl in pairs:
        g = hh // (NH // SSD_G)
        rs = slice(HD * hh, HD * (hh + 1))
        brow = b_ref[bl:bl + 1, SSD_N * g:SSD_N * (g + 1)]
        crow = c_ref[bl:bl + 1, SSD_N * g:SSD_N * (g + 1)]
        new = da_ref[i * tb + bl, hh] * st_ref[bl, rs, :] + cols[hh, bl] * brow
        new_ref[bl, rs, :] = new
        ycols[hh, bl] = jnp.sum(new * crow, axis=1, keepdims=True)
    for hh in range(NH):
        rs = slice(HD * hh, HD * (hh + 1))
        y_rows = y_t_ref[rs, :]
        for bl in range(tb):
            y_rows = jnp.where(sels[bl], ycols[hh, bl], y_rows)
        y_t_ref[rs, :] = y_rows


def _ssd_sample_state(da, state_all, xdt_t, bm, cm, *, li, tb=8):
    b_ = xdt_t.shape[1]
    nblk = b_ // tb
    return pl.pallas_call(
        functools.partial(_ssd_sample_state_kernel, tb=tb),
        grid=(nblk,),
        in_specs=[pl.BlockSpec(memory_space=pltpu.SMEM),
                  pl.BlockSpec((tb, SSD_W, SSD_N), lambda i: (li * nblk + i, 0, 0)),
                  pl.BlockSpec(xdt_t.shape, lambda i: (0, 0)),
                  pl.BlockSpec((tb, SSD_G * SSD_N), lambda i: (i, 0)),
                  pl.BlockSpec((tb, SSD_G * SSD_N), lambda i: (i, 0))],
        out_specs=[pl.BlockSpec((tb, SSD_W, SSD_N), lambda i: (i, 0, 0)),
                   pl.BlockSpec((SSD_W, b_), lambda i: (0, 0))],
        out_shape=[jax.ShapeDtypeStruct((b_, SSD_W, SSD_N), f32), jax.ShapeDtypeStruct((SSD_W, b_), f32)],
        compiler_params=_cparams(("arbitrary",)),
        name="ssd_sample_state",
    )(da, state_all, xdt_t, bm, cm)


def _ssd_sample_post_kernel(y_ref, xs_ref, z_ref, dsk_ref, gn_ref, o_ref):
    y = y_ref[...] + dsk_ref[...] * xs_ref[...]
    v = y * _silu(z_ref[...])
    gl = lax.broadcasted_iota(jnp.int32, v.shape, 1) < SSD_W // SSD_G
    v2 = v * v
    gw = SSD_W // SSD_G
    ss0 = jnp.sum(jnp.where(gl, v2, 0.0), axis=-1, keepdims=True)
    ss1 = jnp.sum(jnp.where(gl, 0.0, v2), axis=-1, keepdims=True)
    rinv = jnp.where(gl, lax.rsqrt(ss0 * (1.0 / gw) + EPS), lax.rsqrt(ss1 * (1.0 / gw) + EPS))
    o_ref[...] = v * rinv * gn_ref[...]


def _ssd_sample_post(y, xs, z, wts):
    return pl.pallas_call(
        _ssd_sample_post_kernel,
        out_shape=jax.ShapeDtypeStruct(y.shape, f32),
        name="ssd_sample_post",
    )(y, xs, z, wts["d_skip_l"], wts["g_ssd"])


def _mla_sample_pre_kernel(q_ref, lat_ref, kpe_ref, wk_ref, wkt_ref, gk_ref, qabs_ref, qpe_ref, snew_ref):
    kn = _dot(lat_ref[...].astype(bf16), wk_ref[...])
    kp = pltpu.roll(kpe_ref[...], NOPE, 1)
    gk = gk_ref[...]
    lane = lax.broadcasted_iota(jnp.int32, snew_ref.shape, 1)
    snew = jnp.zeros(snew_ref.shape, f32)
    for hh in range(NH):
        sl = slice(LANE * hh, LANE * (hh + 1))
        qh = q_ref[:, sl]
        knew = _rms(kn[:, sl] + kp, gk, QK)
        snew = jnp.where(lane == hh, jnp.sum(qh * knew, axis=-1, keepdims=True) * ATTN_SCALE, snew)
        qg = qh * gk
        qabs_ref[:, sl] = _dot_exact(qg, wkt_ref[hh])
        qpe_ref[:, sl] = pltpu.roll(qg, NOPE, 1)
    snew_ref[...] = snew


def _mla_sample_pre(q, lat, kpe128, wts):
    b_ = q.shape[0]
    shapes = [(b_, NH * LANE), (b_, NH * LANE), (b_, LANE)]
    return pl.pallas_call(
        _mla_sample_pre_kernel,
        out_shape=[jax.ShapeDtypeStruct(s, f32) for s in shapes],
        name="mla_sample_pre",
    )(q, lat, kpe128, wts["w_k"], wts["w_kt"], wts["g_qk_k"])


def _attn_sample_kernel(pt_ref, lat_hbm, kpe_hbm, wkt_ref, qabs_ref, qpe_ref, snew_ref, latnew_ref, o_ref,
                        lat_buf, kpe_buf, sem, wext, latb_ref, s_ref, *, li, pages, nchunk):
    b = pl.program_id(0)
    nb = pl.num_programs(0)
    rc = pages * PAGE
    slot = b % 2

    def aligned(idx, size):
        return idx * size if isinstance(idx, int) else pl.multiple_of(idx * size, size)

    def copies(bb, c, i, sl):
        page_idx = c * pages + i
        page = pt_ref[bb, page_idx]
        off = aligned(page_idx, PAGE)
        return (pltpu.make_async_copy(lat_hbm.at[li, page], lat_buf.at[sl, pl.ds(off, PAGE), :], sem.at[0, sl, c]),
                pltpu.make_async_copy(kpe_hbm.at[li, page], kpe_buf.at[sl, :, pl.ds(off, PAGE)], sem.at[1, sl, c]))

    def start_chunk(bb, c, sl):
        for i in range(pages):
            for cp in copies(bb, c, i, sl):
                cp.start()

    @pl.when(b == 0)
    def _():
        def issue(c, carry):
            start_chunk(0, c, 0)
            start_chunk(jnp.minimum(1, nb - 1), c, 1)
            return carry
        lax.fori_loop(0, nchunk, issue, 0)
        wext[0:NH * HD, :] = wkt_ref[...]

    wext[NH * HD:NH * HD + HPAD, :] = qabs_ref[0]
    qpe = qpe_ref[0]
    rowi = lax.broadcasted_iota(jnp.int32, (HPAD, rc), 0)

    def scores(c, carry):
        for i in range(pages):
            for cp in copies(b, c, i, slot):
                cp.wait()
        off = aligned(c, rc)
        latb = lat_buf[slot, pl.ds(off, rc), :].astype(bf16)
        latb_ref[pl.ds(off, rc), :] = latb
        kpe = kpe_buf[slot, :, pl.ds(off, rc)]
        a = _dot_nt(wext[...], latb)
        nsq = jnp.zeros((HPAD, rc), f32)
        for hh in range(NH):
            kh = a[HD * hh:HD * (hh + 1), :]
            nsq = jnp.where(rowi == hh, jnp.sum(kh * kh, axis=0, keepdims=True), nsq)
        s_pe = _dot(qpe, kpe.astype(bf16))
        ksq = jnp.sum(kpe * kpe, axis=0, keepdims=True)
        s_ref[:, pl.ds(off, rc)] = (a[NH * HD:, :] + s_pe) * lax.rsqrt((nsq + ksq) * (1.0 / QK) + EPS) * ATTN_SCALE
        return carry

    lax.fori_loop(0, nchunk, scores, 0)

    b_ahead = jnp.minimum(b + 2, nb - 1)
    for c in range(nchunk):
        start_chunk(b_ahead, c, slot)

    s = s_ref[...]
    s_new = snew_ref[0][:, 0:1]
    m = jnp.maximum(jnp.max(s, axis=-1, keepdims=True), s_new)
    pe = jnp.exp(s - m)
    p_new = jnp.exp(s_new - m)
    l = jnp.sum(pe, axis=-1, keepdims=True) + p_new
    acc = _dot(pe.astype(bf16), latb_ref[...]) + p_new * latnew_ref[0]
    o_ref[0] = acc / l

    @pl.when(b == nb - 1)
    def _():
        def drain(c, carry):
            for sl in range(2):
                for i in range(pages):
                    for cp in copies(b, c, i, sl):
                        cp.wait()
            return carry
        lax.fori_loop(0, nchunk, drain, 0)


def _attn_sample(page_table, cache_lat, cache_kpe_t, wkt, qabs, qpe, snew, lat_new, *, li, pages=64):
    b_, npages = page_table.shape
    nchunk = npages // pages
    assert npages % pages == 0
    seq = npages * PAGE
    grid_spec = pltpu.PrefetchScalarGridSpec(
        num_scalar_prefetch=1,
        grid=(b_,),
        in_specs=[pl.BlockSpec(memory_space=pl.ANY),
                  pl.BlockSpec(memory_space=pl.ANY),
                  pl.BlockSpec(wkt.shape, lambda b, pt: (0, 0)),
                  pl.BlockSpec((1, HPAD, KV_LORA), lambda b, pt: (b, 0, 0)),
                  pl.BlockSpec((1, HPAD, ROPE), lambda b, pt: (b, 0, 0)),
                  pl.BlockSpec((1, HPAD, LANE), lambda b, pt: (b, 0, 0)),
                  pl.BlockSpec((1, 1, KV_LORA), lambda b, pt: (b, 0, 0))],
        out_specs=pl.BlockSpec((1, HPAD, KV_LORA), lambda b, pt: (b, 0, 0)),
        scratch_shapes=[pltpu.VMEM((2, seq, KV_LORA), f32),
                        pltpu.VMEM((2, ROPE, seq), f32),
                        pltpu.SemaphoreType.DMA((2, 2, nchunk)),
                        pltpu.VMEM((NH * HD + HPAD, KV_LORA), bf16),
                        pltpu.VMEM((seq, KV_LORA), bf16),
                        pltpu.VMEM((HPAD, seq), f32)],
    )
    return pl.pallas_call(
        functools.partial(_attn_sample_kernel, li=li, pages=pages, nchunk=nchunk),
        grid_spec=grid_spec,
        out_shape=jax.ShapeDtypeStruct((b_, HPAD, KV_LORA), f32),
        compiler_params=_cparams(("arbitrary",)),
        name="attn_sample",
    )(page_table, cache_lat, cache_kpe_t, wkt, qabs, qpe, snew, lat_new)


def _mla_sample_post_kernel(o_ref, wv_ref, g_ref, out_ref):
    out_ref[...] = _rms(_dot(o_ref[...].astype(bf16), wv_ref[...]), g_ref[...], SSD_W)


def _mla_sample_post(o_lat, w_v_bd, g_out):
    return pl.pallas_call(
        _mla_sample_post_kernel,
        out_shape=jax.ShapeDtypeStruct((o_lat.shape[0], SSD_W), f32),
        name="mla_sample_post",
    )(o_lat, w_v_bd, g_out)


def _pad_heads(w, lo, hi):
    pad = [(0, 0)] * (w.ndim - 1) + [(lo, LANE - hi)]
    w = jnp.pad(w, pad)
    return w.reshape(w.shape[:-2] + (NH * LANE,))


def _rot_cols(w):
    half = ROPE // 2
    return jnp.concatenate([-w[..., half:], w[..., :half]], axis=-1)


def _pad_lanes(a, lo=0):
    return jnp.pad(a, [(0, 0)] * (a.ndim - 1) + [(lo, LANE - lo - a.shape[-1])])


def _stacked_weights(p):
    offs = np.cumsum([SSD_W, CONV_CH, NH, Q_LORA, KV_LORA, ROPE, POOL_W])
    wz, wxbc, wdt, wcq, wckv, wkpe, wu = jnp.split(p["w_in"], offs[:-1].tolist(), axis=2)
    w_in = jnp.concatenate([wz, wxbc, wcq, wckv, wu, _pad_lanes(wkpe), _pad_lanes(_rot_cols(wkpe)), _pad_lanes(wdt)],
                           axis=2).astype(bf16)
    wq = p["w_q_up"]
    w_q = jnp.concatenate([_pad_heads(wq, 0, QK), _pad_heads(_rot_cols(wq[..., NOPE:]), NOPE, QK)], axis=2).astype(bf16)
    return dict(w_in=w_in, w_q=w_q, w_out=p["w_out"].astype(bf16), w_gate=p["w_gate"].astype(bf16),
                w_up=p["w_up"].astype(bf16), w_down=p["w_down"].astype(bf16))


def _layer_weights(p, li):
    d = p["w_in"].shape[1]
    wk = p["w_k_up"][li]
    wk_pad = _pad_heads(wk, 0, NOPE)
    wk_t = jnp.transpose(wk, (1, 2, 0))
    wv = p["w_v_up"][li]
    w_v_bd = jnp.zeros((NH, KV_LORA, NH, HD), f32)
    w_v_bd = w_v_bd.at[jnp.arange(NH), :, jnp.arange(NH), :].set(jnp.transpose(wv, (1, 0, 2)))
    wp = p["w_pool"][li]
    ng = wp.shape[0]
    w_pool_bd = jnp.zeros((ng, HD, ng, HD), f32).at[jnp.arange(ng), :, jnp.arange(ng), :].set(wp)
    return dict(
        g1=p["g_norm1"][li].reshape(1, d),
        g_q=p["g_q_lora"][li].reshape(1, Q_LORA),
        g_qk_q=_pad_lanes(p["g_qk_q"][li].reshape(1, QK)), g_qk_k=_pad_lanes(p["g_qk_k"][li].reshape(1, QK)),
        g_kv=p["g_kv_lora"][li].reshape(1, KV_LORA),
        w_k=wk_pad.astype(bf16), w_v=wv.reshape(KV_LORA, NH * HD).T.astype(bf16),
        w_kt=jnp.pad(wk_t, ((0, 0), (0, LANE - NOPE), (0, 0))),
        w_kt_flat=wk_t.reshape(NH * NOPE, KV_LORA).astype(bf16),
        w_v_bd=w_v_bd.reshape(NH * KV_LORA, NH * HD).astype(bf16),
        g_mla=p["g_mla_out"][li].reshape(1, SSD_W),
        conv_w=p["conv_w"][li], conv_b=p["conv_b"][li].reshape(1, CONV_CH),
        dt_bias=_pad_lanes(p["dt_bias"][li].reshape(1, NH)),
        a_neg=_pad_lanes(-jnp.exp(p["a_log"][li].astype(f32)).reshape(1, NH)),
        d_skip_l=jnp.repeat(p["d_skip"][li], HD).reshape(1, SSD_W), g_ssd=p["g_ssd_norm"][li].reshape(1, SSD_W),
        w_pool=w_pool_bd.reshape(POOL_W, POOL_W).astype(bf16), pool_scale=p["pool_scale"][li].reshape(1, POOL_W),
        g2=p["g_norm2"][li].reshape(1, d),
    )


def _rope_tables(pos):
    half = ROPE // 2
    inv = 1.0 / (ROPE_THETA ** (jnp.arange(half, dtype=f32) / half))
    ang = pos.astype(f32)[:, None] * inv[None, :]
    cos2 = jnp.concatenate([jnp.cos(ang)] * 2, axis=-1)
    sin2 = jnp.concatenate([jnp.sin(ang)] * 2, axis=-1)
    n = pos.shape[0]
    cosq = jnp.concatenate([jnp.ones((n, NOPE), f32), cos2, jnp.zeros((n, LANE - QK), f32)], axis=-1)
    sinq = _pad_lanes(sin2, NOPE)
    return cosq, sinq, _pad_lanes(cos2), _pad_lanes(sin2)


def kernel(x_prompt, x_sample, cache_kv_latent, cache_k_rope, state_ssm, state_conv, state_pool, page_table, c_prompt, c_sample, w_ada, b_ada, g_norm1, w_in, conv_w, conv_b, dt_bias, a_log, d_skip, g_ssd_norm, g_q_lora, w_q_up, g_kv_lora, w_k_up, w_v_up, g_qk_q, g_qk_k, g_mla_out, w_pool, pool_scale, w_out, g_norm2, w_gate, w_up, w_down):
    params = dict(g_norm1=g_norm1, w_in=w_in, conv_w=conv_w, conv_b=conv_b, dt_bias=dt_bias, a_log=a_log, d_skip=d_skip,
                  g_ssd_norm=g_ssd_norm, g_q_lora=g_q_lora, w_q_up=w_q_up, g_kv_lora=g_kv_lora, w_k_up=w_k_up,
                  w_v_up=w_v_up, g_qk_q=g_qk_q, g_qk_k=g_qk_k, g_mla_out=g_mla_out, w_pool=w_pool, pool_scale=pool_scale,
                  w_out=w_out, g_norm2=g_norm2, w_gate=w_gate, w_up=w_up, w_down=w_down)
    depth = w_ada.shape[0]
    bp, seq, d = x_prompt.shape
    bs = x_sample.shape[0]
    past = page_table.shape[1] * PAGE

    mod_p, mod_s = _ada_mod(c_prompt, c_sample, w_ada, b_ada)
    big = _stacked_weights(params)
    state_all = state_ssm.reshape(depth * bs, SSD_W, SSD_N)
    tabs_p = _rope_tables(jnp.arange(seq, dtype=jnp.int32))
    tabs_s = _rope_tables(jnp.full((1,), past, jnp.int32))
    cache_kpe_t = jnp.swapaxes(cache_k_rope, 2, 3)

    yp = x_prompt
    ys = x_sample.reshape(1, bs, d)
    p_new = [[] for _ in range(5)]
    s_new = [[] for _ in range(5)]
    for li in range(depth):
        wts = dict(_layer_weights(params, li), **big)

        z, xbc, dt, u, lat, kpe, q, k, v = _inproj(yp, mod_p, wts, tabs_p, li=li, tm=512, with_kv=True, per_row=False)
        ssd_out, h_t = _ssd_prompt(xbc, z, dt, wts)
        mla_out = _attn_prompt(q, k, v, wts["g_mla"], tq=512)
        pool_out = _pool_prompt(u, wts["w_pool"], wts["pool_scale"])
        yp = _ffn(yp, ssd_out, mla_out, pool_out, mod_p, wts, li=li, tm=512, per_row=False)
        for lst, val in zip(p_new, (lat, kpe, h_t.reshape(bp, NH, HD, SSD_N), xbc[:, seq - (CONV_K - 1):],
                                    u[:, seq - POOL_BUF:])):
            lst.append(val)

        z, xbc, dt, u, lat, kpe, q, kpe128 = _inproj(ys, mod_s, wts, tabs_s, li=li, tm=bs, with_kv=False, per_row=True)
        z, xbc, dt, u, lat, kpe, q, kpe128 = (a[0] for a in (z, xbc, dt, u, lat, kpe, q, kpe128))
        conv_buf = state_conv[li]
        xs, bm, cm, xdt_t, da = _ssd_sample_pre(xbc, jnp.transpose(conv_buf, (1, 0, 2)), dt, wts)
        h_new, y_t = _ssd_sample_state(da[:, :8], state_all, xdt_t, bm, cm, li=li)
        ssd_out = _ssd_sample_post(y_t.T, xs, z, wts)
        qabs, qpe, snew = _mla_sample_pre(q, lat, kpe128, wts)
        qabs = jnp.pad(qabs.reshape(bs, NH, LANE), ((0, 0), (0, HPAD - NH), (0, 0))).astype(bf16)
        qpe = jnp.pad(qpe.reshape(bs, NH, LANE)[:, :, :ROPE], ((0, 0), (0, HPAD - NH), (0, 0))).astype(bf16)
        snew_b = jnp.broadcast_to(jnp.pad(snew[:, :NH], ((0, 0), (0, HPAD - NH)))[:, :, None], (bs, HPAD, LANE))
        o_lat = _attn_sample(page_table, cache_kv_latent, cache_kpe_t, wts["w_kt_flat"], qabs, qpe, snew_b,
                             lat.reshape(bs, 1, KV_LORA), li=li)
        mla_out = _mla_sample_post(o_lat[:, :NH].reshape(bs, NH * KV_LORA), wts["w_v_bd"], wts["g_mla"])
        pool_buf = state_pool[li]
        pool_out = _pool_sample(u, jnp.transpose(pool_buf, (1, 0, 2)), wts["w_pool"], wts["pool_scale"])
        ys = _ffn(ys, ssd_out[None], mla_out[None], pool_out[None], mod_s, wts, li=li, tm=bs, per_row=True)
        conv_new = jnp.concatenate([conv_buf[:, 1:], xbc[:, None, :]], axis=1)
        pool_new = jnp.concatenate([pool_buf[:, 1:], u[:, None, :]], axis=1)
        for lst, val in zip(s_new, (lat[:, None, :], kpe[:, None, :], h_new.reshape(bs, NH, HD, SSD_N), conv_new, pool_new)):
            lst.append(val)

    outs_p = [jnp.stack(vv, axis=0) for vv in p_new]
    outs_s = [jnp.stack(vv, axis=0) for vv in s_new]
    return (yp, ys.reshape(bs, 1, d), *outs_p, *outs_s)
```

```python
import functools
import math

import jax
import jax.numpy as jnp
import numpy as np
from jax import lax
from jax.experimental import pallas as pl
from jax.experimental.pallas import tpu as pltpu

f32 = jnp.float32
bf16 = jnp.bfloat16
HIGHEST = lax.Precision.HIGHEST

EPS = 1e-6
PAGE = 128
NH = 6
HD = 64
NOPE = 64
ROPE = 32
QK = NOPE + ROPE
SSD_W = NH * HD
SSD_G = 2
SSD_N = 128
SSD_CHUNK = 128
INPROJ_ROWS = 1024
FFN_ROWS = 512
ATTN_TILE = 512
SAMPLE_PAGES = 64
ROW_PIECE = 256
QK_AHEAD = 3
CONV_K = 4
CONV_CH = SSD_W + 2 * SSD_G * SSD_N
Q_LORA = 256
KV_LORA = 128
POOL_W = 256
POOL_BUF = 15
ROPE_THETA = 10000.0
ATTN_SCALE = QK ** -0.5
PROMPT_Q_SCALE = ATTN_SCALE * math.log2(math.e)
LANE = 128
HPAD = 16
VMEM_LIMIT = 56 * 1024 * 1024

_C_Z, _C_XBC, _C_CQ, _C_CKV, _C_U, _C_KA, _C_KB, _C_DT, _C_END = 0, 384, 1280, 1536, 1664, 1920, 2048, 2176, 2304


def _cparams(sem):
    return pltpu.CompilerParams(dimension_semantics=sem, vmem_limit_bytes=VMEM_LIMIT)


def _silu(x):
    return x * jax.nn.sigmoid(x)


def _softplus(x):
    return jnp.maximum(x, 0.0) + jnp.log1p(jnp.exp(-jnp.abs(x)))


def _rms(x, g, n):
    ms = jnp.sum(x * x, axis=-1, keepdims=True) * (1.0 / n)
    return x * lax.rsqrt(ms + EPS) * g


def _dot(a, b):
    return jnp.dot(a, b, preferred_element_type=f32)


def _dot_nt(a, b):
    return lax.dot_general(a, b, (((1,), (1,)), ((), ())), preferred_element_type=f32)


def _dot_tn(a, b):
    return lax.dot_general(a, b, (((0,), (0,)), ((), ())), preferred_element_type=f32)


def _dot_exact(a, b):
    return jnp.dot(a, b, precision=HIGHEST, preferred_element_type=f32)


def _head_expand():
    r = lax.broadcasted_iota(jnp.int32, (LANE, SSD_W), 0)
    c = lax.broadcasted_iota(jnp.int32, (LANE, SSD_W), 1)
    return jnp.where(c // HD == r, 1.0, 0.0).astype(bf16)


def _split3(a):
    hi = a.astype(bf16)
    r = a - hi.astype(f32)
    mid = r.astype(bf16)
    return hi, mid, (r - mid.astype(f32)).astype(bf16)


def _dot_sel_right(a, sel):
    hi, mid, lo = _split3(a)
    return _dot(hi, sel) + _dot(mid, sel) + _dot(lo, sel)


def _dot_sel_left(sel, a):
    hi, mid, lo = _split3(a)
    return _dot(sel, hi) + _dot(sel, mid) + _dot(sel, lo)


def _ada_kernel(c_ref, w_ref, b_ref, op_ref, os_ref):
    s = _silu(c_ref[...]).astype(bf16)
    r = _dot(s, w_ref[0].astype(bf16)) + b_ref[0]
    rp = op_ref.shape[2]
    op_ref[0, 0] = r[:rp]
    os_ref[0, 0] = r[rp:]


def _ada_mod(c_prompt, c_sample, w_ada, b_ada):
    depth, d, n6 = w_ada.shape
    nf = n6 // d
    bp, bs = c_prompt.shape[0], c_sample.shape[0]
    rp = -(-bp // 8) * 8
    c_all = jnp.concatenate([c_prompt, jnp.zeros((rp - bp, d), f32), c_sample], axis=0)
    return pl.pallas_call(
        _ada_kernel,
        grid=(depth, nf),
        in_specs=[pl.BlockSpec((rp + bs, d), lambda l, j: (0, 0)),
                  pl.BlockSpec((1, d, d), lambda l, j: (l, 0, j)),
                  pl.BlockSpec((1, 1, d), lambda l, j: (l, 0, j))],
        out_specs=[pl.BlockSpec((1, 1, rp, d), lambda l, j: (l, j, 0, 0)),
                   pl.BlockSpec((1, 1, bs, d), lambda l, j: (l, j, 0, 0))],
        out_shape=[jax.ShapeDtypeStruct((depth, nf, rp, d), f32), jax.ShapeDtypeStruct((depth, nf, bs, d), f32)],
        compiler_params=_cparams(("arbitrary", "arbitrary")),
        name="ada_mod",
    )(c_all, w_ada, b_ada.reshape(depth, 1, n6))


def _mod_rows(mod_ref, field, per_row, tm):
    if per_row:
        return mod_ref[0, field, pl.ds(pl.multiple_of(pl.program_id(1) * tm, tm), tm), :]
    return mod_ref[0, field, pl.ds(pl.program_id(0), 1), :]


def _inproj_kernel(*refs, with_kv, per_row):
    (x_ref, mod_ref, g1_ref, w_ref, gq_ref, wq_ref, cosq_ref, sinq_ref, gqk_ref, gkv_ref,
     cosk_ref, sink_ref) = refs[:12]
    if with_kv:
        wk_ref, gk_ref, wv_ref = refs[12:15]
        z_ref, xbc_ref, dt_ref, u_ref, lat_ref, kpe_ref, q_ref, k_ref, v_ref = refs[15:]
    else:
        z_ref, xbc_ref, dt_ref, u_ref, lat_ref, kpe_ref, q_ref, kpe128_ref = refs[12:]
    tm, d = x_ref.shape[1:]
    shift, scale = (_mod_rows(mod_ref, f, per_row, tm) for f in (0, 1))
    gqk = gqk_ref[...]
    q_scale = PROMPT_Q_SCALE if with_kv else 1.0
    nsplit = tm // ROW_PIECE if tm % ROW_PIECE == 0 else 1
    hm = tm // nsplit

    def rows_of(a, r):
        return a if a.shape[0] == 1 else a[r:r + hm]

    def project(r):
        h = _rms(x_ref[0, r:r + hm, :], g1_ref[...], d) * (1.0 + rows_of(scale, r)) + rows_of(shift, r)
        return _dot(h.astype(bf16), w_ref[0])

    def finish(r, proj):
        rs = slice(r, r + hm)
        z_ref[0, rs, :] = proj[:, _C_Z:_C_XBC]
        xbc_ref[0, rs, :] = proj[:, _C_XBC:_C_CQ]
        u_ref[0, rs, :] = proj[:, _C_U:_C_KA]
        dt_ref[0, rs, :] = proj[:, _C_DT:_C_END]
        cqn = _rms(proj[:, _C_CQ:_C_CKV], gq_ref[...], Q_LORA).astype(bf16)
        qq = _dot(cqn, wq_ref[0])
        cosq = rows_of(cosq_ref[...], r)
        sinq = rows_of(sinq_ref[...], r)
        for hh in range(NH):
            qh = qq[:, LANE * hh:LANE * (hh + 1)] * cosq + qq[:, NH * LANE + LANE * hh:NH * LANE + LANE * (hh + 1)] * sinq
            q_ref[0, rs, LANE * hh:LANE * (hh + 1)] = (_rms(qh, gqk, QK) * q_scale).astype(q_ref.dtype)
        lat = _rms(proj[:, _C_CKV:_C_U], gkv_ref[...], KV_LORA)
        lat_ref[0, rs, :] = lat
        kper = proj[:, _C_KA:_C_KB] * rows_of(cosk_ref[...], r) + proj[:, _C_KB:_C_DT] * rows_of(sink_ref[...], r)
        kpe_ref[0, rs, :] = kper[:, :ROPE]
        if with_kv:
            latb = lat.astype(bf16)
            kn = _dot(latb, wk_ref[...])
            kp = pltpu.roll(kper, NOPE, 1)
            gk = gk_ref[...]
            for hh in range(NH):
                kh = kn[:, LANE * hh:LANE * (hh + 1)] + kp
                k_ref[0, rs, LANE * hh:LANE * (hh + 1)] = _rms(kh, gk, QK).astype(bf16)
            v_ref[0, :, rs] = _dot_nt(wv_ref[...], latb).astype(bf16)
        else:
            kpe128_ref[0, rs, :] = kper

    nxt = project(0)
    for i in range(nsplit):
        proj = nxt
        if i + 1 < nsplit:
            nxt = project((i + 1) * hm)
        finish(i * hm, proj)


def _inproj(x, mod, wts, tabs, *, li, tm, with_kv, per_row):
    g_, t_, d = x.shape
    nt = t_ // tm
    cosq, sinq, cosk, sink = tabs
    tab_rows = cosq.shape[0] != 1
    tab_spec = pl.BlockSpec((tm if tab_rows else 1, LANE), (lambda g, t: (t, 0)) if tab_rows else (lambda g, t: (0, 0)))

    def full(a):
        return pl.BlockSpec(a.shape, lambda g, t: (0,) * a.ndim)

    def layer(a):
        return pl.BlockSpec((1,) + a.shape[1:], lambda g, t: (li,) + (0,) * (a.ndim - 1))

    def row(c):
        return pl.BlockSpec((1, tm, c), lambda g, t: (g, t, 0))

    ins = [x, mod, wts["g1"], wts["w_in"], wts["g_q"], wts["w_q"], cosq, sinq, wts["g_qk_q"], wts["g_kv"], cosk, sink]
    specs = [row(d), layer(mod), full(wts["g1"]), layer(wts["w_in"]), full(wts["g_q"]), layer(wts["w_q"]), tab_spec,
             tab_spec, full(wts["g_qk_q"]), full(wts["g_kv"]), tab_spec, tab_spec]
    widths = [SSD_W, CONV_CH, LANE, POOL_W, KV_LORA, ROPE, NH * LANE]
    dtypes = [f32, f32, f32, f32, f32, f32, bf16 if with_kv else f32]
    if with_kv:
        ins += [wts["w_k"], wts["g_qk_k"], wts["w_v"]]
        specs += [full(wts["w_k"]), full(wts["g_qk_k"]), full(wts["w_v"])]
        widths += [NH * LANE, SSD_W]
        dtypes += [bf16, bf16]
    else:
        widths += [LANE]
        dtypes += [f32]
    out_specs = [row(c) for c in widths]
    out_shape = [jax.ShapeDtypeStruct((g_, t_, c), dt) for c, dt in zip(widths, dtypes)]
    if with_kv:
        out_specs[-1] = pl.BlockSpec((1, SSD_W, tm), lambda g, t: (g, 0, t))
        out_shape[-1] = jax.ShapeDtypeStruct((g_, SSD_W, t_), bf16)
    return pl.pallas_call(
        functools.partial(_inproj_kernel, with_kv=with_kv, per_row=per_row),
        grid=(g_, nt),
        in_specs=specs,
        out_specs=out_specs,
        out_shape=out_shape,
        compiler_params=_cparams(("arbitrary", "arbitrary")),
        name="inproj_kv" if with_kv else "inproj",
    )(*ins)


def _ssd_prompt_kernel(xbc_ref, z_ref, dt_ref, cw_ref, cb_ref, dtb_ref, a_ref, dsk_ref, gn_ref, tril_ref, exp_ref,
                       y_ref, st_ref, ext_ref):
    c = pl.program_id(1)
    blk = xbc_ref.shape[1]
    t_ = SSD_CHUNK

    @pl.when(c == 0)
    def _():
        ext_ref[0:8, :] = jnp.zeros((8, CONV_CH), f32)
        st_ref[...] = jnp.zeros(st_ref.shape, f32)

    ext_ref[8:8 + blk, :] = xbc_ref[0]
    cw = cw_ref[...]
    tril = lax.broadcasted_iota(jnp.int32, (t_, t_), 0) >= lax.broadcasted_iota(jnp.int32, (t_, t_), 1)
    first = lax.broadcasted_iota(jnp.int32, (t_, LANE), 1) < HD
    rfirst = lax.broadcasted_iota(jnp.int32, (LANE, SSD_N), 0) < HD
    expand = exp_ref[...]

    def front(r):
        conv = cb_ref[...] + cw[3:4] * xbc_ref[0, r:r + t_, :]
        for k in range(CONV_K - 1):
            conv = conv + cw[k:k + 1] * ext_ref[5 + k + r:5 + k + r + t_, :]
        act = _silu(conv)
        xs = act[:, :SSD_W]
        bm = [act[:, SSD_W + SSD_N * g:SSD_W + SSD_N * (g + 1)].astype(bf16) for g in range(SSD_G)]
        cm = [act[:, SSD_W + SSD_N * (SSD_G + g):SSD_W + SSD_N * (SSD_G + g + 1)].astype(bf16) for g in range(SSD_G)]
        dt = _softplus(dt_ref[0, r:r + t_, :] + dtb_ref[...])
        a = dt * a_ref[...]
        acum = _dot_sel_left(tril_ref[...], a)
        acum_t = acum.T
        dt_l = _dot_sel_right(dt, expand)
        acum_l = _dot_sel_right(acum, expand)
        e_l = jnp.exp(acum_l)
        xdt = xs * dt_l
        xdt_w = xdt * jnp.exp(acum_l[t_ - 1:t_, :] - acum_l)
        cb = [_dot_nt(cm[g], bm[g]) for g in range(SSD_G)]
        cols = []
        for j in range(NH // 2):
            sl = slice(LANE * j, LANE * (j + 1))
            h0, h1 = 2 * j, 2 * j + 1
            xj = xdt[:, sl].astype(bf16)
            yd = []
            for hh in (h0, h1):
                seg = jnp.where(tril, jnp.exp(acum[:, hh:hh + 1] - acum_t[hh:hh + 1, :]), 0.0)
                yd.append(_dot((cb[hh // (NH // SSD_G)] * seg).astype(bf16), xj))
            tot = jnp.where(rfirst, jnp.exp(acum[t_ - 1:t_, h0:h0 + 1]), jnp.exp(acum[t_ - 1:t_, h1:h1 + 1]))
            cols.append((jnp.where(first, yd[0], yd[1]), xdt_w[:, sl].astype(bf16), tot, e_l[:, sl]))
        return xs, bm, cm, cols, _silu(z_ref[0, r:r + t_, :])

    def back(r, parts):
        xs, bm, cm, cols, gate = parts
        y_cols = []
        for j, (y_diag, xw, tot, e_j) in enumerate(cols):
            sl = slice(LANE * j, LANE * (j + 1))
            g0, g1 = (2 * j) // (NH // SSD_G), (2 * j + 1) // (NH // SSD_G)
            hp = st_ref[0, sl, :]
            hpb = hp.astype(bf16)
            if g0 == g1:
                y_off = _dot_nt(cm[g0], hpb)
                s_new = _dot_tn(xw, bm[g0])
            else:
                y_off = jnp.where(first, _dot_nt(cm[g0], hpb), _dot_nt(cm[g1], hpb))
                s_new = jnp.where(rfirst, _dot_tn(xw, bm[g0]), _dot_tn(xw, bm[g1]))
            st_ref[0, sl, :] = tot * hp + s_new
            y_cols.append(y_diag + y_off * e_j)
        v = (jnp.concatenate(y_cols, axis=-1) + dsk_ref[...] * xs) * gate
        gl = lax.broadcasted_iota(jnp.int32, (t_, SSD_W), 1) < SSD_W // SSD_G
        v2 = v * v
        ss0 = jnp.sum(jnp.where(gl, v2, 0.0), axis=-1, keepdims=True)
        ss1 = jnp.sum(jnp.where(gl, 0.0, v2), axis=-1, keepdims=True)
        gw = SSD_W // SSD_G
        rinv = jnp.where(gl, lax.rsqrt(ss0 * (1.0 / gw) + EPS), lax.rsqrt(ss1 * (1.0 / gw) + EPS))
        y_ref[0, r:r + t_, :] = v * rinv * gn_ref[...]

    fronts = [front(r) for r in range(0, blk, t_)]
    for i, parts in enumerate(fronts):
        back(i * t_, parts)
    ext_ref[0:8, :] = xbc_ref[0, blk - 8:, :]


def _ssd_prompt(xbc, z, dt, wts, *, blk=4 * SSD_CHUNK):
    b_, l_, _ = xbc.shape
    t_ = SSD_CHUNK
    blk = blk if l_ % blk == 0 else t_
    nc = l_ // blk

    def full(a):
        return pl.BlockSpec(a.shape, lambda b, c: (0,) * a.ndim)

    def row(w):
        return pl.BlockSpec((1, blk, w), lambda b, c: (b, c, 0))

    small = [wts["conv_w"], wts["conv_b"], wts["dt_bias"], wts["a_neg"], wts["d_skip_l"], wts["g_ssd"],
             jnp.tril(jnp.ones((t_, t_), bf16)), _head_expand()]
    return pl.pallas_call(
        _ssd_prompt_kernel,
        grid=(b_, nc),
        in_specs=[row(CONV_CH), row(SSD_W), row(LANE)] + [full(a) for a in small],
        out_specs=[row(SSD_W), pl.BlockSpec((1, SSD_W, SSD_N), lambda b, c: (b, 0, 0))],
        out_shape=[jax.ShapeDtypeStruct((b_, l_, SSD_W), f32), jax.ShapeDtypeStruct((b_, SSD_W, SSD_N), f32)],
        scratch_shapes=[pltpu.VMEM((8 + blk, CONV_CH), f32)],
        compiler_params=_cparams(("arbitrary", "arbitrary")),
        name="ssd_prompt",
    )(xbc, z, dt, *small)


def _attn_prompt_kernel(q_ref, k_ref, vt_ref, g_ref, o_ref, m_ref, l_ref, acc_ref, *, tq):
    qi = pl.program_id(1)
    m_ref[...] = jnp.full(m_ref.shape, -1e30, f32)
    l_ref[...] = jnp.zeros(l_ref.shape, f32)
    acc_ref[...] = jnp.zeros(acc_ref.shape, f32)

    ones = jnp.ones((16, tq), bf16)
    masked = lax.broadcasted_iota(jnp.int32, (tq, tq), 0) > lax.broadcasted_iota(jnp.int32, (tq, tq), 1)

    def tiles(key_tiles):
        blocks = []
        for j, diagonal in key_tiles:
            off = pl.multiple_of(j * tq, tq)
            blocks.append((k_ref[0, pl.ds(off, tq), :], vt_ref[0, :, pl.ds(off, tq)], diagonal))
        units = [(bi, hh) for bi in range(len(blocks)) for hh in range(NH)]

        def qk(unit):
            bi, hh = unit
            return _dot_nt(blocks[bi][0][:, LANE * hh:LANE * (hh + 1)], q_ref[0, :, LANE * hh:LANE * (hh + 1)])

        pending = [qk(u) for u in units[:QK_AHEAD]]
        for idx, (bi, hh) in enumerate(units):
            s = pending.pop(0)
            if idx + QK_AHEAD < len(units):
                pending.append(qk(units[idx + QK_AHEAD]))
            _, vblk, diagonal = blocks[bi]
            if diagonal:
                s = jnp.where(masked, -1e30, s)
            m_old = m_ref[hh]
            m_new = jnp.maximum(m_old, jnp.max(s, axis=0, keepdims=True))
            alpha = jnp.exp2(m_old - m_new)
            pe = jnp.exp2(s - m_new).astype(bf16)
            m_ref[hh] = m_new
            rows = slice(HD * hh, HD * (hh + 1))
            pv = _dot(jnp.concatenate([vblk[rows, :], ones], axis=0), pe)
            acc_ref[rows, :] = acc_ref[rows, :] * alpha + pv[:HD]
            l_ref[hh] = alpha * l_ref[hh] + pv[HD:HD + 8]

    def pair(j2, carry):
        tiles([(2 * j2, False), (2 * j2 + 1, False)])
        return carry

    lax.fori_loop(0, qi // 2, pair, 0)

    @pl.when(qi % 2 == 1)
    def _():
        tiles([(qi - 1, False), (qi, True)])

    @pl.when(qi % 2 == 0)
    def _():
        tiles([(qi, True)])

    parts = []
    for hh in range(NH):
        parts.append(acc_ref[HD * hh:HD * (hh + 1), :] * (1.0 / l_ref[hh, 0:1, :]))
    o_t = jnp.concatenate(parts, axis=0)
    ms = jnp.sum(o_t * o_t, axis=0, keepdims=True) * (1.0 / SSD_W)
    o_ref[0] = (o_t * lax.rsqrt(ms + EPS)).T * g_ref[...]


def _attn_prompt(q, k, v_t, g_out, *, tq):
    b_, l_, _ = q.shape
    return pl.pallas_call(
        functools.partial(_attn_prompt_kernel, tq=tq),
        grid=(b_, l_ // tq),
        in_specs=[pl.BlockSpec((1, tq, NH * LANE), lambda b, i: (b, i, 0)),
                  pl.BlockSpec((1, l_, NH * LANE), lambda b, i: (b, 0, 0)),
                  pl.BlockSpec((1, SSD_W, l_), lambda b, i: (b, 0, 0)),
                  pl.BlockSpec(g_out.shape, lambda b, i: (0, 0))],
        out_specs=pl.BlockSpec((1, tq, SSD_W), lambda b, i: (b, i, 0)),
        out_shape=jax.ShapeDtypeStruct((b_, l_, SSD_W), f32),
        scratch_shapes=[pltpu.VMEM((NH, 1, tq), f32), pltpu.VMEM((NH, 8, tq), f32), pltpu.VMEM((SSD_W, tq), f32)],
        compiler_params=_cparams(("arbitrary", "arbitrary")),
        name="attn_prompt",
    )(q, k, v_t, g_out)


def _pool_select(lane, a, b, c, d):
    return jnp.where(lane < 64, a, jnp.where(lane < 128, b, jnp.where(lane < 192, c, d)))


def _pool_prompt_kernel(u_ref, w_ref, sc_ref, o_ref, e1, e2, e4, e8, *, rt):
    t_ = u_ref.shape[1]
    hist = 16
    for e in (e1, e2, e4, e8):
        e[0:hist, :] = jnp.zeros((hist, POOL_W), f32)
    e1[hist:hist + t_, :] = u_ref[0]
    lane = lax.broadcasted_iota(jnp.int32, (rt, POOL_W), 1)
    win = _pool_select(lane, 2, 4, 8, 16)
    for i in range(t_ // rt):
        r0 = hist + i * rt
        a = e1[r0:r0 + rt, :]
        s2 = a + e1[r0 - 1:r0 - 1 + rt, :]
        e2[r0:r0 + rt, :] = s2
        s4 = s2 + e2[r0 - 2:r0 - 2 + rt, :]
        e4[r0:r0 + rt, :] = s4
        s8 = s4 + e4[r0 - 4:r0 - 4 + rt, :]
        e8[r0:r0 + rt, :] = s8
        s16 = s8 + e8[r0 - 8:r0 - 8 + rt, :]
        pos = lax.broadcasted_iota(jnp.int32, (rt, POOL_W), 0) + i * rt
        cnt = jnp.minimum(pos + 1, win).astype(f32)
        pooled = _pool_select(lane, s2, s4, s8, s16) / cnt - a
        o_ref[0, i * rt:(i + 1) * rt, :] = _dot(pooled.astype(bf16), w_ref[...]) * sc_ref[...]


def _pool_prompt(u, w_bd, scale):
    b_, l_, _ = u.shape
    return pl.pallas_call(
        functools.partial(_pool_prompt_kernel, rt=256),
        grid=(b_,),
        in_specs=[pl.BlockSpec((1, l_, POOL_W), lambda b: (b, 0, 0)),
                  pl.BlockSpec(w_bd.shape, lambda b: (0, 0)),
                  pl.BlockSpec(scale.shape, lambda b: (0, 0))],
        out_specs=pl.BlockSpec((1, l_, POOL_W), lambda b: (b, 0, 0)),
        out_shape=jax.ShapeDtypeStruct((b_, l_, POOL_W), f32),
        scratch_shapes=[pltpu.VMEM((16 + l_, POOL_W), f32)] * 4,
        compiler_params=_cparams(("arbitrary",)),
        name="pool_prompt",
    )(u, w_bd, scale)


def _pool_sample_kernel(u_ref, buf_ref, w_ref, sc_ref, o_ref):
    u = u_ref[...]
    run = u
    sums = {}
    for i in range(1, 16):
        run = run + buf_ref[POOL_BUF - i]
        if i + 1 in (2, 4, 8, 16):
            sums[i + 1] = run * (1.0 / (i + 1))
    lane = lax.broadcasted_iota(jnp.int32, u.shape, 1)
    pooled = _pool_select(lane, sums[2], sums[4], sums[8], sums[16]) - u
    o_ref[...] = _dot(pooled.astype(bf16), w_ref[...]) * sc_ref[...]


def _pool_sample(u, buf_t, w_bd, scale):
    return pl.pallas_call(
        _pool_sample_kernel,
        out_shape=jax.ShapeDtypeStruct(u.shape, f32),
        compiler_params=pltpu.CompilerParams(vmem_limit_bytes=VMEM_LIMIT),
        name="pool_sample",
    )(u, buf_t, w_bd, scale)


def _ffn_kernel(x_ref, ssd_ref, mla_ref, pool_ref, mod_ref, gn_ref, wo_ref, wg_ref, wu_ref, wd_ref, o_ref,
                *, nch, per_row):
    tm, d = x_ref.shape[1:]
    gate1, shift2, scale2, gate2 = (_mod_rows(mod_ref, f, per_row, tm) for f in (2, 3, 4, 5))
    hc = wg_ref.shape[2] // nch
    nsplit = tm // ROW_PIECE if tm % ROW_PIECE == 0 else 1
    hm = tm // nsplit

    def rows_of(a, r):
        return a if a.shape[0] == 1 else a[r:r + hm]

    def out_proj(r):
        rs = slice(r, r + hm)
        mixed = jnp.concatenate([ssd_ref[0, rs, :].astype(bf16), mla_ref[0, rs, :].astype(bf16),
                                 pool_ref[0, rs, :].astype(bf16)], axis=-1)
        return _dot(mixed, wo_ref[0])

    def residual_norm(r, mix):
        x1 = x_ref[0, r:r + hm, :] + rows_of(gate1, r) * mix
        return x1, (_rms(x1, gn_ref[...], d) * (1.0 + rows_of(scale2, r)) + rows_of(shift2, r)).astype(bf16)

    def gate_up(h2, c):
        return _dot(h2, wg_ref[0, :, hc * c:hc * (c + 1)]), _dot(h2, wu_ref[0, :, hc * c:hc * (c + 1)])

    mixes = [out_proj(i * hm) for i in range(nsplit)]
    x1, h2 = residual_norm(0, mixes[0])
    for i in range(nsplit):
        r = i * hm
        acc = jnp.zeros(x1.shape, f32)
        nxt = gate_up(h2, 0)
        if i + 1 < nsplit:
            x1_next, h2_next = residual_norm(r + hm, mixes[i + 1])
        for c in range(nch):
            gate, up = nxt
            if c + 1 < nch:
                nxt = gate_up(h2, c + 1)
            acc = acc + _dot((_silu(gate) * up).astype(bf16), wd_ref[0, hc * c:hc * (c + 1), :])
        o_ref[0, r:r + hm, :] = x1 + rows_of(gate2, r) * acc
        if i + 1 < nsplit:
            x1, h2 = x1_next, h2_next


def _ffn(x, ssd, mla, pool, mod, wts, *, li, tm, per_row):
    g_, t_, d = x.shape

    def layer(a):
        return pl.BlockSpec((1,) + a.shape[1:], lambda g, t: (li,) + (0,) * (a.ndim - 1), pipeline_mode=pl.Buffered(1))

    def row(c):
        return pl.BlockSpec((1, tm, c), lambda g, t: (g, t, 0))

    ws = [wts["w_out"], wts["w_gate"], wts["w_up"], wts["w_down"]]
    return pl.pallas_call(
        functools.partial(_ffn_kernel, nch=11, per_row=per_row),
        grid=(g_, t_ // tm),
        in_specs=[row(d), row(SSD_W), row(SSD_W), row(POOL_W), layer(mod),
                  pl.BlockSpec(wts["g2"].shape, lambda g, t: (0, 0))] + [layer(a) for a in ws],
        out_specs=row(d),
        out_shape=jax.ShapeDtypeStruct((g_, t_, d), f32),
        compiler_params=_cparams(("arbitrary", "arbitrary")),
        name="outproj_ffn",
    )(x, ssd, mla, pool, mod, wts["g2"], *ws)


def _ssd_sample_pre_kernel(xbc_ref, buf_ref, dt_ref, cw_ref, cb_ref, dtb_ref, a_ref,
                           xs_ref, b_ref, c_ref, xdt_t_ref, da_ref):
    cw = cw_ref[...]
    conv = cb_ref[...] + cw[3:4] * xbc_ref[...]
    for k in range(CONV_K - 1):
        conv = conv + cw[k:k + 1] * buf_ref[k]
    act = _silu(conv)
    xs = act[:, :SSD_W]
    xs_ref[...] = xs
    b_ref[...] = act[:, SSD_W:SSD_W + SSD_G * SSD_N]
    c_ref[...] = act[:, SSD_W + SSD_G * SSD_N:]
    dt = _softplus(dt_ref[...] + dtb_ref[...])
    da_ref[...] = jnp.exp(dt * a_ref[...])
    xdt = xs * _dot_sel_right(dt, _head_expand())
    for j in range(SSD_W // LANE):
        xdt_t_ref[LANE * j:LANE * (j + 1), :] = xdt[:, LANE * j:LANE * (j + 1)].T


def _ssd_sample_pre(xbc, buf_t, dt, wts):
    b_ = xbc.shape[0]
    shapes = [(b_, SSD_W), (b_, SSD_G * SSD_N), (b_, SSD_G * SSD_N), (SSD_W, b_), (b_, LANE)]
    return pl.pallas_call(
        _ssd_sample_pre_kernel,
        out_shape=[jax.ShapeDtypeStruct(s, f32) for s in shapes],
        compiler_params=pltpu.CompilerParams(vmem_limit_bytes=VMEM_LIMIT),
        name="ssd_sample_pre",
    )(xbc, buf_t, dt, wts["conv_w"], wts["conv_b"], wts["dt_bias"], wts["a_neg"])


def _ssd_sample_state_kernel(da_ref, st_ref, xdt_t_ref, b_ref, c_ref, new_ref, y_t_ref, *, tb):
    i = pl.program_id(0)

    @pl.when(i == 0)
    def _():
        y_t_ref[...] = jnp.zeros(y_t_ref.shape, f32)

    nb = xdt_t_ref.shape[1]
    lane = lax.broadcasted_iota(jnp.int32, (HD, nb), 1)
    pairs = [(hh, bl) for hh in range(NH) for bl in range(tb)]
    sels = [lane == i * tb + bl for bl in range(tb)]
    cols = {}
    for hh, bl in pairs:
        cols[hh, bl] = jnp.sum(jnp.where(sels[bl], xdt_t_ref[HD * hh:HD * (hh + 1), :], 0.0), axis=1, keepdims=True)
    ycols = {}
    for hh, bl in pairs:
        g = hh // (NH // SSD_G)
        rs = slice(HD * hh, HD * (hh + 1))
        brow = b_ref[bl:bl + 1, SSD_N * g:SSD_N * (g + 1)]
        crow = c_ref[bl:bl + 1, SSD_N * g:SSD_N * (g + 1)]
        new = da_ref[i * tb + bl, hh] * st_ref[bl, rs, :] + cols[hh, bl] * brow
        new_ref[bl, rs, :] = new
        ycols[hh, bl] = jnp.sum(new * crow, axis=1, keepdims=True)
    for hh in range(NH):
        rs = slice(HD * hh, HD * (hh + 1))
        y_rows = y_t_ref[rs, :]
        for bl in range(tb):
            y_rows = jnp.where(sels[bl], ycols[hh, bl], y_rows)
        y_t_ref[rs, :] = y_rows


def _ssd_sample_state(da, state_all, xdt_t, bm, cm, *, li, tb=8):
    b_ = xdt_t.shape[1]
    nblk = b_ // tb
    return pl.pallas_call(
        functools.partial(_ssd_sample_state_kernel, tb=tb),
        grid=(nblk,),
        in_specs=[pl.BlockSpec(memory_space=pltpu.SMEM),
                  pl.BlockSpec((tb, SSD_W, SSD_N), lambda i: (li * nblk + i, 0, 0)),
                  pl.BlockSpec(xdt_t.shape, lambda i: (0, 0)),
                  pl.BlockSpec((tb, SSD_G * SSD_N), lambda i: (i, 0)),
                  pl.BlockSpec((tb, SSD_G * SSD_N), lambda i: (i, 0))],
        out_specs=[pl.BlockSpec((tb, SSD_W, SSD_N), lambda i: (i, 0, 0)),
                   pl.BlockSpec((SSD_W, b_), lambda i: (0, 0))],
        out_shape=[jax.ShapeDtypeStruct((b_, SSD_W, SSD_N), f32), jax.ShapeDtypeStruct((SSD_W, b_), f32)],
        compiler_params=_cparams(("arbitrary",)),
        name="ssd_sample_state",
    )(da, state_all, xdt_t, bm, cm)


def _ssd_sample_post_kernel(y_ref, xs_ref, z_ref, dsk_ref, gn_ref, o_ref):
    y = y_ref[...] + dsk_ref[...] * xs_ref[...]
    v = y * _silu(z_ref[...])
    gl = lax.broadcasted_iota(jnp.int32, v.shape, 1) < SSD_W // SSD_G
    v2 = v * v
    gw = SSD_W // SSD_G
    ss0 = jnp.sum(jnp.where(gl, v2, 0.0), axis=-1, keepdims=True)
    ss1 = jnp.sum(jnp.where(gl, 0.0, v2), axis=-1, keepdims=True)
    rinv = jnp.where(gl, lax.rsqrt(ss0 * (1.0 / gw) + EPS), lax.rsqrt(ss1 * (1.0 / gw) + EPS))
    o_ref[...] = v * rinv * gn_ref[...]


def _ssd_sample_post(y, xs, z, wts):
    return pl.pallas_call(
        _ssd_sample_post_kernel,
        out_shape=jax.ShapeDtypeStruct(y.shape, f32),
        name="ssd_sample_post",
    )(y, xs, z, wts["d_skip_l"], wts["g_ssd"])


def _mla_sample_pre_kernel(q_ref, lat_ref, kpe_ref, wk_ref, wkt_ref, gk_ref, qabs_ref, qpe_ref, snew_ref):
    kn = _dot(lat_ref[...].astype(bf16), wk_ref[...])
    kp = pltpu.roll(kpe_ref[...], NOPE, 1)
    gk = gk_ref[...]
    lane = lax.broadcasted_iota(jnp.int32, snew_ref.shape, 1)
    snew = jnp.zeros(snew_ref.shape, f32)
    for hh in range(NH):
        sl = slice(LANE * hh, LANE * (hh + 1))
        qh = q_ref[:, sl]
        knew = _rms(kn[:, sl] + kp, gk, QK)
        snew = jnp.where(lane == hh, jnp.sum(qh * knew, axis=-1, keepdims=True) * ATTN_SCALE, snew)
        qg = qh * gk
        qabs_ref[:, sl] = _dot_exact(qg, wkt_ref[hh])
        qpe_ref[:, sl] = pltpu.roll(qg, NOPE, 1)
    snew_ref[...] = snew


def _mla_sample_pre(q, lat, kpe128, wts):
    b_ = q.shape[0]
    shapes = [(b_, NH * LANE), (b_, NH * LANE), (b_, LANE)]
    return pl.pallas_call(
        _mla_sample_pre_kernel,
        out_shape=[jax.ShapeDtypeStruct(s, f32) for s in shapes],
        name="mla_sample_pre",
    )(q, lat, kpe128, wts["w_k"], wts["w_kt"], wts["g_qk_k"])


def _attn_sample_kernel(pt_ref, lat_hbm, kpe_hbm, wkt_ref, qabs_ref, qpe_ref, snew_ref, latnew_ref, o_ref,
                        lat_buf, kpe_buf, sem, wext, latb_ref, s_ref, *, li, pages, nchunk):
    b = pl.program_id(0)
    nb = pl.num_programs(0)
    rc = pages * PAGE
    slot = b % 2

    def aligned(idx, size):
        return idx * size if isinstance(idx, int) else pl.multiple_of(idx * size, size)

    def copies(bb, c, i, sl):
        page_idx = c * pages + i
        page = pt_ref[bb, page_idx]
        off = aligned(page_idx, PAGE)
        return (pltpu.make_async_copy(lat_hbm.at[li, page], lat_buf.at[sl, pl.ds(off, PAGE), :], sem.at[0, sl, c]),
                pltpu.make_async_copy(kpe_hbm.at[li, page], kpe_buf.at[sl, :, pl.ds(off, PAGE)], sem.at[1, sl, c]))

    def start_chunk(bb, c, sl):
        for i in range(pages):
            for cp in copies(bb, c, i, sl):
                cp.start()

    @pl.when(b == 0)
    def _():
        def issue(c, carry):
            start_chunk(0, c, 0)
            start_chunk(jnp.minimum(1, nb - 1), c, 1)
            return carry
        lax.fori_loop(0, nchunk, issue, 0)
        wext[0:NH * HD, :] = wkt_ref[...]

    wext[NH * HD:NH * HD + HPAD, :] = qabs_ref[0]
    qpe = qpe_ref[0]
    rowi = lax.broadcasted_iota(jnp.int32, (HPAD, rc), 0)

    def scores(c, carry):
        for i in range(pages):
            for cp in copies(b, c, i, slot):
                cp.wait()
        off = aligned(c, rc)
        latb = lat_buf[slot, pl.ds(off, rc), :].astype(bf16)
        latb_ref[pl.ds(off, rc), :] = latb
        kpe = kpe_buf[slot, :, pl.ds(off, rc)]
        a = _dot_nt(wext[...], latb)
        nsq = jnp.zeros((HPAD, rc), f32)
        for hh in range(NH):
            kh = a[HD * hh:HD * (hh + 1), :]
            nsq = jnp.where(rowi == hh, jnp.sum(kh * kh, axis=0, keepdims=True), nsq)
        s_pe = _dot(qpe, kpe.astype(bf16))
        ksq = jnp.sum(kpe * kpe, axis=0, keepdims=True)
        s_ref[:, pl.ds(off, rc)] = (a[NH * HD:, :] + s_pe) * lax.rsqrt((nsq + ksq) * (1.0 / QK) + EPS) * ATTN_SCALE
        return carry

    lax.fori_loop(0, nchunk, scores, 0)

    b_ahead = jnp.minimum(b + 2, nb - 1)
    for c in range(nchunk):
        start_chunk(b_ahead, c, slot)

    s = s_ref[...]
    s_new = snew_ref[0][:, 0:1]
    m = jnp.maximum(jnp.max(s, axis=-1, keepdims=True), s_new)
    pe = jnp.exp(s - m)
    p_new = jnp.exp(s_new - m)
    l = jnp.sum(pe, axis=-1, keepdims=True) + p_new
    acc = _dot(pe.astype(bf16), latb_ref[...]) + p_new * latnew_ref[0]
    o_ref[0] = acc / l

    @pl.when(b == nb - 1)
    def _():
        def drain(c, carry):
            for sl in range(2):
                for i in range(pages):
                    for cp in copies(b, c, i, sl):
                        cp.wait()
            return carry
        lax.fori_loop(0, nchunk, drain, 0)


def _attn_sample(page_table, cache_lat, cache_kpe_t, wkt, qabs, qpe, snew, lat_new, *, li, pages=SAMPLE_PAGES):
    b_, npages = page_table.shape
    nchunk = npages // pages
    assert npages % pages == 0
    seq = npages * PAGE
    grid_spec = pltpu.PrefetchScalarGridSpec(
        num_scalar_prefetch=1,
        grid=(b_,),
        in_specs=[pl.BlockSpec(memory_space=pl.ANY),
                  pl.BlockSpec(memory_space=pl.ANY),
                  pl.BlockSpec(wkt.shape, lambda b, pt: (0, 0)),
                  pl.BlockSpec((1, HPAD, KV_LORA), lambda b, pt: (b, 0, 0)),
                  pl.BlockSpec((1, HPAD, ROPE), lambda b, pt: (b, 0, 0)),
                  pl.BlockSpec((1, HPAD, LANE), lambda b, pt: (b, 0, 0)),
                  pl.BlockSpec((1, 1, KV_LORA), lambda b, pt: (b, 0, 0))],
        out_specs=pl.BlockSpec((1, HPAD, KV_LORA), lambda b, pt: (b, 0, 0)),
        scratch_shapes=[pltpu.VMEM((2, seq, KV_LORA), f32),
                        pltpu.VMEM((2, ROPE, seq), f32),
                        pltpu.SemaphoreType.DMA((2, 2, nchunk)),
                        pltpu.VMEM((NH * HD + HPAD, KV_LORA), bf16),
                        pltpu.VMEM((seq, KV_LORA), bf16),
                        pltpu.VMEM((HPAD, seq), f32)],
    )
    return pl.pallas_call(
        functools.partial(_attn_sample_kernel, li=li, pages=pages, nchunk=nchunk),
        grid_spec=grid_spec,
        out_shape=jax.ShapeDtypeStruct((b_, HPAD, KV_LORA), f32),
        compiler_params=_cparams(("arbitrary",)),
        name="attn_sample",
    )(page_table, cache_lat, cache_kpe_t, wkt, qabs, qpe, snew, lat_new)


def _mla_sample_post_kernel(o_ref, wv_ref, g_ref, out_ref):
    out_ref[...] = _rms(_dot(o_ref[...].astype(bf16), wv_ref[...]), g_ref[...], SSD_W)


def _mla_sample_post(o_lat, w_v_bd, g_out):
    return pl.pallas_call(
        _mla_sample_post_kernel,
        out_shape=jax.ShapeDtypeStruct((o_lat.shape[0], SSD_W), f32),
        name="mla_sample_post",
    )(o_lat, w_v_bd, g_out)


def _pad_heads(w, lo, hi):
    pad = [(0, 0)] * (w.ndim - 1) + [(lo, LANE - hi)]
    w = jnp.pad(w, pad)
    return w.reshape(w.shape[:-2] + (NH * LANE,))


def _rot_cols(w):
    half = ROPE // 2
    return jnp.concatenate([-w[..., half:], w[..., :half]], axis=-1)


def _pad_lanes(a, lo=0):
    return jnp.pad(a, [(0, 0)] * (a.ndim - 1) + [(lo, LANE - lo - a.shape[-1])])


def _stacked_weights(p):
    offs = np.cumsum([SSD_W, CONV_CH, NH, Q_LORA, KV_LORA, ROPE, POOL_W])
    wz, wxbc, wdt, wcq, wckv, wkpe, wu = jnp.split(p["w_in"], offs[:-1].tolist(), axis=2)
    w_in = jnp.concatenate([wz, wxbc, wcq, wckv, wu, _pad_lanes(wkpe), _pad_lanes(_rot_cols(wkpe)), _pad_lanes(wdt)],
                           axis=2).astype(bf16)
    wq = p["w_q_up"]
    w_q = jnp.concatenate([_pad_heads(wq, 0, QK), _pad_heads(_rot_cols(wq[..., NOPE:]), NOPE, QK)], axis=2).astype(bf16)
    return dict(w_in=w_in, w_q=w_q, w_out=p["w_out"].astype(bf16), w_gate=p["w_gate"].astype(bf16),
                w_up=p["w_up"].astype(bf16), w_down=p["w_down"].astype(bf16))


def _layer_weights(p, li):
    d = p["w_in"].shape[1]
    wk = p["w_k_up"][li]
    wk_pad = _pad_heads(wk, 0, NOPE)
    wk_t = jnp.transpose(wk, (1, 2, 0))
    wv = p["w_v_up"][li]
    w_v_bd = jnp.zeros((NH, KV_LORA, NH, HD), f32)
    w_v_bd = w_v_bd.at[jnp.arange(NH), :, jnp.arange(NH), :].set(jnp.transpose(wv, (1, 0, 2)))
    wp = p["w_pool"][li]
    ng = wp.shape[0]
    w_pool_bd = jnp.zeros((ng, HD, ng, HD), f32).at[jnp.arange(ng), :, jnp.arange(ng), :].set(wp)
    return dict(
        g1=p["g_norm1"][li].reshape(1, d),
        g_q=p["g_q_lora"][li].reshape(1, Q_LORA),
        g_qk_q=_pad_lanes(p["g_qk_q"][li].reshape(1, QK)), g_qk_k=_pad_lanes(p["g_qk_k"][li].reshape(1, QK)),
        g_kv=p["g_kv_lora"][li].reshape(1, KV_LORA),
        w_k=wk_pad.astype(bf16), w_v=wv.reshape(KV_LORA, NH * HD).T.astype(bf16),
        w_kt=jnp.pad(wk_t, ((0, 0), (0, LANE - NOPE), (0, 0))),
        w_kt_flat=wk_t.reshape(NH * NOPE, KV_LORA).astype(bf16),
        w_v_bd=w_v_bd.reshape(NH * KV_LORA, NH * HD).astype(bf16),
        g_mla=p["g_mla_out"][li].reshape(1, SSD_W),
        conv_w=p["conv_w"][li], conv_b=p["conv_b"][li].reshape(1, CONV_CH),
        dt_bias=_pad_lanes(p["dt_bias"][li].reshape(1, NH)),
        a_neg=_pad_lanes(-jnp.exp(p["a_log"][li].astype(f32)).reshape(1, NH)),
        d_skip_l=jnp.repeat(p["d_skip"][li], HD).reshape(1, SSD_W), g_ssd=p["g_ssd_norm"][li].reshape(1, SSD_W),
        w_pool=w_pool_bd.reshape(POOL_W, POOL_W).astype(bf16), pool_scale=p["pool_scale"][li].reshape(1, POOL_W),
        g2=p["g_norm2"][li].reshape(1, d),
    )


def _rope_tables(pos):
    half = ROPE // 2
    inv = 1.0 / (ROPE_THETA ** (jnp.arange(half, dtype=f32) / half))
    ang = pos.astype(f32)[:, None] * inv[None, :]
    cos2 = jnp.concatenate([jnp.cos(ang)] * 2, axis=-1)
    sin2 = jnp.concatenate([jnp.sin(ang)] * 2, axis=-1)
    n = pos.shape[0]
    cosq = jnp.concatenate([jnp.ones((n, NOPE), f32), cos2, jnp.zeros((n, LANE - QK), f32)], axis=-1)
    sinq = _pad_lanes(sin2, NOPE)
    return cosq, sinq, _pad_lanes(cos2), _pad_lanes(sin2)


def kernel(x_prompt, x_sample, cache_kv_latent, cache_k_rope, state_ssm, state_conv, state_pool, page_table, c_prompt, c_sample, w_ada, b_ada, g_norm1, w_in, conv_w, conv_b, dt_bias, a_log, d_skip, g_ssd_norm, g_q_lora, w_q_up, g_kv_lora, w_k_up, w_v_up, g_qk_q, g_qk_k, g_mla_out, w_pool, pool_scale, w_out, g_norm2, w_gate, w_up, w_down):
    params = dict(g_norm1=g_norm1, w_in=w_in, conv_w=conv_w, conv_b=conv_b, dt_bias=dt_bias, a_log=a_log, d_skip=d_skip,
                  g_ssd_norm=g_ssd_norm, g_q_lora=g_q_lora, w_q_up=w_q_up, g_kv_lora=g_kv_lora, w_k_up=w_k_up,
                  w_v_up=w_v_up, g_qk_q=g_qk_q, g_qk_k=g_qk_k, g_mla_out=g_mla_out, w_pool=w_pool, pool_scale=pool_scale,
                  w_out=w_out, g_norm2=g_norm2, w_gate=w_gate, w_up=w_up, w_down=w_down)
    depth = w_ada.shape[0]
    bp, seq, d = x_prompt.shape
    bs = x_sample.shape[0]
    past = page_table.shape[1] * PAGE

    mod_p, mod_s = _ada_mod(c_prompt, c_sample, w_ada, b_ada)
    big = _stacked_weights(params)
    state_all = state_ssm.reshape(depth * bs, SSD_W, SSD_N)
    tabs_p = _rope_tables(jnp.arange(seq, dtype=jnp.int32))
    tabs_s = _rope_tables(jnp.full((1,), past, jnp.int32))
    cache_kpe_t = jnp.swapaxes(cache_k_rope, 2, 3)

    yp = x_prompt
    ys = x_sample.reshape(1, bs, d)
    p_new = [[] for _ in range(5)]
    s_new = [[] for _ in range(5)]
    for li in range(depth):
        wts = dict(_layer_weights(params, li), **big)

        z, xbc, dt, u, lat, kpe, q, k, v = _inproj(yp, mod_p, wts, tabs_p, li=li, tm=min(INPROJ_ROWS, seq), with_kv=True,
                                                   per_row=False)
        ssd_out, h_t = _ssd_prompt(xbc, z, dt, wts)
        mla_out = _attn_prompt(q, k, v, wts["g_mla"], tq=ATTN_TILE)
        pool_out = _pool_prompt(u, wts["w_pool"], wts["pool_scale"])
        yp = _ffn(yp, ssd_out, mla_out, pool_out, mod_p, wts, li=li, tm=FFN_ROWS, per_row=False)
        for lst, val in zip(p_new, (lat, kpe, h_t.reshape(bp, NH, HD, SSD_N), xbc[:, seq - (CONV_K - 1):],
                                    u[:, seq - POOL_BUF:])):
            lst.append(val)

        z, xbc, dt, u, lat, kpe, q, kpe128 = _inproj(ys, mod_s, wts, tabs_s, li=li, tm=bs, with_kv=False, per_row=True)
        z, xbc, dt, u, lat, kpe, q, kpe128 = (a[0] for a in (z, xbc, dt, u, lat, kpe, q, kpe128))
        conv_buf = state_conv[li]
        xs, bm, cm, xdt_t, da = _ssd_sample_pre(xbc, jnp.transpose(conv_buf, (1, 0, 2)), dt, wts)
        h_new, y_t = _ssd_sample_state(da[:, :8], state_all, xdt_t, bm, cm, li=li)
        ssd_out = _ssd_sample_post(y_t.T, xs, z, wts)
        qabs, qpe, snew = _mla_sample_pre(q, lat, kpe128, wts)
        qabs = jnp.pad(qabs.reshape(bs, NH, LANE), ((0, 0), (0, HPAD - NH), (0, 0))).astype(bf16)
        qpe = jnp.pad(qpe.reshape(bs, NH, LANE)[:, :, :ROPE], ((0, 0), (0, HPAD - NH), (0, 0))).astype(bf16)
        snew_b = jnp.broadcast_to(jnp.pad(snew[:, :NH], ((0, 0), (0, HPAD - NH)))[:, :, None], (bs, HPAD, LANE))
        o_lat = _attn_sample(page_table, cache_kv_latent, cache_kpe_t, wts["w_kt_flat"], qabs, qpe, snew_b,
                             lat.reshape(bs, 1, KV_LORA), li=li)
        mla_out = _mla_sample_post(o_lat[:, :NH].reshape(bs, NH * KV_LORA), wts["w_v_bd"], wts["g_mla"])
        pool_buf = state_pool[li]
        pool_out = _pool_sample(u, jnp.transpose(pool_buf, (1, 0, 2)), wts["w_pool"], wts["pool_scale"])
        ys = _ffn(ys, ssd_out[None], mla_out[None], pool_out[None], mod_s, wts, li=li, tm=bs, per_row=True)
        conv_new = jnp.concatenate([conv_buf[:, 1:], xbc[:, None, :]], axis=1)
        pool_new = jnp.concatenate([pool_buf[:, 1:], u[:, None, :]], axis=1)
        for lst, val in zip(s_new, (lat[:, None, :], kpe[:, None, :], h_new.reshape(bs, NH, HD, SSD_N), conv_new, pool_new)):
            lst.append(val)

    outs_p = [jnp.stack(vv, axis=0) for vv in p_new]
    outs_s = [jnp.stack(vv, axis=0) for vv in s_new]
    return (yp, ys.reshape(bs, 1, d), *outs_p, *outs_s)
```

```python
import functools
import math

import jax
import jax.numpy as jnp
import numpy as np
from jax import lax
from jax.experimental import pallas as pl
from jax.experimental.pallas import tpu as pltpu

f32 = jnp.float32
bf16 = jnp.bfloat16
HIGHEST = lax.Precision.HIGHEST

EPS = 1e-6
PAGE = 128
NH = 6
HD = 64
NOPE = 64
ROPE = 32
QK = NOPE + ROPE
SSD_W = NH * HD
SSD_G = 2
SSD_N = 128
SSD_CHUNK = 128
INPROJ_ROWS = 1024
FFN_ROWS = 512
ATTN_TILE = 512
SAMPLE_PAGES = 32
ROW_PIECE = 256
QK_AHEAD = 3
CONV_K = 4
CONV_CH = SSD_W + 2 * SSD_G * SSD_N
Q_LORA = 256
KV_LORA = 128
POOL_W = 256
POOL_BUF = 15
ROPE_THETA = 10000.0
ATTN_SCALE = QK ** -0.5
PROMPT_Q_SCALE = ATTN_SCALE * math.log2(math.e)
LANE = 128
HPAD = 16
VMEM_LIMIT = 56 * 1024 * 1024

_C_Z, _C_XBC, _C_CQ, _C_CKV, _C_U, _C_KA, _C_KB, _C_DT, _C_END = 0, 384, 1280, 1536, 1664, 1920, 2048, 2176, 2304


def _cparams(sem):
    return pltpu.CompilerParams(dimension_semantics=sem, vmem_limit_bytes=VMEM_LIMIT)


def _silu(x):
    return x * jax.nn.sigmoid(x)


def _softplus(x):
    return jnp.maximum(x, 0.0) + jnp.log1p(jnp.exp(-jnp.abs(x)))


def _rms(x, g, n):
    ms = jnp.sum(x * x, axis=-1, keepdims=True) * (1.0 / n)
    return x * lax.rsqrt(ms + EPS) * g


def _dot(a, b):
    return jnp.dot(a, b, preferred_element_type=f32)


def _dot_nt(a, b):
    return lax.dot_general(a, b, (((1,), (1,)), ((), ())), preferred_element_type=f32)


def _dot_tn(a, b):
    return lax.dot_general(a, b, (((0,), (0,)), ((), ())), preferred_element_type=f32)


def _dot_exact(a, b):
    return jnp.dot(a, b, precision=HIGHEST, preferred_element_type=f32)


def _head_expand():
    r = lax.broadcasted_iota(jnp.int32, (LANE, SSD_W), 0)
    c = lax.broadcasted_iota(jnp.int32, (LANE, SSD_W), 1)
    return jnp.where(c // HD == r, 1.0, 0.0).astype(bf16)


def _split3(a):
    hi = a.astype(bf16)
    r = a - hi.astype(f32)
    mid = r.astype(bf16)
    return hi, mid, (r - mid.astype(f32)).astype(bf16)


def _dot_sel_right(a, sel):
    hi, mid, lo = _split3(a)
    return _dot(hi, sel) + _dot(mid, sel) + _dot(lo, sel)


def _dot_sel_left(sel, a):
    hi, mid, lo = _split3(a)
    return _dot(sel, hi) + _dot(sel, mid) + _dot(sel, lo)


def _ada_kernel(c_ref, w_ref, b_ref, op_ref, os_ref):
    s = _silu(c_ref[...]).astype(bf16)
    r = _dot(s, w_ref[0].astype(bf16)) + b_ref[0]
    rp = op_ref.shape[2]
    op_ref[0, 0] = r[:rp]
    os_ref[0, 0] = r[rp:]


def _ada_mod(c_prompt, c_sample, w_ada, b_ada):
    depth, d, n6 = w_ada.shape
    nf = n6 // d
    bp, bs = c_prompt.shape[0], c_sample.shape[0]
    rp = -(-bp // 8) * 8
    c_all = jnp.concatenate([c_prompt, jnp.zeros((rp - bp, d), f32), c_sample], axis=0)
    return pl.pallas_call(
        _ada_kernel,
        grid=(depth, nf),
        in_specs=[pl.BlockSpec((rp + bs, d), lambda l, j: (0, 0)),
                  pl.BlockSpec((1, d, d), lambda l, j: (l, 0, j)),
                  pl.BlockSpec((1, 1, d), lambda l, j: (l, 0, j))],
        out_specs=[pl.BlockSpec((1, 1, rp, d), lambda l, j: (l, j, 0, 0)),
                   pl.BlockSpec((1, 1, bs, d), lambda l, j: (l, j, 0, 0))],
        out_shape=[jax.ShapeDtypeStruct((depth, nf, rp, d), f32), jax.ShapeDtypeStruct((depth, nf, bs, d), f32)],
        compiler_params=_cparams(("arbitrary", "arbitrary")),
        name="ada_mod",
    )(c_all, w_ada, b_ada.reshape(depth, 1, n6))


def _mod_rows(mod_ref, field, per_row, tm):
    if per_row:
        return mod_ref[0, field, pl.ds(pl.multiple_of(pl.program_id(1) * tm, tm), tm), :]
    return mod_ref[0, field, pl.ds(pl.program_id(0), 1), :]


def _inproj_kernel(*refs, with_kv, per_row):
    (x_ref, mod_ref, g1_ref, w_ref, gq_ref, wq_ref, cosq_ref, sinq_ref, gqk_ref, gkv_ref,
     cosk_ref, sink_ref) = refs[:12]
    if with_kv:
        wk_ref, gk_ref, wv_ref = refs[12:15]
        z_ref, xbc_ref, dt_ref, u_ref, lat_ref, kpe_ref, q_ref, k_ref, v_ref = refs[15:]
    else:
        z_ref, xbc_ref, dt_ref, u_ref, lat_ref, kpe_ref, q_ref, kpe128_ref = refs[12:]
    tm, d = x_ref.shape[1:]
    shift, scale = (_mod_rows(mod_ref, f, per_row, tm) for f in (0, 1))
    gqk = gqk_ref[...]
    q_scale = PROMPT_Q_SCALE if with_kv else 1.0
    nsplit = tm // ROW_PIECE if tm % ROW_PIECE == 0 else 1
    hm = tm // nsplit

    def rows_of(a, r):
        return a if a.shape[0] == 1 else a[r:r + hm]

    def project(r):
        h = _rms(x_ref[0, r:r + hm, :], g1_ref[...], d) * (1.0 + rows_of(scale, r)) + rows_of(shift, r)
        return _dot(h.astype(bf16), w_ref[0])

    def finish(r, proj):
        rs = slice(r, r + hm)
        z_ref[0, rs, :] = proj[:, _C_Z:_C_XBC]
        xbc_ref[0, rs, :] = proj[:, _C_XBC:_C_CQ]
        u_ref[0, rs, :] = proj[:, _C_U:_C_KA]
        dt_ref[0, rs, :] = proj[:, _C_DT:_C_END]
        cqn = _rms(proj[:, _C_CQ:_C_CKV], gq_ref[...], Q_LORA).astype(bf16)
        qq = _dot(cqn, wq_ref[0])
        cosq = rows_of(cosq_ref[...], r)
        sinq = rows_of(sinq_ref[...], r)
        for hh in range(NH):
            qh = qq[:, LANE * hh:LANE * (hh + 1)] * cosq + qq[:, NH * LANE + LANE * hh:NH * LANE + LANE * (hh + 1)] * sinq
            q_ref[0, rs, LANE * hh:LANE * (hh + 1)] = (_rms(qh, gqk, QK) * q_scale).astype(q_ref.dtype)
        lat = _rms(proj[:, _C_CKV:_C_U], gkv_ref[...], KV_LORA)
        lat_ref[0, rs, :] = lat
        kper = proj[:, _C_KA:_C_KB] * rows_of(cosk_ref[...], r) + proj[:, _C_KB:_C_DT] * rows_of(sink_ref[...], r)
        kpe_ref[0, rs, :] = kper[:, :ROPE]
        if with_kv:
            latb = lat.astype(bf16)
            kn = _dot(latb, wk_ref[...])
            kp = pltpu.roll(kper, NOPE, 1)
            gk = gk_ref[...]
            for hh in range(NH):
                kh = kn[:, LANE * hh:LANE * (hh + 1)] + kp
                k_ref[0, rs, LANE * hh:LANE * (hh + 1)] = _rms(kh, gk, QK).astype(bf16)
            v_ref[0, :, rs] = _dot_nt(wv_ref[...], latb).astype(bf16)
        else:
            kpe128_ref[0, rs, :] = kper

    nxt = project(0)
    for i in range(nsplit):
        proj = nxt
        if i + 1 < nsplit:
            nxt = project((i + 1) * hm)
        finish(i * hm, proj)


def _inproj(x, mod, wts, tabs, *, li, tm, with_kv, per_row):
    g_, t_, d = x.shape
    nt = t_ // tm
    cosq, sinq, cosk, sink = tabs
    tab_rows = cosq.shape[0] != 1
    tab_spec = pl.BlockSpec((tm if tab_rows else 1, LANE), (lambda g, t: (t, 0)) if tab_rows else (lambda g, t: (0, 0)))

    def full(a):
        return pl.BlockSpec(a.shape, lambda g, t: (0,) * a.ndim)

    def layer(a):
        return pl.BlockSpec((1,) + a.shape[1:], lambda g, t: (li,) + (0,) * (a.ndim - 1))

    def row(c):
        return pl.BlockSpec((1, tm, c), lambda g, t: (g, t, 0))

    ins = [x, mod, wts["g1"], wts["w_in"], wts["g_q"], wts["w_q"], cosq, sinq, wts["g_qk_q"], wts["g_kv"], cosk, sink]
    specs = [row(d), layer(mod), full(wts["g1"]), layer(wts["w_in"]), full(wts["g_q"]), layer(wts["w_q"]), tab_spec,
             tab_spec, full(wts["g_qk_q"]), full(wts["g_kv"]), tab_spec, tab_spec]
    widths = [SSD_W, CONV_CH, LANE, POOL_W, KV_LORA, ROPE, NH * LANE]
    dtypes = [f32, f32, f32, f32, f32, f32, bf16 if with_kv else f32]
    if with_kv:
        ins += [wts["w_k"], wts["g_qk_k"], wts["w_v"]]
        specs += [full(wts["w_k"]), full(wts["g_qk_k"]), full(wts["w_v"])]
        widths += [NH * LANE, SSD_W]
        dtypes += [bf16, bf16]
    else:
        widths += [LANE]
        dtypes += [f32]
    out_specs = [row(c) for c in widths]
    out_shape = [jax.ShapeDtypeStruct((g_, t_, c), dt) for c, dt in zip(widths, dtypes)]
    if with_kv:
        out_specs[-1] = pl.BlockSpec((1, SSD_W, tm), lambda g, t: (g, 0, t))
        out_shape[-1] = jax.ShapeDtypeStruct((g_, SSD_W, t_), bf16)
    return pl.pallas_call(
        functools.partial(_inproj_kernel, with_kv=with_kv, per_row=per_row),
        grid=(g_, nt),
        in_specs=specs,
        out_specs=out_specs,
        out_shape=out_shape,
        compiler_params=_cparams(("arbitrary", "arbitrary")),
        name="inproj_kv" if with_kv else "inproj",
    )(*ins)


def _ssd_prompt_kernel(xbc_ref, z_ref, dt_ref, cw_ref, cb_ref, dtb_ref, a_ref, dsk_ref, gn_ref, tril_ref, exp_ref,
                       y_ref, st_ref, ext_ref):
    c = pl.program_id(1)
    blk = xbc_ref.shape[1]
    t_ = SSD_CHUNK

    @pl.when(c == 0)
    def _():
        ext_ref[0:8, :] = jnp.zeros((8, CONV_CH), f32)
        st_ref[...] = jnp.zeros(st_ref.shape, f32)

    ext_ref[8:8 + blk, :] = xbc_ref[0]
    cw = cw_ref[...]
    tril = lax.broadcasted_iota(jnp.int32, (t_, t_), 0) >= lax.broadcasted_iota(jnp.int32, (t_, t_), 1)
    first = lax.broadcasted_iota(jnp.int32, (t_, LANE), 1) < HD
    rfirst = lax.broadcasted_iota(jnp.int32, (LANE, SSD_N), 0) < HD
    expand = exp_ref[...]

    def front(r):
        conv = cb_ref[...] + cw[3:4] * xbc_ref[0, r:r + t_, :]
        for k in range(CONV_K - 1):
            conv = conv + cw[k:k + 1] * ext_ref[5 + k + r:5 + k + r + t_, :]
        act = _silu(conv)
        xs = act[:, :SSD_W]
        bm = [act[:, SSD_W + SSD_N * g:SSD_W + SSD_N * (g + 1)].astype(bf16) for g in range(SSD_G)]
        cm = [act[:, SSD_W + SSD_N * (SSD_G + g):SSD_W + SSD_N * (SSD_G + g + 1)].astype(bf16) for g in range(SSD_G)]
        dt = _softplus(dt_ref[0, r:r + t_, :] + dtb_ref[...])
        a = dt * a_ref[...]
        acum = _dot_sel_left(tril_ref[...], a)
        acum_t = acum.T
        dt_l = _dot_sel_right(dt, expand)
        acum_l = _dot_sel_right(acum, expand)
        e_l = jnp.exp(acum_l)
        xdt = xs * dt_l
        xdt_w = xdt * jnp.exp(acum_l[t_ - 1:t_, :] - acum_l)
        cb = [_dot_nt(cm[g], bm[g]) for g in range(SSD_G)]
        cols = []
        for j in range(NH // 2):
            sl = slice(LANE * j, LANE * (j + 1))
            h0, h1 = 2 * j, 2 * j + 1
            xj = xdt[:, sl].astype(bf16)
            yd = []
            for hh in (h0, h1):
                seg = jnp.where(tril, jnp.exp(acum[:, hh:hh + 1] - acum_t[hh:hh + 1, :]), 0.0)
                yd.append(_dot((cb[hh // (NH // SSD_G)] * seg).astype(bf16), xj))
            tot = jnp.where(rfirst, jnp.exp(acum[t_ - 1:t_, h0:h0 + 1]), jnp.exp(acum[t_ - 1:t_, h1:h1 + 1]))
            cols.append((jnp.where(first, yd[0], yd[1]), xdt_w[:, sl].astype(bf16), tot, e_l[:, sl]))
        return xs, bm, cm, cols, _silu(z_ref[0, r:r + t_, :])

    def back(r, parts):
        xs, bm, cm, cols, gate = parts
        y_cols = []
        for j, (y_diag, xw, tot, e_j) in enumerate(cols):
            sl = slice(LANE * j, LANE * (j + 1))
            g0, g1 = (2 * j) // (NH // SSD_G), (2 * j + 1) // (NH // SSD_G)
            hp = st_ref[0, sl, :]
            hpb = hp.astype(bf16)
            if g0 == g1:
                y_off = _dot_nt(cm[g0], hpb)
                s_new = _dot_tn(xw, bm[g0])
            else:
                y_off = jnp.where(first, _dot_nt(cm[g0], hpb), _dot_nt(cm[g1], hpb))
                s_new = jnp.where(rfirst, _dot_tn(xw, bm[g0]), _dot_tn(xw, bm[g1]))
            st_ref[0, sl, :] = tot * hp + s_new
            y_cols.append(y_diag + y_off * e_j)
        v = (jnp.concatenate(y_cols, axis=-1) + dsk_ref[...] * xs) * gate
        gl = lax.broadcasted_iota(jnp.int32, (t_, SSD_W), 1) < SSD_W // SSD_G
        v2 = v * v
        ss0 = jnp.sum(jnp.where(gl, v2, 0.0), axis=-1, keepdims=True)
        ss1 = jnp.sum(jnp.where(gl, 0.0, v2), axis=-1, keepdims=True)
        gw = SSD_W // SSD_G
        rinv = jnp.where(gl, lax.rsqrt(ss0 * (1.0 / gw) + EPS), lax.rsqrt(ss1 * (1.0 / gw) + EPS))
        y_ref[0, r:r + t_, :] = v * rinv * gn_ref[...]

    fronts = [front(r) for r in range(0, blk, t_)]
    for i, parts in enumerate(fronts):
        back(i * t_, parts)
    ext_ref[0:8, :] = xbc_ref[0, blk - 8:, :]


def _ssd_prompt(xbc, z, dt, wts, *, blk=4 * SSD_CHUNK):
    b_, l_, _ = xbc.shape
    t_ = SSD_CHUNK
    blk = blk if l_ % blk == 0 else t_
    nc = l_ // blk

    def full(a):
        return pl.BlockSpec(a.shape, lambda b, c: (0,) * a.ndim)

    def row(w):
        return pl.BlockSpec((1, blk, w), lambda b, c: (b, c, 0))

    small = [wts["conv_w"], wts["conv_b"], wts["dt_bias"], wts["a_neg"], wts["d_skip_l"], wts["g_ssd"],
             jnp.tril(jnp.ones((t_, t_), bf16)), _head_expand()]
    return pl.pallas_call(
        _ssd_prompt_kernel,
        grid=(b_, nc),
        in_specs=[row(CONV_CH), row(SSD_W), row(LANE)] + [full(a) for a in small],
        out_specs=[row(SSD_W), pl.BlockSpec((1, SSD_W, SSD_N), lambda b, c: (b, 0, 0))],
        out_shape=[jax.ShapeDtypeStruct((b_, l_, SSD_W), f32), jax.ShapeDtypeStruct((b_, SSD_W, SSD_N), f32)],
        scratch_shapes=[pltpu.VMEM((8 + blk, CONV_CH), f32)],
        compiler_params=_cparams(("arbitrary", "arbitrary")),
        name="ssd_prompt",
    )(xbc, z, dt, *small)


def _attn_prompt_kernel(q_ref, k_ref, vt_ref, g_ref, o_ref, m_ref, l_ref, acc_ref, *, tq):
    qi = pl.program_id(1)
    m_ref[...] = jnp.full(m_ref.shape, -1e30, f32)
    l_ref[...] = jnp.zeros(l_ref.shape, f32)
    acc_ref[...] = jnp.zeros(acc_ref.shape, f32)

    ones = jnp.ones((16, tq), bf16)
    masked = lax.broadcasted_iota(jnp.int32, (tq, tq), 0) > lax.broadcasted_iota(jnp.int32, (tq, tq), 1)

    def tiles(key_tiles):
        blocks = []
        for j, diagonal in key_tiles:
            off = pl.multiple_of(j * tq, tq)
            blocks.append((k_ref[0, pl.ds(off, tq), :], vt_ref[0, :, pl.ds(off, tq)], diagonal))
        units = [(bi, hh) for bi in range(len(blocks)) for hh in range(NH)]

        def qk(unit):
            bi, hh = unit
            return _dot_nt(blocks[bi][0][:, LANE * hh:LANE * (hh + 1)], q_ref[0, :, LANE * hh:LANE * (hh + 1)])

        pending = [qk(u) for u in units[:QK_AHEAD]]
        for idx, (bi, hh) in enumerate(units):
            s = pending.pop(0)
            if idx + QK_AHEAD < len(units):
                pending.append(qk(units[idx + QK_AHEAD]))
            _, vblk, diagonal = blocks[bi]
            if diagonal:
                s = jnp.where(masked, -1e30, s)
            m_old = m_ref[hh]
            m_new = jnp.maximum(m_old, jnp.max(s, axis=0, keepdims=True))
            alpha = jnp.exp2(m_old - m_new)
            pe = jnp.exp2(s - m_new).astype(bf16)
            m_ref[hh] = m_new
            rows = slice(HD * hh, HD * (hh + 1))
            pv = _dot(jnp.concatenate([vblk[rows, :], ones], axis=0), pe)
            acc_ref[rows, :] = acc_ref[rows, :] * alpha + pv[:HD]
            l_ref[hh] = alpha * l_ref[hh] + pv[HD:HD + 8]

    def pair(j2, carry):
        tiles([(2 * j2, False), (2 * j2 + 1, False)])
        return carry

    lax.fori_loop(0, qi // 2, pair, 0)

    @pl.when(qi % 2 == 1)
    def _():
        tiles([(qi - 1, False), (qi, True)])

    @pl.when(qi % 2 == 0)
    def _():
        tiles([(qi, True)])

    parts = []
    for hh in range(NH):
        parts.append(acc_ref[HD * hh:HD * (hh + 1), :] * (1.0 / l_ref[hh, 0:1, :]))
    o_t = jnp.concatenate(parts, axis=0)
    ms = jnp.sum(o_t * o_t, axis=0, keepdims=True) * (1.0 / SSD_W)
    o_ref[0] = (o_t * lax.rsqrt(ms + EPS)).T * g_ref[...]


def _attn_prompt(q, k, v_t, g_out, *, tq):
    b_, l_, _ = q.shape
    return pl.pallas_call(
        functools.partial(_attn_prompt_kernel, tq=tq),
        grid=(b_, l_ // tq),
        in_specs=[pl.BlockSpec((1, tq, NH * LANE), lambda b, i: (b, i, 0)),
                  pl.BlockSpec((1, l_, NH * LANE), lambda b, i: (b, 0, 0)),
                  pl.BlockSpec((1, SSD_W, l_), lambda b, i: (b, 0, 0)),
                  pl.BlockSpec(g_out.shape, lambda b, i: (0, 0))],
        out_specs=pl.BlockSpec((1, tq, SSD_W), lambda b, i: (b, i, 0)),
        out_shape=jax.ShapeDtypeStruct((b_, l_, SSD_W), f32),
        scratch_shapes=[pltpu.VMEM((NH, 1, tq), f32), pltpu.VMEM((NH, 8, tq), f32), pltpu.VMEM((SSD_W, tq), f32)],
        compiler_params=_cparams(("arbitrary", "arbitrary")),
        name="attn_prompt",
    )(q, k, v_t, g_out)


def _pool_select(lane, a, b, c, d):
    return jnp.where(lane < 64, a, jnp.where(lane < 128, b, jnp.where(lane < 192, c, d)))


def _pool_prompt_kernel(u_ref, w_ref, sc_ref, o_ref, e1, e2, e4, e8, *, rt):
    t_ = u_ref.shape[1]
    hist = 16
    for e in (e1, e2, e4, e8):
        e[0:hist, :] = jnp.zeros((hist, POOL_W), f32)
    e1[hist:hist + t_, :] = u_ref[0]
    lane = lax.broadcasted_iota(jnp.int32, (rt, POOL_W), 1)
    win = _pool_select(lane, 2, 4, 8, 16)
    for i in range(t_ // rt):
        r0 = hist + i * rt
        a = e1[r0:r0 + rt, :]
        s2 = a + e1[r0 - 1:r0 - 1 + rt, :]
        e2[r0:r0 + rt, :] = s2
        s4 = s2 + e2[r0 - 2:r0 - 2 + rt, :]
        e4[r0:r0 + rt, :] = s4
        s8 = s4 + e4[r0 - 4:r0 - 4 + rt, :]
        e8[r0:r0 + rt, :] = s8
        s16 = s8 + e8[r0 - 8:r0 - 8 + rt, :]
        pos = lax.broadcasted_iota(jnp.int32, (rt, POOL_W), 0) + i * rt
        cnt = jnp.minimum(pos + 1, win).astype(f32)
        pooled = _pool_select(lane, s2, s4, s8, s16) / cnt - a
        o_ref[0, i * rt:(i + 1) * rt, :] = _dot(pooled.astype(bf16), w_ref[...]) * sc_ref[...]


def _pool_prompt(u, w_bd, scale):
    b_, l_, _ = u.shape
    return pl.pallas_call(
        functools.partial(_pool_prompt_kernel, rt=256),
        grid=(b_,),
        in_specs=[pl.BlockSpec((1, l_, POOL_W), lambda b: (b, 0, 0)),
                  pl.BlockSpec(w_bd.shape, lambda b: (0, 0)),
                  pl.BlockSpec(scale.shape, lambda b: (0, 0))],
        out_specs=pl.BlockSpec((1, l_, POOL_W), lambda b: (b, 0, 0)),
        out_shape=jax.ShapeDtypeStruct((b_, l_, POOL_W), f32),
        scratch_shapes=[pltpu.VMEM((16 + l_, POOL_W), f32)] * 4,
        compiler_params=_cparams(("arbitrary",)),
        name="pool_prompt",
    )(u, w_bd, scale)


def _pool_sample_kernel(u_ref, buf_ref, w_ref, sc_ref, o_ref):
    u = u_ref[...]
    run = u
    sums = {}
    for i in range(1, 16):
        run = run + buf_ref[POOL_BUF - i]
        if i + 1 in (2, 4, 8, 16):
            sums[i + 1] = run * (1.0 / (i + 1))
    lane = lax.broadcasted_iota(jnp.int32, u.shape, 1)
    pooled = _pool_select(lane, sums[2], sums[4], sums[8], sums[16]) - u
    o_ref[...] = _dot(pooled.astype(bf16), w_ref[...]) * sc_ref[...]


def _pool_sample(u, buf_t, w_bd, scale):
    return pl.pallas_call(
        _pool_sample_kernel,
        out_shape=jax.ShapeDtypeStruct(u.shape, f32),
        compiler_params=pltpu.CompilerParams(vmem_limit_bytes=VMEM_LIMIT),
        name="pool_sample",
    )(u, buf_t, w_bd, scale)


def _ffn_kernel(x_ref, ssd_ref, mla_ref, pool_ref, mod_ref, gn_ref, wo_ref, wg_ref, wu_ref, wd_ref, o_ref,
                *, nch, per_row):
    tm, d = x_ref.shape[1:]
    gate1, shift2, scale2, gate2 = (_mod_rows(mod_ref, f, per_row, tm) for f in (2, 3, 4, 5))
    hc = wg_ref.shape[2] // nch
    nsplit = tm // ROW_PIECE if tm % ROW_PIECE == 0 else 1
    hm = tm // nsplit

    def rows_of(a, r):
        return a if a.shape[0] == 1 else a[r:r + hm]

    def out_proj(r):
        rs = slice(r, r + hm)
        mixed = jnp.concatenate([ssd_ref[0, rs, :].astype(bf16), mla_ref[0, rs, :].astype(bf16),
                                 pool_ref[0, rs, :].astype(bf16)], axis=-1)
        return _dot(mixed, wo_ref[0])

    def residual_norm(r, mix):
        x1 = x_ref[0, r:r + hm, :] + rows_of(gate1, r) * mix
        return x1, (_rms(x1, gn_ref[...], d) * (1.0 + rows_of(scale2, r)) + rows_of(shift2, r)).astype(bf16)

    def gate_up(h2, c):
        return _dot(h2, wg_ref[0, :, hc * c:hc * (c + 1)]), _dot(h2, wu_ref[0, :, hc * c:hc * (c + 1)])

    mixes = [out_proj(i * hm) for i in range(nsplit)]
    x1, h2 = residual_norm(0, mixes[0])
    for i in range(nsplit):
        r = i * hm
        acc = jnp.zeros(x1.shape, f32)
        nxt = gate_up(h2, 0)
        if i + 1 < nsplit:
            x1_next, h2_next = residual_norm(r + hm, mixes[i + 1])
        for c in range(nch):
            gate, up = nxt
            if c + 1 < nch:
                nxt = gate_up(h2, c + 1)
            acc = acc + _dot((_silu(gate) * up).astype(bf16), wd_ref[0, hc * c:hc * (c + 1), :])
        o_ref[0, r:r + hm, :] = x1 + rows_of(gate2, r) * acc
        if i + 1 < nsplit:
            x1, h2 = x1_next, h2_next


def _ffn(x, ssd, mla, pool, mod, wts, *, li, tm, per_row):
    g_, t_, d = x.shape

    def layer(a):
        return pl.BlockSpec((1,) + a.shape[1:], lambda g, t: (li,) + (0,) * (a.ndim - 1), pipeline_mode=pl.Buffered(1))

    def row(c):
        return pl.BlockSpec((1, tm, c), lambda g, t: (g, t, 0))

    ws = [wts["w_out"], wts["w_gate"], wts["w_up"], wts["w_down"]]
    return pl.pallas_call(
        functools.partial(_ffn_kernel, nch=11, per_row=per_row),
        grid=(g_, t_ // tm),
        in_specs=[row(d), row(SSD_W), row(SSD_W), row(POOL_W), layer(mod),
                  pl.BlockSpec(wts["g2"].shape, lambda g, t: (0, 0))] + [layer(a) for a in ws],
        out_specs=row(d),
        out_shape=jax.ShapeDtypeStruct((g_, t_, d), f32),
        compiler_params=_cparams(("arbitrary", "arbitrary")),
        name="outproj_ffn",
    )(x, ssd, mla, pool, mod, wts["g2"], *ws)


def _ssd_sample_pre_kernel(xbc_ref, buf_ref, dt_ref, cw_ref, cb_ref, dtb_ref, a_ref,
                           xs_ref, b_ref, c_ref, xdt_t_ref, da_ref):
    cw = cw_ref[...]
    conv = cb_ref[...] + cw[3:4] * xbc_ref[...]
    for k in range(CONV_K - 1):
        conv = conv + cw[k:k + 1] * buf_ref[k]
    act = _silu(conv)
    xs = act[:, :SSD_W]
    xs_ref[...] = xs
    b_ref[...] = act[:, SSD_W:SSD_W + SSD_G * SSD_N]
    c_ref[...] = act[:, SSD_W + SSD_G * SSD_N:]
    dt = _softplus(dt_ref[...] + dtb_ref[...])
    da_ref[...] = jnp.exp(dt * a_ref[...])
    xdt = xs * _dot_sel_right(dt, _head_expand())
    for j in range(SSD_W // LANE):
        xdt_t_ref[LANE * j:LANE * (j + 1), :] = xdt[:, LANE * j:LANE * (j + 1)].T


def _ssd_sample_pre(xbc, buf_t, dt, wts):
    b_ = xbc.shape[0]
    shapes = [(b_, SSD_W), (b_, SSD_G * SSD_N), (b_, SSD_G * SSD_N), (SSD_W, b_), (b_, LANE)]
    return pl.pallas_call(
        _ssd_sample_pre_kernel,
        out_shape=[jax.ShapeDtypeStruct(s, f32) for s in shapes],
        compiler_params=pltpu.CompilerParams(vmem_limit_bytes=VMEM_LIMIT),
        name="ssd_sample_pre",
    )(xbc, buf_t, dt, wts["conv_w"], wts["conv_b"], wts["dt_bias"], wts["a_neg"])


def _ssd_sample_state_kernel(da_ref, st_ref, xdt_t_ref, b_ref, c_ref, new_ref, y_t_ref, *, tb):
    i = pl.program_id(0)

    @pl.when(i == 0)
    def _():
        y_t_ref[...] = jnp.zeros(y_t_ref.shape, f32)

    nb = xdt_t_ref.shape[1]
    lane = lax.broadcasted_iota(jnp.int32, (HD, nb), 1)
    pairs = [(hh, bl) for hh in range(NH) for bl in range(tb)]
    sels = [lane == i * tb + bl for bl in range(tb)]
    cols = {}
    for hh, bl in pairs:
        cols[hh, bl] = jnp.sum(jnp.where(sels[bl], xdt_t_ref[HD * hh:HD * (hh + 1), :], 0.0), axis=1, keepdims=True)
    ycols = {}
    for hh, bl in pairs:
        g = hh // (NH // SSD_G)
        rs = slice(HD * hh, HD * (hh + 1))
        brow = b_ref[bl:bl + 1, SSD_N * g:SSD_N * (g + 1)]
        crow = c_ref[bl:bl + 1, SSD_N * g:SSD_N * (g + 1)]
        new = da_ref[i * tb + bl, hh] * st_ref[bl, rs, :] + cols[hh, bl] * brow
        new_ref[bl, rs, :] = new
        ycols[hh, bl] = jnp.sum(new * crow, axis=1, keepdims=True)
    for hh in range(NH):
        rs = slice(HD * hh, HD * (hh + 1))
        y_rows = y_t_ref[rs, :]
        for bl in range(tb):
            y_rows = jnp.where(sels[bl], ycols[hh, bl], y_rows)
        y_t_ref[rs, :] = y_rows


def _ssd_sample_state(da, state_all, xdt_t, bm, cm, *, li, tb=8):
    b_ = xdt_t.shape[1]
    nblk = b_ // tb
    return pl.pallas_call(
        functools.partial(_ssd_sample_state_kernel, tb=tb),
        grid=(nblk,),
        in_specs=[pl.BlockSpec(memory_space=pltpu.SMEM),
                  pl.BlockSpec((tb, SSD_W, SSD_N), lambda i: (li * nblk + i, 0, 0)),
                  pl.BlockSpec(xdt_t.shape, lambda i: (0, 0)),
                  pl.BlockSpec((tb, SSD_G * SSD_N), lambda i: (i, 0)),
                  pl.BlockSpec((tb, SSD_G * SSD_N), lambda i: (i, 0))],
        out_specs=[pl.BlockSpec((tb, SSD_W, SSD_N), lambda i: (i, 0, 0)),
                   pl.BlockSpec((SSD_W, b_), lambda i: (0, 0))],
        out_shape=[jax.ShapeDtypeStruct((b_, SSD_W, SSD_N), f32), jax.ShapeDtypeStruct((SSD_W, b_), f32)],
        compiler_params=_cparams(("arbitrary",)),
        name="ssd_sample_state",
    )(da, state_all, xdt_t, bm, cm)


def _ssd_sample_post_kernel(y_ref, xs_ref, z_ref, dsk_ref, gn_ref, o_ref):
    y = y_ref[...] + dsk_ref[...] * xs_ref[...]
    v = y * _silu(z_ref[...])
    gl = lax.broadcasted_iota(jnp.int32, v.shape, 1) < SSD_W // SSD_G
    v2 = v * v
    gw = SSD_W // SSD_G
    ss0 = jnp.sum(jnp.where(gl, v2, 0.0), axis=-1, keepdims=True)
    ss1 = jnp.sum(jnp.where(gl, 0.0, v2), axis=-1, keepdims=True)
    rinv = jnp.where(gl, lax.rsqrt(ss0 * (1.0 / gw) + EPS), lax.rsqrt(ss1 * (1.0 / gw) + EPS))
    o_ref[...] = v * rinv * gn_ref[...]


def _ssd_sample_post(y, xs, z, wts):
    return pl.pallas_call(
        _ssd_sample_post_kernel,
        out_shape=jax.ShapeDtypeStruct(y.shape, f32),
        name="ssd_sample_post",
    )(y, xs, z, wts["d_skip_l"], wts["g_ssd"])


def _mla_sample_pre_kernel(q_ref, lat_ref, kpe_ref, wk_ref, wkt_ref, gk_ref, qabs_ref, qpe_ref, snew_ref):
    kn = _dot(lat_ref[...].astype(bf16), wk_ref[...])
    kp = pltpu.roll(kpe_ref[...], NOPE, 1)
    gk = gk_ref[...]
    lane = lax.broadcasted_iota(jnp.int32, snew_ref.shape, 1)
    snew = jnp.zeros(snew_ref.shape, f32)
    for hh in range(NH):
        sl = slice(LANE * hh, LANE * (hh + 1))
        qh = q_ref[:, sl]
        knew = _rms(kn[:, sl] + kp, gk, QK)
        snew = jnp.where(lane == hh, jnp.sum(qh * knew, axis=-1, keepdims=True) * ATTN_SCALE, snew)
        qg = qh * gk
        qabs_ref[:, sl] = _dot_exact(qg, wkt_ref[hh])
        qpe_ref[:, sl] = pltpu.roll(qg, NOPE, 1)
    snew_ref[...] = snew


def _mla_sample_pre(q, lat, kpe128, wts):
    b_ = q.shape[0]
    shapes = [(b_, NH * LANE), (b_, NH * LANE), (b_, LANE)]
    return pl.pallas_call(
        _mla_sample_pre_kernel,
        out_shape=[jax.ShapeDtypeStruct(s, f32) for s in shapes],
        name="mla_sample_pre",
    )(q, lat, kpe128, wts["w_k"], wts["w_kt"], wts["g_qk_k"])


def _attn_sample_kernel(pt_ref, lat_hbm, kpe_hbm, wkt_ref, qabs_ref, qpe_ref, snew_ref, latnew_ref, o_ref,
                        lat_buf, kpe_buf, sem, wext, latb_ref, s_ref, *, li, pages, nchunk):
    b = pl.program_id(0)
    nb = pl.num_programs(0)
    rc = pages * PAGE
    slot = b % 2

    def aligned(idx, size):
        return idx * size if isinstance(idx, int) else pl.multiple_of(idx * size, size)

    def copies(bb, c, i, sl):
        page_idx = c * pages + i
        page = pt_ref[bb, page_idx]
        off = aligned(page_idx, PAGE)
        return (pltpu.make_async_copy(lat_hbm.at[li, page], lat_buf.at[sl, pl.ds(off, PAGE), :], sem.at[0, sl, c]),
                pltpu.make_async_copy(kpe_hbm.at[li, page], kpe_buf.at[sl, :, pl.ds(off, PAGE)], sem.at[1, sl, c]))

    def start_chunk(bb, c, sl):
        for i in range(pages):
            for cp in copies(bb, c, i, sl):
                cp.start()

    @pl.when(b == 0)
    def _():
        def issue(c, carry):
            start_chunk(0, c, 0)
            start_chunk(jnp.minimum(1, nb - 1), c, 1)
            return carry
        lax.fori_loop(0, nchunk, issue, 0)
        wext[0:NH * HD, :] = wkt_ref[...]

    wext[NH * HD:NH * HD + HPAD, :] = qabs_ref[0]
    qpe = qpe_ref[0]
    rowi = lax.broadcasted_iota(jnp.int32, (HPAD, rc), 0)

    for c in range(nchunk):
        for i in range(pages):
            for cp in copies(b, c, i, slot):
                cp.wait()

    def project(c):
        latb = lat_buf[slot, c * rc:(c + 1) * rc, :].astype(bf16)
        latb_ref[c * rc:(c + 1) * rc, :] = latb
        return _dot_nt(wext[...], latb)

    def scores(c, a):
        kpe = kpe_buf[slot, :, c * rc:(c + 1) * rc]
        nsq = jnp.zeros((HPAD, rc), f32)
        for hh in range(NH):
            kh = a[HD * hh:HD * (hh + 1), :]
            nsq = jnp.where(rowi == hh, jnp.sum(kh * kh, axis=0, keepdims=True), nsq)
        s_pe = _dot(qpe, kpe.astype(bf16))
        ksq = jnp.sum(kpe * kpe, axis=0, keepdims=True)
        s_ref[:, c * rc:(c + 1) * rc] = ((a[NH * HD:, :] + s_pe)
                                         * lax.rsqrt((nsq + ksq) * (1.0 / QK) + EPS) * ATTN_SCALE)

    a_next = project(0)
    for c in range(nchunk):
        a = a_next
        if c + 1 < nchunk:
            a_next = project(c + 1)
        scores(c, a)

    b_ahead = jnp.minimum(b + 2, nb - 1)
    for c in range(nchunk):
        start_chunk(b_ahead, c, slot)

    s = s_ref[...]
    s_new = snew_ref[0][:, 0:1]
    m = jnp.maximum(jnp.max(s, axis=-1, keepdims=True), s_new)
    pe = jnp.exp(s - m)
    p_new = jnp.exp(s_new - m)
    l = jnp.sum(pe, axis=-1, keepdims=True) + p_new
    acc = _dot(pe.astype(bf16), latb_ref[...]) + p_new * latnew_ref[0]
    o_ref[0] = acc / l

    @pl.when(b == nb - 1)
    def _():
        def drain(c, carry):
            for sl in range(2):
                for i in range(pages):
                    for cp in copies(b, c, i, sl):
                        cp.wait()
            return carry
        lax.fori_loop(0, nchunk, drain, 0)


def _attn_sample(page_table, cache_lat, cache_kpe_t, wkt, qabs, qpe, snew, lat_new, *, li, pages=SAMPLE_PAGES):
    b_, npages = page_table.shape
    nchunk = npages // pages
    assert npages % pages == 0
    seq = npages * PAGE
    grid_spec = pltpu.PrefetchScalarGridSpec(
        num_scalar_prefetch=1,
        grid=(b_,),
        in_specs=[pl.BlockSpec(memory_space=pl.ANY),
                  pl.BlockSpec(memory_space=pl.ANY),
                  pl.BlockSpec(wkt.shape, lambda b, pt: (0, 0)),
                  pl.BlockSpec((1, HPAD, KV_LORA), lambda b, pt: (b, 0, 0)),
                  pl.BlockSpec((1, HPAD, ROPE), lambda b, pt: (b, 0, 0)),
                  pl.BlockSpec((1, HPAD, LANE), lambda b, pt: (b, 0, 0)),
                  pl.BlockSpec((1, 1, KV_LORA), lambda b, pt: (b, 0, 0))],
        out_specs=pl.BlockSpec((1, HPAD, KV_LORA), lambda b, pt: (b, 0, 0)),
        scratch_shapes=[pltpu.VMEM((2, seq, KV_LORA), f32),
                        pltpu.VMEM((2, ROPE, seq), f32),
                        pltpu.SemaphoreType.DMA((2, 2, nchunk)),
                        pltpu.VMEM((NH * HD + HPAD, KV_LORA), bf16),
                        pltpu.VMEM((seq, KV_LORA), bf16),
                        pltpu.VMEM((HPAD, seq), f32)],
    )
    return pl.pallas_call(
        functools.partial(_attn_sample_kernel, li=li, pages=pages, nchunk=nchunk),
        grid_spec=grid_spec,
        out_shape=jax.ShapeDtypeStruct((b_, HPAD, KV_LORA), f32),
        compiler_params=_cparams(("arbitrary",)),
        name="attn_sample",
    )(page_table, cache_lat, cache_kpe_t, wkt, qabs, qpe, snew, lat_new)


def _mla_sample_post_kernel(o_ref, wv_ref, g_ref, out_ref):
    out_ref[...] = _rms(_dot(o_ref[...].astype(bf16), wv_ref[...]), g_ref[...], SSD_W)


def _mla_sample_post(o_lat, w_v_bd, g_out):
    return pl.pallas_call(
        _mla_sample_post_kernel,
        out_shape=jax.ShapeDtypeStruct((o_lat.shape[0], SSD_W), f32),
        name="mla_sample_post",
    )(o_lat, w_v_bd, g_out)


def _pad_heads(w, lo, hi):
    pad = [(0, 0)] * (w.ndim - 1) + [(lo, LANE - hi)]
    w = jnp.pad(w, pad)
    return w.reshape(w.shape[:-2] + (NH * LANE,))


def _rot_cols(w):
    half = ROPE // 2
    return jnp.concatenate([-w[..., half:], w[..., :half]], axis=-1)


def _pad_lanes(a, lo=0):
    return jnp.pad(a, [(0, 0)] * (a.ndim - 1) + [(lo, LANE - lo - a.shape[-1])])


def _stacked_weights(p):
    offs = np.cumsum([SSD_W, CONV_CH, NH, Q_LORA, KV_LORA, ROPE, POOL_W])
    wz, wxbc, wdt, wcq, wckv, wkpe, wu = jnp.split(p["w_in"], offs[:-1].tolist(), axis=2)
    w_in = jnp.concatenate([wz, wxbc, wcq, wckv, wu, _pad_lanes(wkpe), _pad_lanes(_rot_cols(wkpe)), _pad_lanes(wdt)],
                           axis=2).astype(bf16)
    wq = p["w_q_up"]
    w_q = jnp.concatenate([_pad_heads(wq, 0, QK), _pad_heads(_rot_cols(wq[..., NOPE:]), NOPE, QK)], axis=2).astype(bf16)
    return dict(w_in=w_in, w_q=w_q, w_out=p["w_out"].astype(bf16), w_gate=p["w_gate"].astype(bf16),
                w_up=p["w_up"].astype(bf16), w_down=p["w_down"].astype(bf16))


def _layer_weights(p, li):
    d = p["w_in"].shape[1]
    wk = p["w_k_up"][li]
    wk_pad = _pad_heads(wk, 0, NOPE)
    wk_t = jnp.transpose(wk, (1, 2, 0))
    wv = p["w_v_up"][li]
    w_v_bd = jnp.zeros((NH, KV_LORA, NH, HD), f32)
    w_v_bd = w_v_bd.at[jnp.arange(NH), :, jnp.arange(NH), :].set(jnp.transpose(wv, (1, 0, 2)))
    wp = p["w_pool"][li]
    ng = wp.shape[0]
    w_pool_bd = jnp.zeros((ng, HD, ng, HD), f32).at[jnp.arange(ng), :, jnp.arange(ng), :].set(wp)
    return dict(
        g1=p["g_norm1"][li].reshape(1, d),
        g_q=p["g_q_lora"][li].reshape(1, Q_LORA),
        g_qk_q=_pad_lanes(p["g_qk_q"][li].reshape(1, QK)), g_qk_k=_pad_lanes(p["g_qk_k"][li].reshape(1, QK)),
        g_kv=p["g_kv_lora"][li].reshape(1, KV_LORA),
        w_k=wk_pad.astype(bf16), w_v=wv.reshape(KV_LORA, NH * HD).T.astype(bf16),
        w_kt=jnp.pad(wk_t, ((0, 0), (0, LANE - NOPE), (0, 0))),
        w_kt_flat=wk_t.reshape(NH * NOPE, KV_LORA).astype(bf16),
        w_v_bd=w_v_bd.reshape(NH * KV_LORA, NH * HD).astype(bf16),
        g_mla=p["g_mla_out"][li].reshape(1, SSD_W),
        conv_w=p["conv_w"][li], conv_b=p["conv_b"][li].reshape(1, CONV_CH),
        dt_bias=_pad_lanes(p["dt_bias"][li].reshape(1, NH)),
        a_neg=_pad_lanes(-jnp.exp(p["a_log"][li].astype(f32)).reshape(1, NH)),
        d_skip_l=jnp.repeat(p["d_skip"][li], HD).reshape(1, SSD_W), g_ssd=p["g_ssd_norm"][li].reshape(1, SSD_W),
        w_pool=w_pool_bd.reshape(POOL_W, POOL_W).astype(bf16), pool_scale=p["pool_scale"][li].reshape(1, POOL_W),
        g2=p["g_norm2"][li].reshape(1, d),
    )


def _rope_tables(pos):
    half = ROPE // 2
    inv = 1.0 / (ROPE_THETA ** (jnp.arange(half, dtype=f32) / half))
    ang = pos.astype(f32)[:, None] * inv[None, :]
    cos2 = jnp.concatenate([jnp.cos(ang)] * 2, axis=-1)
    sin2 = jnp.concatenate([jnp.sin(ang)] * 2, axis=-1)
    n = pos.shape[0]
    cosq = jnp.concatenate([jnp.ones((n, NOPE), f32), cos2, jnp.zeros((n, LANE - QK), f32)], axis=-1)
    sinq = _pad_lanes(sin2, NOPE)
    return cosq, sinq, _pad_lanes(cos2), _pad_lanes(sin2)


def kernel(x_prompt, x_sample, cache_kv_latent, cache_k_rope, state_ssm, state_conv, state_pool, page_table, c_prompt, c_sample, w_ada, b_ada, g_norm1, w_in, conv_w, conv_b, dt_bias, a_log, d_skip, g_ssd_norm, g_q_lora, w_q_up, g_kv_lora, w_k_up, w_v_up, g_qk_q, g_qk_k, g_mla_out, w_pool, pool_scale, w_out, g_norm2, w_gate, w_up, w_down):
    params = dict(g_norm1=g_norm1, w_in=w_in, conv_w=conv_w, conv_b=conv_b, dt_bias=dt_bias, a_log=a_log, d_skip=d_skip,
                  g_ssd_norm=g_ssd_norm, g_q_lora=g_q_lora, w_q_up=w_q_up, g_kv_lora=g_kv_lora, w_k_up=w_k_up,
                  w_v_up=w_v_up, g_qk_q=g_qk_q, g_qk_k=g_qk_k, g_mla_out=g_mla_out, w_pool=w_pool, pool_scale=pool_scale,
                  w_out=w_out, g_norm2=g_norm2, w_gate=w_gate, w_up=w_up, w_down=w_down)
    depth = w_ada.shape[0]
    bp, seq, d = x_prompt.shape
    bs = x_sample.shape[0]
    past = page_table.shape[1] * PAGE

    mod_p, mod_s = _ada_mod(c_prompt, c_sample, w_ada, b_ada)
    big = _stacked_weights(params)
    state_all = state_ssm.reshape(depth * bs, SSD_W, SSD_N)
    tabs_p = _rope_tables(jnp.arange(seq, dtype=jnp.int32))
    tabs_s = _rope_tables(jnp.full((1,), past, jnp.int32))
    cache_kpe_t = jnp.swapaxes(cache_k_rope, 2, 3)

    yp = x_prompt
    ys = x_sample.reshape(1, bs, d)
    p_new = [[] for _ in range(5)]
    s_new = [[] for _ in range(5)]
    for li in range(depth):
        wts = dict(_layer_weights(params, li), **big)

        z, xbc, dt, u, lat, kpe, q, k, v = _inproj(yp, mod_p, wts, tabs_p, li=li, tm=min(INPROJ_ROWS, seq), with_kv=True,
                                                   per_row=False)
        ssd_out, h_t = _ssd_prompt(xbc, z, dt, wts)
        mla_out = _attn_prompt(q, k, v, wts["g_mla"], tq=ATTN_TILE)
        pool_out = _pool_prompt(u, wts["w_pool"], wts["pool_scale"])
        yp = _ffn(yp, ssd_out, mla_out, pool_out, mod_p, wts, li=li, tm=FFN_ROWS, per_row=False)
        for lst, val in zip(p_new, (lat, kpe, h_t.reshape(bp, NH, HD, SSD_N), xbc[:, seq - (CONV_K - 1):],
                                    u[:, seq - POOL_BUF:])):
            lst.append(val)

        z, xbc, dt, u, lat, kpe, q, kpe128 = _inproj(ys, mod_s, wts, tabs_s, li=li, tm=bs, with_kv=False, per_row=True)
        z, xbc, dt, u, lat, kpe, q, kpe128 = (a[0] for a in (z, xbc, dt, u, lat, kpe, q, kpe128))
        conv_buf = state_conv[li]
        xs, bm, cm, xdt_t, da = _ssd_sample_pre(xbc, jnp.transpose(conv_buf, (1, 0, 2)), dt, wts)
        h_new, y_t = _ssd_sample_state(da[:, :8], state_all, xdt_t, bm, cm, li=li)
        ssd_out = _ssd_sample_post(y_t.T, xs, z, wts)
        qabs, qpe, snew = _mla_sample_pre(q, lat, kpe128, wts)
        qabs = jnp.pad(qabs.reshape(bs, NH, LANE), ((0, 0), (0, HPAD - NH), (0, 0))).astype(bf16)
        qpe = jnp.pad(qpe.reshape(bs, NH, LANE)[:, :, :ROPE], ((0, 0), (0, HPAD - NH), (0, 0))).astype(bf16)
        snew_b = jnp.broadcast_to(jnp.pad(snew[:, :NH], ((0, 0), (0, HPAD - NH)))[:, :, None], (bs, HPAD, LANE))
        o_lat = _attn_sample(page_table, cache_kv_latent, cache_kpe_t, wts["w_kt_flat"], qabs, qpe, snew_b,
                             lat.reshape(bs, 1, KV_LORA), li=li)
        mla_out = _mla_sample_post(o_lat[:, :NH].reshape(bs, NH * KV_LORA), wts["w_v_bd"], wts["g_mla"])
        pool_buf = state_pool[li]
        pool_out = _pool_sample(u, jnp.transpose(pool_buf, (1, 0, 2)), wts["w_pool"], wts["pool_scale"])
        ys = _ffn(ys, ssd_out[None], mla_out[None], pool_out[None], mod_s, wts, li=li, tm=bs, per_row=True)
        conv_new = jnp.concatenate([conv_buf[:, 1:], xbc[:, None, :]], axis=1)
        pool_new = jnp.concatenate([pool_buf[:, 1:], u[:, None, :]], axis=1)
        for lst, val in zip(s_new, (lat[:, None, :], kpe[:, None, :], h_new.reshape(bs, NH, HD, SSD_N), conv_new, pool_new)):
            lst.append(val)

    outs_p = [jnp.stack(vv, axis=0) for vv in p_new]
    outs_s = [jnp.stack(vv, axis=0) for vv in s_new]
    return (yp, ys.reshape(bs, 1, d), *outs_p, *outs_s)
```

```python
import functools
import math

import jax
import jax.numpy as jnp
import numpy as np
from jax import lax
from jax.experimental import pallas as pl
from jax.experimental.pallas import tpu as pltpu

f32 = jnp.float32
bf16 = jnp.bfloat16
HIGHEST = lax.Precision.HIGHEST

EPS = 1e-6
PAGE = 128
NH = 6
HD = 64
NOPE = 64
ROPE = 32
QK = NOPE + ROPE
SSD_W = NH * HD
SSD_G = 2
SSD_N = 128
SSD_CHUNK = 128
INPROJ_ROWS = 1024
FFN_ROWS = 512
ATTN_TILE = 512
SAMPLE_PAGES = 32
ROW_PIECE = 256
QK_AHEAD = 3
CONV_K = 4
CONV_CH = SSD_W + 2 * SSD_G * SSD_N
Q_LORA = 256
KV_LORA = 128
POOL_W = 256
POOL_BUF = 15
ROPE_THETA = 10000.0
ATTN_SCALE = QK ** -0.5
PROMPT_Q_SCALE = ATTN_SCALE * math.log2(math.e)
LANE = 128
HPAD = 16
VMEM_LIMIT = 56 * 1024 * 1024

_C_Z, _C_XBC, _C_CQ, _C_CKV, _C_U, _C_KA, _C_KB, _C_DT, _C_END = 0, 384, 1280, 1536, 1664, 1920, 2048, 2176, 2304


def _cparams(sem):
    return pltpu.CompilerParams(dimension_semantics=sem, vmem_limit_bytes=VMEM_LIMIT)


def _silu(x):
    return x * jax.nn.sigmoid(x)


def _softplus(x):
    return jnp.maximum(x, 0.0) + jnp.log1p(jnp.exp(-jnp.abs(x)))


def _rms(x, g, n):
    ms = jnp.sum(x * x, axis=-1, keepdims=True) * (1.0 / n)
    return x * lax.rsqrt(ms + EPS) * g


def _dot(a, b):
    return jnp.dot(a, b, preferred_element_type=f32)


def _dot_nt(a, b):
    return lax.dot_general(a, b, (((1,), (1,)), ((), ())), preferred_element_type=f32)


def _dot_tn(a, b):
    return lax.dot_general(a, b, (((0,), (0,)), ((), ())), preferred_element_type=f32)


def _dot_exact(a, b):
    return jnp.dot(a, b, precision=HIGHEST, preferred_element_type=f32)


def _head_expand():
    r = lax.broadcasted_iota(jnp.int32, (LANE, SSD_W), 0)
    c = lax.broadcasted_iota(jnp.int32, (LANE, SSD_W), 1)
    return jnp.where(c // HD == r, 1.0, 0.0).astype(bf16)


def _split3(a):
    hi = a.astype(bf16)
    r = a - hi.astype(f32)
    mid = r.astype(bf16)
    return hi, mid, (r - mid.astype(f32)).astype(bf16)


def _dot_sel_right(a, sel):
    hi, mid, lo = _split3(a)
    return _dot(hi, sel) + _dot(mid, sel) + _dot(lo, sel)


def _dot_sel_left(sel, a):
    hi, mid, lo = _split3(a)
    return _dot(sel, hi) + _dot(sel, mid) + _dot(sel, lo)


def _ada_kernel(c_ref, w_ref, b_ref, op_ref, os_ref):
    s = _silu(c_ref[...]).astype(bf16)
    r = _dot(s, w_ref[0].astype(bf16)) + b_ref[0]
    rp = op_ref.shape[2]
    op_ref[0, 0] = r[:rp]
    os_ref[0, 0] = r[rp:]


def _ada_mod(c_prompt, c_sample, w_ada, b_ada):
    depth, d, n6 = w_ada.shape
    nf = n6 // d
    bp, bs = c_prompt.shape[0], c_sample.shape[0]
    rp = -(-bp // 8) * 8
    c_all = jnp.concatenate([c_prompt, jnp.zeros((rp - bp, d), f32), c_sample], axis=0)
    return pl.pallas_call(
        _ada_kernel,
        grid=(depth, nf),
        in_specs=[pl.BlockSpec((rp + bs, d), lambda l, j: (0, 0)),
                  pl.BlockSpec((1, d, d), lambda l, j: (l, 0, j)),
                  pl.BlockSpec((1, 1, d), lambda l, j: (l, 0, j))],
        out_specs=[pl.BlockSpec((1, 1, rp, d), lambda l, j: (l, j, 0, 0)),
                   pl.BlockSpec((1, 1, bs, d), lambda l, j: (l, j, 0, 0))],
        out_shape=[jax.ShapeDtypeStruct((depth, nf, rp, d), f32), jax.ShapeDtypeStruct((depth, nf, bs, d), f32)],
        compiler_params=_cparams(("arbitrary", "arbitrary")),
        name="ada_mod",
    )(c_all, w_ada, b_ada.reshape(depth, 1, n6))


def _mod_rows(mod_ref, field, per_row, tm):
    if per_row:
        return mod_ref[0, field, pl.ds(pl.multiple_of(pl.program_id(1) * tm, tm), tm), :]
    return mod_ref[0, field, pl.ds(pl.program_id(0), 1), :]


def _inproj_kernel(*refs, with_kv, per_row):
    (x_ref, mod_ref, g1_ref, w_ref, gq_ref, wq_ref, cosq_ref, sinq_ref, gqk_ref, gkv_ref,
     cosk_ref, sink_ref) = refs[:12]
    if with_kv:
        wk_ref, gk_ref, wv_ref = refs[12:15]
        z_ref, xbc_ref, dt_ref, u_ref, lat_ref, kpe_ref, q_ref, k_ref, v_ref = refs[15:]
    else:
        z_ref, xbc_ref, dt_ref, u_ref, lat_ref, kpe_ref, q_ref, kpe128_ref = refs[12:]
    tm, d = x_ref.shape[1:]
    shift, scale = (_mod_rows(mod_ref, f, per_row, tm) for f in (0, 1))
    gqk = gqk_ref[...]
    q_scale = PROMPT_Q_SCALE if with_kv else 1.0
    nsplit = tm // ROW_PIECE if tm % ROW_PIECE == 0 else 1
    hm = tm // nsplit

    def rows_of(a, r):
        return a if a.shape[0] == 1 else a[r:r + hm]

    def project(r):
        h = _rms(x_ref[0, r:r + hm, :], g1_ref[...], d) * (1.0 + rows_of(scale, r)) + rows_of(shift, r)
        return _dot(h.astype(bf16), w_ref[0])

    def finish(r, proj):
        rs = slice(r, r + hm)
        z_ref[0, rs, :] = proj[:, _C_Z:_C_XBC]
        xbc_ref[0, rs, :] = proj[:, _C_XBC:_C_CQ]
        u_ref[0, rs, :] = proj[:, _C_U:_C_KA]
        dt_ref[0, rs, :] = proj[:, _C_DT:_C_END]
        cqn = _rms(proj[:, _C_CQ:_C_CKV], gq_ref[...], Q_LORA).astype(bf16)
        qq = _dot(cqn, wq_ref[0])
        cosq = rows_of(cosq_ref[...], r)
        sinq = rows_of(sinq_ref[...], r)
        for hh in range(NH):
            qh = qq[:, LANE * hh:LANE * (hh + 1)] * cosq + qq[:, NH * LANE + LANE * hh:NH * LANE + LANE * (hh + 1)] * sinq
            q_ref[0, rs, LANE * hh:LANE * (hh + 1)] = (_rms(qh, gqk, QK) * q_scale).astype(q_ref.dtype)
        lat = _rms(proj[:, _C_CKV:_C_U], gkv_ref[...], KV_LORA)
        lat_ref[0, rs, :] = lat
        kper = proj[:, _C_KA:_C_KB] * rows_of(cosk_ref[...], r) + proj[:, _C_KB:_C_DT] * rows_of(sink_ref[...], r)
        kpe_ref[0, rs, :] = kper[:, :ROPE]
        if with_kv:
            latb = lat.astype(bf16)
            kn = _dot(latb, wk_ref[...])
            kp = pltpu.roll(kper, NOPE, 1)
            gk = gk_ref[...]
            for hh in range(NH):
                kh = kn[:, LANE * hh:LANE * (hh + 1)] + kp
                k_ref[0, rs, LANE * hh:LANE * (hh + 1)] = _rms(kh, gk, QK).astype(bf16)
            v_ref[0, :, rs] = _dot_nt(wv_ref[...], latb).astype(bf16)
        else:
            kpe128_ref[0, rs, :] = kper

    nxt = project(0)
    for i in range(nsplit):
        proj = nxt
        if i + 1 < nsplit:
            nxt = project((i + 1) * hm)
        finish(i * hm, proj)


def _inproj(x, mod, wts, tabs, *, li, tm, with_kv, per_row):
    g_, t_, d = x.shape
    nt = t_ // tm
    cosq, sinq, cosk, sink = tabs
    tab_rows = cosq.shape[0] != 1
    tab_spec = pl.BlockSpec((tm if tab_rows else 1, LANE), (lambda g, t: (t, 0)) if tab_rows else (lambda g, t: (0, 0)))

    def full(a):
        return pl.BlockSpec(a.shape, lambda g, t: (0,) * a.ndim)

    def layer(a):
        return pl.BlockSpec((1,) + a.shape[1:], lambda g, t: (li,) + (0,) * (a.ndim - 1))

    def row(c):
        return pl.BlockSpec((1, tm, c), lambda g, t: (g, t, 0))

    ins = [x, mod, wts["g1"], wts["w_in"], wts["g_q"], wts["w_q"], cosq, sinq, wts["g_qk_q"], wts["g_kv"], cosk, sink]
    specs = [row(d), layer(mod), full(wts["g1"]), layer(wts["w_in"]), full(wts["g_q"]), layer(wts["w_q"]), tab_spec,
             tab_spec, full(wts["g_qk_q"]), full(wts["g_kv"]), tab_spec, tab_spec]
    widths = [SSD_W, CONV_CH, LANE, POOL_W, KV_LORA, ROPE, NH * LANE]
    dtypes = [f32, f32, f32, f32, f32, f32, bf16 if with_kv else f32]
    if with_kv:
        ins += [wts["w_k"], wts["g_qk_k"], wts["w_v"]]
        specs += [full(wts["w_k"]), full(wts["g_qk_k"]), full(wts["w_v"])]
        widths += [NH * LANE, SSD_W]
        dtypes += [bf16, bf16]
    else:
        widths += [LANE]
        dtypes += [f32]
    out_specs = [row(c) for c in widths]
    out_shape = [jax.ShapeDtypeStruct((g_, t_, c), dt) for c, dt in zip(widths, dtypes)]
    if with_kv:
        out_specs[-1] = pl.BlockSpec((1, SSD_W, tm), lambda g, t: (g, 0, t))
        out_shape[-1] = jax.ShapeDtypeStruct((g_, SSD_W, t_), bf16)
    return pl.pallas_call(
        functools.partial(_inproj_kernel, with_kv=with_kv, per_row=per_row),
        grid=(g_, nt),
        in_specs=specs,
        out_specs=out_specs,
        out_shape=out_shape,
        compiler_params=_cparams(("arbitrary", "arbitrary")),
        name="inproj_kv" if with_kv else "inproj",
    )(*ins)


def _ssd_prompt_kernel(xbc_ref, z_ref, dt_ref, cw_ref, cb_ref, dtb_ref, a_ref, dsk_ref, gn_ref, tril_ref, exp_ref,
                       y_ref, st_ref, ext_ref):
    c = pl.program_id(1)
    blk = xbc_ref.shape[1]
    t_ = SSD_CHUNK

    @pl.when(c == 0)
    def _():
        ext_ref[0:8, :] = jnp.zeros((8, CONV_CH), f32)
        st_ref[...] = jnp.zeros(st_ref.shape, f32)

    ext_ref[8:8 + blk, :] = xbc_ref[0]
    cw = cw_ref[...]
    tril = lax.broadcasted_iota(jnp.int32, (t_, t_), 0) >= lax.broadcasted_iota(jnp.int32, (t_, t_), 1)
    first = lax.broadcasted_iota(jnp.int32, (t_, LANE), 1) < HD
    rfirst = lax.broadcasted_iota(jnp.int32, (LANE, SSD_N), 0) < HD
    expand = exp_ref[...]

    def front(r):
        conv = cb_ref[...] + cw[3:4] * xbc_ref[0, r:r + t_, :]
        for k in range(CONV_K - 1):
            conv = conv + cw[k:k + 1] * ext_ref[5 + k + r:5 + k + r + t_, :]
        act = _silu(conv)
        xs = act[:, :SSD_W]
        bm = [act[:, SSD_W + SSD_N * g:SSD_W + SSD_N * (g + 1)].astype(bf16) for g in range(SSD_G)]
        cm = [act[:, SSD_W + SSD_N * (SSD_G + g):SSD_W + SSD_N * (SSD_G + g + 1)].astype(bf16) for g in range(SSD_G)]
        dt = _softplus(dt_ref[0, r:r + t_, :] + dtb_ref[...])
        a = dt * a_ref[...]
        acum = _dot_sel_left(tril_ref[...], a)
        acum_t = acum.T
        dt_l = _dot_sel_right(dt, expand)
        acum_l = _dot_sel_right(acum, expand)
        e_l = jnp.exp(acum_l)
        xdt = xs * dt_l
        xdt_w = xdt * jnp.exp(acum_l[t_ - 1:t_, :] - acum_l)
        cb = [_dot_nt(cm[g], bm[g]) for g in range(SSD_G)]
        cols = []
        for j in range(NH // 2):
            sl = slice(LANE * j, LANE * (j + 1))
            h0, h1 = 2 * j, 2 * j + 1
            xj = xdt[:, sl].astype(bf16)
            yd = []
            for hh in (h0, h1):
                seg = jnp.where(tril, jnp.exp(acum[:, hh:hh + 1] - acum_t[hh:hh + 1, :]), 0.0)
                yd.append(_dot((cb[hh // (NH // SSD_G)] * seg).astype(bf16), xj))
            tot = jnp.where(rfirst, jnp.exp(acum[t_ - 1:t_, h0:h0 + 1]), jnp.exp(acum[t_ - 1:t_, h1:h1 + 1]))
            cols.append((jnp.where(first, yd[0], yd[1]), xdt_w[:, sl].astype(bf16), tot, e_l[:, sl]))
        return xs, bm, cm, cols, _silu(z_ref[0, r:r + t_, :])

    def back(r, parts):
        xs, bm, cm, cols, gate = parts
        y_cols = []
        for j, (y_diag, xw, tot, e_j) in enumerate(cols):
            sl = slice(LANE * j, LANE * (j + 1))
            g0, g1 = (2 * j) // (NH // SSD_G), (2 * j + 1) // (NH // SSD_G)
            hp = st_ref[0, sl, :]
            hpb = hp.astype(bf16)
            if g0 == g1:
                y_off = _dot_nt(cm[g0], hpb)
                s_new = _dot_tn(xw, bm[g0])
            else:
                y_off = jnp.where(first, _dot_nt(cm[g0], hpb), _dot_nt(cm[g1], hpb))
                s_new = jnp.where(rfirst, _dot_tn(xw, bm[g0]), _dot_tn(xw, bm[g1]))
            st_ref[0, sl, :] = tot * hp + s_new
            y_cols.append(y_diag + y_off * e_j)
        v = (jnp.concatenate(y_cols, axis=-1) + dsk_ref[...] * xs) * gate
        gl = lax.broadcasted_iota(jnp.int32, (t_, SSD_W), 1) < SSD_W // SSD_G
        v2 = v * v
        ss0 = jnp.sum(jnp.where(gl, v2, 0.0), axis=-1, keepdims=True)
        ss1 = jnp.sum(jnp.where(gl, 0.0, v2), axis=-1, keepdims=True)
        gw = SSD_W // SSD_G
        rinv = jnp.where(gl, lax.rsqrt(ss0 * (1.0 / gw) + EPS), lax.rsqrt(ss1 * (1.0 / gw) + EPS))
        y_ref[0, r:r + t_, :] = v * rinv * gn_ref[...]

    fronts = [front(r) for r in range(0, blk, t_)]
    for i, parts in enumerate(fronts):
        back(i * t_, parts)
    ext_ref[0:8, :] = xbc_ref[0, blk - 8:, :]


def _ssd_prompt(xbc, z, dt, wts, *, blk=4 * SSD_CHUNK):
    b_, l_, _ = xbc.shape
    t_ = SSD_CHUNK
    blk = blk if l_ % blk == 0 else t_
    nc = l_ // blk

    def full(a):
        return pl.BlockSpec(a.shape, lambda b, c: (0,) * a.ndim)

    def row(w):
        return pl.BlockSpec((1, blk, w), lambda b, c: (b, c, 0))

    small = [wts["conv_w"], wts["conv_b"], wts["dt_bias"], wts["a_neg"], wts["d_skip_l"], wts["g_ssd"],
             jnp.tril(jnp.ones((t_, t_), bf16)), _head_expand()]
    return pl.pallas_call(
        _ssd_prompt_kernel,
        grid=(b_, nc),
        in_specs=[row(CONV_CH), row(SSD_W), row(LANE)] + [full(a) for a in small],
        out_specs=[row(SSD_W), pl.BlockSpec((1, SSD_W, SSD_N), lambda b, c: (b, 0, 0))],
        out_shape=[jax.ShapeDtypeStruct((b_, l_, SSD_W), f32), jax.ShapeDtypeStruct((b_, SSD_W, SSD_N), f32)],
        scratch_shapes=[pltpu.VMEM((8 + blk, CONV_CH), f32)],
        compiler_params=_cparams(("arbitrary", "arbitrary")),
        name="ssd_prompt",
    )(xbc, z, dt, *small)


def _attn_prompt_kernel(q_ref, k_ref, vt_ref, g_ref, o_ref, m_ref, l_ref, acc_ref, *, tq):
    qi = pl.program_id(1)
    m_ref[...] = jnp.full(m_ref.shape, -1e30, f32)
    l_ref[...] = jnp.zeros(l_ref.shape, f32)
    acc_ref[...] = jnp.zeros(acc_ref.shape, f32)

    ones = jnp.ones((16, tq), bf16)
    masked = lax.broadcasted_iota(jnp.int32, (tq, tq), 0) > lax.broadcasted_iota(jnp.int32, (tq, tq), 1)

    def tiles(key_tiles):
        blocks = []
        for j, diagonal in key_tiles:
            off = pl.multiple_of(j * tq, tq)
            blocks.append((k_ref[0, pl.ds(off, tq), :], vt_ref[0, :, pl.ds(off, tq)], diagonal))
        units = [(bi, hh) for bi in range(len(blocks)) for hh in range(NH)]

        def qk(unit):
            bi, hh = unit
            return _dot_nt(blocks[bi][0][:, LANE * hh:LANE * (hh + 1)], q_ref[0, :, LANE * hh:LANE * (hh + 1)])

        pending = [qk(u) for u in units[:QK_AHEAD]]
        for idx, (bi, hh) in enumerate(units):
            s = pending.pop(0)
            if idx + QK_AHEAD < len(units):
                pending.append(qk(units[idx + QK_AHEAD]))
            _, vblk, diagonal = blocks[bi]
            if diagonal:
                s = jnp.where(masked, -1e30, s)
            m_old = m_ref[hh]
            m_new = jnp.maximum(m_old, jnp.max(s, axis=0, keepdims=True))
            alpha = jnp.exp2(m_old - m_new)
            pe = jnp.exp2(s - m_new).astype(bf16)
            m_ref[hh] = m_new
            rows = slice(HD * hh, HD * (hh + 1))
            pv = _dot(jnp.concatenate([vblk[rows, :], ones], axis=0), pe)
            acc_ref[rows, :] = acc_ref[rows, :] * alpha + pv[:HD]
            l_ref[hh] = alpha * l_ref[hh] + pv[HD:HD + 8]

    def pair(j2, carry):
        tiles([(2 * j2, False), (2 * j2 + 1, False)])
        return carry

    lax.fori_loop(0, qi // 2, pair, 0)

    @pl.when(qi % 2 == 1)
    def _():
        tiles([(qi - 1, False), (qi, True)])

    @pl.when(qi % 2 == 0)
    def _():
        tiles([(qi, True)])

    parts = []
    for hh in range(NH):
        parts.append(acc_ref[HD * hh:HD * (hh + 1), :] * (1.0 / l_ref[hh, 0:1, :]))
    o_t = jnp.concatenate(parts, axis=0)
    ms = jnp.sum(o_t * o_t, axis=0, keepdims=True) * (1.0 / SSD_W)
    o_ref[0] = (o_t * lax.rsqrt(ms + EPS)).T * g_ref[...]


def _attn_prompt(q, k, v_t, g_out, *, tq):
    b_, l_, _ = q.shape
    return pl.pallas_call(
        functools.partial(_attn_prompt_kernel, tq=tq),
        grid=(b_, l_ // tq),
        in_specs=[pl.BlockSpec((1, tq, NH * LANE), lambda b, i: (b, i, 0)),
                  pl.BlockSpec((1, l_, NH * LANE), lambda b, i: (b, 0, 0)),
                  pl.BlockSpec((1, SSD_W, l_), lambda b, i: (b, 0, 0)),
                  pl.BlockSpec(g_out.shape, lambda b, i: (0, 0))],
        out_specs=pl.BlockSpec((1, tq, SSD_W), lambda b, i: (b, i, 0)),
        out_shape=jax.ShapeDtypeStruct((b_, l_, SSD_W), f32),
        scratch_shapes=[pltpu.VMEM((NH, 1, tq), f32), pltpu.VMEM((NH, 8, tq), f32), pltpu.VMEM((SSD_W, tq), f32)],
        compiler_params=_cparams(("arbitrary", "arbitrary")),
        name="attn_prompt",
    )(q, k, v_t, g_out)


def _pool_select(lane, a, b, c, d):
    return jnp.where(lane < 64, a, jnp.where(lane < 128, b, jnp.where(lane < 192, c, d)))


def _pool_prompt_kernel(u_ref, w_ref, sc_ref, o_ref, e1, e2, e4, e8, *, rt):
    t_ = u_ref.shape[1]
    hist = 16
    for e in (e1, e2, e4, e8):
        e[0:hist, :] = jnp.zeros((hist, POOL_W), f32)
    e1[hist:hist + t_, :] = u_ref[0]
    lane = lax.broadcasted_iota(jnp.int32, (rt, POOL_W), 1)
    win = _pool_select(lane, 2, 4, 8, 16)
    for i in range(t_ // rt):
        r0 = hist + i * rt
        a = e1[r0:r0 + rt, :]
        s2 = a + e1[r0 - 1:r0 - 1 + rt, :]
        e2[r0:r0 + rt, :] = s2
        s4 = s2 + e2[r0 - 2:r0 - 2 + rt, :]
        e4[r0:r0 + rt, :] = s4
        s8 = s4 + e4[r0 - 4:r0 - 4 + rt, :]
        e8[r0:r0 + rt, :] = s8
        s16 = s8 + e8[r0 - 8:r0 - 8 + rt, :]
        pos = lax.broadcasted_iota(jnp.int32, (rt, POOL_W), 0) + i * rt
        cnt = jnp.minimum(pos + 1, win).astype(f32)
        pooled = _pool_select(lane, s2, s4, s8, s16) / cnt - a
        o_ref[0, i * rt:(i + 1) * rt, :] = _dot(pooled.astype(bf16), w_ref[...]) * sc_ref[...]


def _pool_prompt(u, w_bd, scale):
    b_, l_, _ = u.shape
    return pl.pallas_call(
        functools.partial(_pool_prompt_kernel, rt=256),
        grid=(b_,),
        in_specs=[pl.BlockSpec((1, l_, POOL_W), lambda b: (b, 0, 0)),
                  pl.BlockSpec(w_bd.shape, lambda b: (0, 0)),
                  pl.BlockSpec(scale.shape, lambda b: (0, 0))],
        out_specs=pl.BlockSpec((1, l_, POOL_W), lambda b: (b, 0, 0)),
        out_shape=jax.ShapeDtypeStruct((b_, l_, POOL_W), f32),
        scratch_shapes=[pltpu.VMEM((16 + l_, POOL_W), f32)] * 4,
        compiler_params=_cparams(("arbitrary",)),
        name="pool_prompt",
    )(u, w_bd, scale)


def _pool_sample_kernel(u_ref, buf_ref, w_ref, sc_ref, o_ref):
    u = u_ref[...]
    run = u
    sums = {}
    for i in range(1, 16):
        run = run + buf_ref[POOL_BUF - i]
        if i + 1 in (2, 4, 8, 16):
            sums[i + 1] = run * (1.0 / (i + 1))
    lane = lax.broadcasted_iota(jnp.int32, u.shape, 1)
    pooled = _pool_select(lane, sums[2], sums[4], sums[8], sums[16]) - u
    o_ref[...] = _dot(pooled.astype(bf16), w_ref[...]) * sc_ref[...]


def _pool_sample(u, buf_t, w_bd, scale):
    return pl.pallas_call(
        _pool_sample_kernel,
        out_shape=jax.ShapeDtypeStruct(u.shape, f32),
        compiler_params=pltpu.CompilerParams(vmem_limit_bytes=VMEM_LIMIT),
        name="pool_sample",
    )(u, buf_t, w_bd, scale)


def _ffn_kernel(x_ref, ssd_ref, mla_ref, pool_ref, mod_ref, gn_ref, wo_ref, wg_ref, wu_ref, wd_ref, o_ref,
                *, nch, per_row):
    tm, d = x_ref.shape[1:]
    gate1, shift2, scale2, gate2 = (_mod_rows(mod_ref, f, per_row, tm) for f in (2, 3, 4, 5))
    hc = wg_ref.shape[2] // nch
    nsplit = tm // ROW_PIECE if tm % ROW_PIECE == 0 else 1
    hm = tm // nsplit

    def rows_of(a, r):
        return a if a.shape[0] == 1 else a[r:r + hm]

    def out_proj(r):
        rs = slice(r, r + hm)
        mixed = jnp.concatenate([ssd_ref[0, rs, :].astype(bf16), mla_ref[0, rs, :].astype(bf16),
                                 pool_ref[0, rs, :].astype(bf16)], axis=-1)
        return _dot(mixed, wo_ref[0])

    def residual_norm(r, mix):
        x1 = x_ref[0, r:r + hm, :] + rows_of(gate1, r) * mix
        return x1, (_rms(x1, gn_ref[...], d) * (1.0 + rows_of(scale2, r)) + rows_of(shift2, r)).astype(bf16)

    def gate_up(h2, c):
        return _dot(h2, wg_ref[0, :, hc * c:hc * (c + 1)]), _dot(h2, wu_ref[0, :, hc * c:hc * (c + 1)])

    mixes = [out_proj(i * hm) for i in range(nsplit)]
    x1, h2 = residual_norm(0, mixes[0])
    for i in range(nsplit):
        r = i * hm
        acc = jnp.zeros(x1.shape, f32)
        nxt = gate_up(h2, 0)
        if i + 1 < nsplit:
            x1_next, h2_next = residual_norm(r + hm, mixes[i + 1])
        for c in range(nch):
            gate, up = nxt
            if c + 1 < nch:
                nxt = gate_up(h2, c + 1)
            acc = acc + _dot((_silu(gate) * up).astype(bf16), wd_ref[0, hc * c:hc * (c + 1), :])
        o_ref[0, r:r + hm, :] = x1 + rows_of(gate2, r) * acc
        if i + 1 < nsplit:
            x1, h2 = x1_next, h2_next


def _ffn(x, ssd, mla, pool, mod, wts, *, li, tm, per_row):
    g_, t_, d = x.shape

    def layer(a):
        return pl.BlockSpec((1,) + a.shape[1:], lambda g, t: (li,) + (0,) * (a.ndim - 1), pipeline_mode=pl.Buffered(1))

    def row(c):
        return pl.BlockSpec((1, tm, c), lambda g, t: (g, t, 0))

    ws = [wts["w_out"], wts["w_gate"], wts["w_up"], wts["w_down"]]
    return pl.pallas_call(
        functools.partial(_ffn_kernel, nch=11, per_row=per_row),
        grid=(g_, t_ // tm),
        in_specs=[row(d), row(SSD_W), row(SSD_W), row(POOL_W), layer(mod),
                  pl.BlockSpec(wts["g2"].shape, lambda g, t: (0, 0))] + [layer(a) for a in ws],
        out_specs=row(d),
        out_shape=jax.ShapeDtypeStruct((g_, t_, d), f32),
        compiler_params=_cparams(("arbitrary", "arbitrary")),
        name="outproj_ffn",
    )(x, ssd, mla, pool, mod, wts["g2"], *ws)


def _ssd_sample_pre_kernel(xbc_ref, buf_ref, dt_ref, cw_ref, cb_ref, dtb_ref, a_ref,
                           xs_ref, b_ref, c_ref, xdt_t_ref, da_ref):
    cw = cw_ref[...]
    conv = cb_ref[...] + cw[3:4] * xbc_ref[...]
    for k in range(CONV_K - 1):
        conv = conv + cw[k:k + 1] * buf_ref[k]
    act = _silu(conv)
    xs = act[:, :SSD_W]
    xs_ref[...] = xs
    b_ref[...] = act[:, SSD_W:SSD_W + SSD_G * SSD_N]
    c_ref[...] = act[:, SSD_W + SSD_G * SSD_N:]
    dt = _softplus(dt_ref[...] + dtb_ref[...])
    da_ref[...] = jnp.exp(dt * a_ref[...])
    xdt = xs * _dot_sel_right(dt, _head_expand())
    for j in range(SSD_W // LANE):
        xdt_t_ref[LANE * j:LANE * (j + 1), :] = xdt[:, LANE * j:LANE * (j + 1)].T


def _ssd_sample_pre(xbc, buf_t, dt, wts):
    b_ = xbc.shape[0]
    shapes = [(b_, SSD_W), (b_, SSD_G * SSD_N), (b_, SSD_G * SSD_N), (SSD_W, b_), (b_, LANE)]
    return pl.pallas_call(
        _ssd_sample_pre_kernel,
        out_shape=[jax.ShapeDtypeStruct(s, f32) for s in shapes],
        compiler_params=pltpu.CompilerParams(vmem_limit_bytes=VMEM_LIMIT),
        name="ssd_sample_pre",
    )(xbc, buf_t, dt, wts["conv_w"], wts["conv_b"], wts["dt_bias"], wts["a_neg"])


def _ssd_sample_state_kernel(da_ref, st_ref, xdt_t_ref, b_ref, c_ref, new_ref, y_t_ref, *, tb):
    i = pl.program_id(0)

    @pl.when(i == 0)
    def _():
        y_t_ref[...] = jnp.zeros(y_t_ref.shape, f32)

    nb = xdt_t_ref.shape[1]
    lane = lax.broadcasted_iota(jnp.int32, (HD, nb), 1)
    pairs = [(hh, bl) for hh in range(NH) for bl in range(tb)]
    sels = [lane == i * tb + bl for bl in range(tb)]
    cols = {}
    for hh, bl in pairs:
        cols[hh, bl] = jnp.sum(jnp.where(sels[bl], xdt_t_ref[HD * hh:HD * (hh + 1), :], 0.0), axis=1, keepdims=True)
    ycols = {}
    for hh, bl in pairs:
        g = hh // (NH // SSD_G)
        rs = slice(HD * hh, HD * (hh + 1))
        brow = b_ref[bl:bl + 1, SSD_N * g:SSD_N * (g + 1)]
        crow = c_ref[bl:bl + 1, SSD_N * g:SSD_N * (g + 1)]
        new = da_ref[i * tb + bl, hh] * st_ref[bl, rs, :] + cols[hh, bl] * brow
        new_ref[bl, rs, :] = new
        ycols[hh, bl] = jnp.sum(new * crow, axis=1, keepdims=True)
    for hh in range(NH):
        rs = slice(HD * hh, HD * (hh + 1))
        y_rows = y_t_ref[rs, :]
        for bl in range(tb):
            y_rows = jnp.where(sels[bl], ycols[hh, bl], y_rows)
        y_t_ref[rs, :] = y_rows


def _ssd_sample_state(da, state_all, xdt_t, bm, cm, *, li, tb=8):
    b_ = xdt_t.shape[1]
    nblk = b_ // tb
    return pl.pallas_call(
        functools.partial(_ssd_sample_state_kernel, tb=tb),
        grid=(nblk,),
        in_specs=[pl.BlockSpec(memory_space=pltpu.SMEM),
                  pl.BlockSpec((tb, SSD_W, SSD_N), lambda i: (li * nblk + i, 0, 0)),
                  pl.BlockSpec(xdt_t.shape, lambda i: (0, 0)),
                  pl.BlockSpec((tb, SSD_G * SSD_N), lambda i: (i, 0)),
                  pl.BlockSpec((tb, SSD_G * SSD_N), lambda i: (i, 0))],
        out_specs=[pl.BlockSpec((tb, SSD_W, SSD_N), lambda i: (i, 0, 0)),
                   pl.BlockSpec((SSD_W, b_), lambda i: (0, 0))],
        out_shape=[jax.ShapeDtypeStruct((b_, SSD_W, SSD_N), f32), jax.ShapeDtypeStruct((SSD_W, b_), f32)],
        compiler_params=_cparams(("arbitrary",)),
        name="ssd_sample_state",
    )(da, state_all, xdt_t, bm, cm)


def _ssd_sample_post_kernel(y_ref, xs_ref, z_ref, dsk_ref, gn_ref, o_ref):
    y = y_ref[...] + dsk_ref[...] * xs_ref[...]
    v = y * _silu(z_ref[...])
    gl = lax.broadcasted_iota(jnp.int32, v.shape, 1) < SSD_W // SSD_G
    v2 = v * v
    gw = SSD_W // SSD_G
    ss0 = jnp.sum(jnp.where(gl, v2, 0.0), axis=-1, keepdims=True)
    ss1 = jnp.sum(jnp.where(gl, 0.0, v2), axis=-1, keepdims=True)
    rinv = jnp.where(gl, lax.rsqrt(ss0 * (1.0 / gw) + EPS), lax.rsqrt(ss1 * (1.0 / gw) + EPS))
    o_ref[...] = v * rinv * gn_ref[...]


def _ssd_sample_post(y, xs, z, wts):
    return pl.pallas_call(
        _ssd_sample_post_kernel,
        out_shape=jax.ShapeDtypeStruct(y.shape, f32),
        name="ssd_sample_post",
    )(y, xs, z, wts["d_skip_l"], wts["g_ssd"])


def _mla_sample_pre_kernel(q_ref, lat_ref, kpe_ref, wk_ref, wkt_ref, gk_ref, qabs_ref, qpe_ref, snew_ref):
    kn = _dot(lat_ref[...].astype(bf16), wk_ref[...])
    kp = pltpu.roll(kpe_ref[...], NOPE, 1)
    gk = gk_ref[...]
    lane = lax.broadcasted_iota(jnp.int32, snew_ref.shape, 1)
    snew = jnp.zeros(snew_ref.shape, f32)
    for hh in range(NH):
        sl = slice(LANE * hh, LANE * (hh + 1))
        qh = q_ref[:, sl]
        knew = _rms(kn[:, sl] + kp, gk, QK)
        snew = jnp.where(lane == hh, jnp.sum(qh * knew, axis=-1, keepdims=True) * ATTN_SCALE, snew)
        qg = qh * gk
        qabs_ref[:, sl] = _dot_exact(qg, wkt_ref[hh])
        qpe_ref[:, sl] = pltpu.roll(qg, NOPE, 1)
    snew_ref[...] = snew


def _mla_sample_pre(q, lat, kpe128, wts):
    b_ = q.shape[0]
    shapes = [(b_, NH * LANE), (b_, NH * LANE), (b_, LANE)]
    return pl.pallas_call(
        _mla_sample_pre_kernel,
        out_shape=[jax.ShapeDtypeStruct(s, f32) for s in shapes],
        name="mla_sample_pre",
    )(q, lat, kpe128, wts["w_k"], wts["w_kt"], wts["g_qk_k"])


def _attn_sample_kernel(pt_ref, lat_hbm, kpe_hbm, wkt_ref, qabs_ref, qpe_ref, snew_ref, latnew_ref, o_ref,
                        lat_buf, kpe_buf, sem, wext, latb_ref, s_ref, *, li, pages, nchunk):
    b = pl.program_id(0)
    nb = pl.num_programs(0)
    rc = pages * PAGE
    slot = b % 2

    def aligned(idx, size):
        return idx * size if isinstance(idx, int) else pl.multiple_of(idx * size, size)

    def copies(bb, c, i, sl):
        page_idx = c * pages + i
        page = pt_ref[bb, page_idx]
        off = aligned(page_idx, PAGE)
        return (pltpu.make_async_copy(lat_hbm.at[li, page], lat_buf.at[sl, pl.ds(off, PAGE), :], sem.at[0, sl, c]),
                pltpu.make_async_copy(kpe_hbm.at[li, page], kpe_buf.at[sl, :, pl.ds(off, PAGE)], sem.at[1, sl, c]))

    def start_chunk(bb, c, sl):
        for i in range(pages):
            for cp in copies(bb, c, i, sl):
                cp.start(priority=i % 2)

    @pl.when(b == 0)
    def _():
        def issue(c, carry):
            start_chunk(0, c, 0)
            start_chunk(jnp.minimum(1, nb - 1), c, 1)
            return carry
        lax.fori_loop(0, nchunk, issue, 0)
        wext[0:NH * HD, :] = wkt_ref[...]

    wext[NH * HD:NH * HD + HPAD, :] = qabs_ref[0]
    qpe = qpe_ref[0]
    rowi = lax.broadcasted_iota(jnp.int32, (HPAD, rc), 0)

    for c in range(nchunk):
        for i in range(pages):
            for cp in copies(b, c, i, slot):
                cp.wait()

    def project(c):
        latb = lat_buf[slot, c * rc:(c + 1) * rc, :].astype(bf16)
        latb_ref[c * rc:(c + 1) * rc, :] = latb
        return _dot_nt(wext[...], latb)

    def scores(c, a):
        kpe = kpe_buf[slot, :, c * rc:(c + 1) * rc]
        nsq = jnp.zeros((HPAD, rc), f32)
        for hh in range(NH):
            kh = a[HD * hh:HD * (hh + 1), :]
            nsq = jnp.where(rowi == hh, jnp.sum(kh * kh, axis=0, keepdims=True), nsq)
        s_pe = _dot(qpe, kpe.astype(bf16))
        ksq = jnp.sum(kpe * kpe, axis=0, keepdims=True)
        s_ref[:, c * rc:(c + 1) * rc] = ((a[NH * HD:, :] + s_pe)
                                         * lax.rsqrt((nsq + ksq) * (1.0 / QK) + EPS) * ATTN_SCALE)

    a_next = project(0)
    for c in range(nchunk):
        a = a_next
        if c + 1 < nchunk:
            a_next = project(c + 1)
        scores(c, a)

    b_ahead = jnp.minimum(b + 2, nb - 1)
    for c in range(nchunk):
        start_chunk(b_ahead, c, slot)

    s = s_ref[...]
    s_new = snew_ref[0][:, 0:1]
    m = jnp.maximum(jnp.max(s, axis=-1, keepdims=True), s_new)
    pe = jnp.exp(s - m)
    p_new = jnp.exp(s_new - m)
    l = jnp.sum(pe, axis=-1, keepdims=True) + p_new
    acc = _dot(pe.astype(bf16), latb_ref[...]) + p_new * latnew_ref[0]
    o_ref[0] = acc / l

    @pl.when(b == nb - 1)
    def _():
        def drain(c, carry):
            for sl in range(2):
                for i in range(pages):
                    for cp in copies(b, c, i, sl):
                        cp.wait()
            return carry
        lax.fori_loop(0, nchunk, drain, 0)


def _attn_sample(page_table, cache_lat, cache_kpe_t, wkt, qabs, qpe, snew, lat_new, *, li, pages=SAMPLE_PAGES):
    b_, npages = page_table.shape
    nchunk = npages // pages
    assert npages % pages == 0
    seq = npages * PAGE
    grid_spec = pltpu.PrefetchScalarGridSpec(
        num_scalar_prefetch=1,
        grid=(b_,),
        in_specs=[pl.BlockSpec(memory_space=pl.ANY),
                  pl.BlockSpec(memory_space=pl.ANY),
                  pl.BlockSpec(wkt.shape, lambda b, pt: (0, 0)),
                  pl.BlockSpec((1, HPAD, KV_LORA), lambda b, pt: (b, 0, 0)),
                  pl.BlockSpec((1, HPAD, ROPE), lambda b, pt: (b, 0, 0)),
                  pl.BlockSpec((1, HPAD, LANE), lambda b, pt: (b, 0, 0)),
                  pl.BlockSpec((1, 1, KV_LORA), lambda b, pt: (b, 0, 0))],
        out_specs=pl.BlockSpec((1, HPAD, KV_LORA), lambda b, pt: (b, 0, 0)),
        scratch_shapes=[pltpu.VMEM((2, seq, KV_LORA), f32),
                        pltpu.VMEM((2, ROPE, seq), f32),
                        pltpu.SemaphoreType.DMA((2, 2, nchunk)),
                        pltpu.VMEM((NH * HD + HPAD, KV_LORA), bf16),
                        pltpu.VMEM((seq, KV_LORA), bf16),
                        pltpu.VMEM((HPAD, seq), f32)],
    )
    return pl.pallas_call(
        functools.partial(_attn_sample_kernel, li=li, pages=pages, nchunk=nchunk),
        grid_spec=grid_spec,
        out_shape=jax.ShapeDtypeStruct((b_, HPAD, KV_LORA), f32),
        compiler_params=_cparams(("arbitrary",)),
        name="attn_sample",
    )(page_table, cache_lat, cache_kpe_t, wkt, qabs, qpe, snew, lat_new)


def _mla_sample_post_kernel(o_ref, wv_ref, g_ref, out_ref):
    out_ref[...] = _rms(_dot(o_ref[...].astype(bf16), wv_ref[...]), g_ref[...], SSD_W)


def _mla_sample_post(o_lat, w_v_bd, g_out):
    return pl.pallas_call(
        _mla_sample_post_kernel,
        out_shape=jax.ShapeDtypeStruct((o_lat.shape[0], SSD_W), f32),
        name="mla_sample_post",
    )(o_lat, w_v_bd, g_out)


def _pad_heads(w, lo, hi):
    pad = [(0, 0)] * (w.ndim - 1) + [(lo, LANE - hi)]
    w = jnp.pad(w, pad)
    return w.reshape(w.shape[:-2] + (NH * LANE,))


def _rot_cols(w):
    half = ROPE // 2
    return jnp.concatenate([-w[..., half:], w[..., :half]], axis=-1)


def _pad_lanes(a, lo=0):
    return jnp.pad(a, [(0, 0)] * (a.ndim - 1) + [(lo, LANE - lo - a.shape[-1])])


def _stacked_weights(p):
    offs = np.cumsum([SSD_W, CONV_CH, NH, Q_LORA, KV_LORA, ROPE, POOL_W])
    wz, wxbc, wdt, wcq, wckv, wkpe, wu = jnp.split(p["w_in"], offs[:-1].tolist(), axis=2)
    w_in = jnp.concatenate([wz, wxbc, wcq, wckv, wu, _pad_lanes(wkpe), _pad_lanes(_rot_cols(wkpe)), _pad_lanes(wdt)],
                           axis=2).astype(bf16)
    wq = p["w_q_up"]
    w_q = jnp.concatenate([_pad_heads(wq, 0, QK), _pad_heads(_rot_cols(wq[..., NOPE:]), NOPE, QK)], axis=2).astype(bf16)
    return dict(w_in=w_in, w_q=w_q, w_out=p["w_out"].astype(bf16), w_gate=p["w_gate"].astype(bf16),
                w_up=p["w_up"].astype(bf16), w_down=p["w_down"].astype(bf16))


def _layer_weights(p, li):
    d = p["w_in"].shape[1]
    wk = p["w_k_up"][li]
    wk_pad = _pad_heads(wk, 0, NOPE)
    wk_t = jnp.transpose(wk, (1, 2, 0))
    wv = p["w_v_up"][li]
    w_v_bd = jnp.zeros((NH, KV_LORA, NH, HD), f32)
    w_v_bd = w_v_bd.at[jnp.arange(NH), :, jnp.arange(NH), :].set(jnp.transpose(wv, (1, 0, 2)))
    wp = p["w_pool"][li]
    ng = wp.shape[0]
    w_pool_bd = jnp.zeros((ng, HD, ng, HD), f32).at[jnp.arange(ng), :, jnp.arange(ng), :].set(wp)
    return dict(
        g1=p["g_norm1"][li].reshape(1, d),
        g_q=p["g_q_lora"][li].reshape(1, Q_LORA),
        g_qk_q=_pad_lanes(p["g_qk_q"][li].reshape(1, QK)), g_qk_k=_pad_lanes(p["g_qk_k"][li].reshape(1, QK)),
        g_kv=p["g_kv_lora"][li].reshape(1, KV_LORA),
        w_k=wk_pad.astype(bf16), w_v=wv.reshape(KV_LORA, NH * HD).T.astype(bf16),
        w_kt=jnp.pad(wk_t, ((0, 0), (0, LANE - NOPE), (0, 0))),
        w_kt_flat=wk_t.reshape(NH * NOPE, KV_LORA).astype(bf16),
        w_v_bd=w_v_bd.reshape(NH * KV_LORA, NH * HD).astype(bf16),
        g_mla=p["g_mla_out"][li].reshape(1, SSD_W),
        conv_w=p["conv_w"][li], conv_b=p["conv_b"][li].reshape(1, CONV_CH),
        dt_bias=_pad_lanes(p["dt_bias"][li].reshape(1, NH)),
        a_neg=_pad_lanes(-jnp.exp(p["a_log"][li].astype(f32)).reshape(1, NH)),
        d_skip_l=jnp.repeat(p["d_skip"][li], HD).reshape(1, SSD_W), g_ssd=p["g_ssd_norm"][li].reshape(1, SSD_W),
        w_pool=w_pool_bd.reshape(POOL_W, POOL_W).astype(bf16), pool_scale=p["pool_scale"][li].reshape(1, POOL_W),
        g2=p["g_norm2"][li].reshape(1, d),
    )


def _rope_tables(pos):
    half = ROPE // 2
    inv = 1.0 / (ROPE_THETA ** (jnp.arange(half, dtype=f32) / half))
    ang = pos.astype(f32)[:, None] * inv[None, :]
    cos2 = jnp.concatenate([jnp.cos(ang)] * 2, axis=-1)
    sin2 = jnp.concatenate([jnp.sin(ang)] * 2, axis=-1)
    n = pos.shape[0]
    cosq = jnp.concatenate([jnp.ones((n, NOPE), f32), cos2, jnp.zeros((n, LANE - QK), f32)], axis=-1)
    sinq = _pad_lanes(sin2, NOPE)
    return cosq, sinq, _pad_lanes(cos2), _pad_lanes(sin2)


def kernel(x_prompt, x_sample, cache_kv_latent, cache_k_rope, state_ssm, state_conv, state_pool, page_table, c_prompt, c_sample, w_ada, b_ada, g_norm1, w_in, conv_w, conv_b, dt_bias, a_log, d_skip, g_ssd_norm, g_q_lora, w_q_up, g_kv_lora, w_k_up, w_v_up, g_qk_q, g_qk_k, g_mla_out, w_pool, pool_scale, w_out, g_norm2, w_gate, w_up, w_down):
    params = dict(g_norm1=g_norm1, w_in=w_in, conv_w=conv_w, conv_b=conv_b, dt_bias=dt_bias, a_log=a_log, d_skip=d_skip,
                  g_ssd_norm=g_ssd_norm, g_q_lora=g_q_lora, w_q_up=w_q_up, g_kv_lora=g_kv_lora, w_k_up=w_k_up,
                  w_v_up=w_v_up, g_qk_q=g_qk_q, g_qk_k=g_qk_k, g_mla_out=g_mla_out, w_pool=w_pool, pool_scale=pool_scale,
                  w_out=w_out, g_norm2=g_norm2, w_gate=w_gate, w_up=w_up, w_down=w_down)
    depth = w_ada.shape[0]
    bp, seq, d = x_prompt.shape
    bs = x_sample.shape[0]
    past = page_table.shape[1] * PAGE

    mod_p, mod_s = _ada_mod(c_prompt, c_sample, w_ada, b_ada)
    big = _stacked_weights(params)
    state_all = state_ssm.reshape(depth * bs, SSD_W, SSD_N)
    tabs_p = _rope_tables(jnp.arange(seq, dtype=jnp.int32))
    tabs_s = _rope_tables(jnp.full((1,), past, jnp.int32))
    cache_kpe_t = jnp.swapaxes(cache_k_rope, 2, 3)

    yp = x_prompt
    ys = x_sample.reshape(1, bs, d)
    p_new = [[] for _ in range(5)]
    s_new = [[] for _ in range(5)]
    for li in range(depth):
        wts = dict(_layer_weights(params, li), **big)

        z, xbc, dt, u, lat, kpe, q, k, v = _inproj(yp, mod_p, wts, tabs_p, li=li, tm=min(INPROJ_ROWS, seq), with_kv=True,
                                                   per_row=False)
        ssd_out, h_t = _ssd_prompt(xbc, z, dt, wts)
        mla_out = _attn_prompt(q, k, v, wts["g_mla"], tq=ATTN_TILE)
        pool_out = _pool_prompt(u, wts["w_pool"], wts["pool_scale"])
        yp = _ffn(yp, ssd_out, mla_out, pool_out, mod_p, wts, li=li, tm=FFN_ROWS, per_row=False)
        for lst, val in zip(p_new, (lat, kpe, h_t.reshape(bp, NH, HD, SSD_N), xbc[:, seq - (CONV_K - 1):],
                                    u[:, seq - POOL_BUF:])):
            lst.append(val)

        z, xbc, dt, u, lat, kpe, q, kpe128 = _inproj(ys, mod_s, wts, tabs_s, li=li, tm=bs, with_kv=False, per_row=True)
        z, xbc, dt, u, lat, kpe, q, kpe128 = (a[0] for a in (z, xbc, dt, u, lat, kpe, q, kpe128))
        conv_buf = state_conv[li]
        xs, bm, cm, xdt_t, da = _ssd_sample_pre(xbc, jnp.transpose(conv_buf, (1, 0, 2)), dt, wts)
        h_new, y_t = _ssd_sample_state(da[:, :8], state_all, xdt_t, bm, cm, li=li)
        ssd_out = _ssd_sample_post(y_t.T, xs, z, wts)
        qabs, qpe, snew = _mla_sample_pre(q, lat, kpe128, wts)
        qabs = jnp.pad(qabs.reshape(bs, NH, LANE), ((0, 0), (0, HPAD - NH), (0, 0))).astype(bf16)
        qpe = jnp.pad(qpe.reshape(bs, NH, LANE)[:, :, :ROPE], ((0, 0), (0, HPAD - NH), (0, 0))).astype(bf16)
        snew_b = jnp.broadcast_to(jnp.pad(snew[:, :NH], ((0, 0), (0, HPAD - NH)))[:, :, None], (bs, HPAD, LANE))
        o_lat = _attn_sample(page_table, cache_kv_latent, cache_kpe_t, wts["w_kt_flat"], qabs, qpe, snew_b,
                             lat.reshape(bs, 1, KV_LORA), li=li)
        mla_out = _mla_sample_post(o_lat[:, :NH].reshape(bs, NH * KV_LORA), wts["w_v_bd"], wts["g_mla"])
        pool_buf = state_pool[li]
        pool_out = _pool_sample(u, jnp.transpose(pool_buf, (1, 0, 2)), wts["w_pool"], wts["pool_scale"])
        ys = _ffn(ys, ssd_out[None], mla_out[None], pool_out[None], mod_s, wts, li=li, tm=bs, per_row=True)
        conv_new = jnp.concatenate([conv_buf[:, 1:], xbc[:, None, :]], axis=1)
        pool_new = jnp.concatenate([pool_buf[:, 1:], u[:, None, :]], axis=1)
        for lst, val in zip(s_new, (lat[:, None, :], kpe[:, None, :], h_new.reshape(bs, NH, HD, SSD_N), conv_new, pool_new)):
            lst.append(val)

    outs_p = [jnp.stack(vv, axis=0) for vv in p_new]
    outs_s = [jnp.stack(vv, axis=0) for vv in s_new]
    return (yp, ys.reshape(bs, 1, d), *outs_p, *outs_s)
```
